```python
import math
import jax
import jax.numpy as jnp
from jax import lax
import numpy as np

D_MODEL = 1024
BATCH = 2
SEQ = 8192
DEPTH = 4
DEC_BATCH = 128
DEC_SEQ = 1
PAST_LEN = 2048
PAGE_SIZE = 128

HEAD_DIM = 64
N_HEADS_TOTAL = D_MODEL // HEAD_DIM
N_GROUPS_C = D_MODEL // (4 * HEAD_DIM)
N_HEADS_A = (N_HEADS_TOTAL - N_GROUPS_C) // 2
N_HEADS_B = N_HEADS_TOTAL - N_GROUPS_C - N_HEADS_A
D_A = N_HEADS_A * HEAD_DIM
D_B = N_HEADS_B * HEAD_DIM
D_C = N_GROUPS_C * HEAD_DIM
D_MIX = D_A + D_B + D_C
LORA_W = 64
LORA_A = 64
LORA_G = 128
W_SHIFT = 3 * D_A + LORA_W + LORA_A + LORA_G
N_IN = W_SHIFT + 3 * D_B + 2 * D_C
MOBA_BLOCK = 256
MOBA_TOPK = 3
Q_CHUNK = 64
CHUNK_C = 128
N_BUCKETS = 32
REL_MAX_DIST = 4096
D_FF = 4 * D_MODEL
EPS = 1e-6
LNX_EPS = 64e-5
F32 = jnp.float32

kernel_name = 'hymba_rwkv7_moba_gmlp_decoder_step'


def rms_norm(x, gain, eps=EPS):
    xf = x.astype(F32)
    y = xf * lax.rsqrt(jnp.mean(xf * xf, axis=-1, keepdims=True) + eps)
    return (y * gain.astype(F32)).astype(x.dtype)


def rel_bucket(dist):
    n = jnp.maximum(dist, 0)
    max_exact = N_BUCKETS // 2
    nf = jnp.maximum(n, 1).astype(F32)
    large = max_exact + (jnp.log(nf / max_exact) / math.log(REL_MAX_DIST / max_exact)
                         * (N_BUCKETS - max_exact)).astype(jnp.int32)
    large = jnp.minimum(large, N_BUCKETS - 1)
    return jnp.where(n < max_exact, n, large)


def token_shift(z, prev, mu):
    z_prev = jnp.concatenate([prev[:, None].astype(z.dtype), z[:, :-1]], axis=1)
    return z + mu * (z_prev - z), z[:, -1]


def rwkv7_time_mix(z, shift0, wkv0, p):
    B, T, _ = z.shape
    zs, shift_new = token_shift(z, shift0, p['mu_shift'])
    zs = zs.astype(F32)
    r, k, v, zw, za, zg = jnp.split(
        zs, [D_A, 2 * D_A, 3 * D_A, 3 * D_A + LORA_W, 3 * D_A + LORA_W + LORA_A], axis=-1)
    f = lambda name: p[name].astype(F32)
    w_log = -jax.nn.softplus(-(f('w0') + jnp.tanh(zw) @ f('w2'))) - 0.5
    decay = jnp.exp(-jnp.exp(w_log))
    a = jax.nn.sigmoid(f('a0') + za @ f('a2'))
    g = jax.nn.sigmoid(zg) @ f('g2')
    heads = lambda t: t.reshape(B, T, N_HEADS_A, HEAD_DIM)
    kk = heads(k * f('k_k'))
    kk = kk / jnp.maximum(jnp.linalg.norm(kk, axis=-1, keepdims=True), 1e-12)
    k = k * (1.0 + (a - 1.0) * f('k_a'))
    r, k, v, decay, a = heads(r), heads(k), heads(v), heads(decay), heads(a)

    def step(S, xs_t):
        r_t, w_t, k_t, v_t, kk_t, a_t = xs_t
        sa = jnp.einsum('bhij,bhj->bhi', S, -kk_t)
        S = (S * w_t[:, :, None, :] + sa[..., None] * (kk_t * a_t)[:, :, None, :]
             + v_t[..., None] * k_t[:, :, None, :])
        return S, jnp.einsum('bhij,bhj->bhi', S, r_t)

    seq_first = lambda t: jnp.moveaxis(t, 1, 0)
    wkv_new, y = lax.scan(step, wkv0.astype(F32),
                          tuple(seq_first(t) for t in (r, decay, k, v, kk, a)))
    y = jnp.moveaxis(y, 0, 1)
    mu = jnp.mean(y, axis=-1, keepdims=True)
    var = jnp.mean(jnp.square(y - mu), axis=-1, keepdims=True)
    yn = ((y - mu) * lax.rsqrt(var + LNX_EPS)).reshape(B, T, D_A) * f('lnx_w') + f('lnx_b')
    bonus = jnp.sum(r * k * f('r_k'), axis=-1, keepdims=True) * v
    out = (yn + bonus.reshape(B, T, D_A)) * g
    return out, wkv_new, shift_new


def to_blocks(t):
    B, L, H, D = t.shape
    nb = -(-L // MOBA_BLOCK)
    tp = jnp.pad(t, ((0, 0), (0, nb * MOBA_BLOCK - L), (0, 0), (0, 0)))
    return tp.reshape(B, nb, MOBA_BLOCK, H, D).transpose(0, 3, 1, 2, 4)


def moba_attend(q, q_pos, kb, vb, kmean, bias_table):
    B, H, Q, D = q.shape
    nb = kb.shape[2]
    own = q_pos // MOBA_BLOCK
    gate = jnp.einsum('bhqd,bhnd->bhqn', q, kmean).astype(F32)
    fully_past = jnp.arange(nb)[None, :] < own[:, None]
    gate = jnp.where(fully_past, gate, -jnp.inf)
    _, top = lax.top_k(gate, min(MOBA_TOPK, nb))
    top_valid = top < own[:, None]
    sel = jnp.concatenate([top, jnp.broadcast_to(own[:, None], (B, H, Q, 1))], axis=-1)
    sel_valid = jnp.concatenate([top_valid, jnp.ones((B, H, Q, 1), dtype=bool)], axis=-1)
    bi = jnp.arange(B)[:, None, None, None]
    hi = jnp.arange(H)[None, :, None, None]
    kg = kb[bi, hi, sel]
    vg = vb[bi, hi, sel]
    key_pos = sel[..., None] * MOBA_BLOCK + jnp.arange(MOBA_BLOCK)
    dist = q_pos[:, None, None] - key_pos
    mask = sel_valid[..., None] & (dist >= 0)
    bias = bias_table[hi[..., None], rel_bucket(dist)].astype(F32)
    logits = jnp.einsum('bhqd,bhqskd->bhqsk', q, kg).astype(F32) * (HEAD_DIM ** -0.5) + bias
    logits = jnp.where(mask, logits, -jnp.inf)
    probs = jax.nn.softmax(logits.reshape(B, H, Q, -1), axis=-1).reshape(logits.shape)
    return jnp.einsum('bhqsk,bhqskd->bhqd', probs.astype(vg.dtype), vg)


def moba_attention(q, q_pos, k_all, v_all, rel_bias):
    B, T, H, D = q.shape
    kb, vb = to_blocks(k_all), to_blocks(v_all)
    kmean = jnp.mean(kb.astype(F32), axis=3).astype(kb.dtype)
    bias_table = rel_bias.T
    qh = q.transpose(0, 2, 1, 3)
    if T > Q_CHUNK and T % Q_CHUNK == 0:
        n = T // Q_CHUNK
        qc = qh.reshape(B, H, n, Q_CHUNK, D).transpose(2, 0, 1, 3, 4)
        pc = q_pos.reshape(n, Q_CHUNK)
        out = lax.map(lambda a: moba_attend(a[0], a[1], kb, vb, kmean, bias_table), (qc, pc))
        return out.transpose(1, 0, 3, 2, 4).reshape(B, T, H, D)
    return moba_attend(qh, q_pos, kb, vb, kmean, bias_table).transpose(0, 2, 1, 3)


def gmlp_spatial_gate(u, v, w_s, b_s, v_norm):
    B, T, _ = u.shape
    vg = rms_norm(v.reshape(B, T, N_GROUPS_C, HEAD_DIM), v_norm.reshape(N_GROUPS_C, HEAD_DIM))
    n_chunk = -(-T // CHUNK_C)
    pad = n_chunk * CHUNK_C - T
    vp = jnp.pad(vg, ((0, 0), (0, pad), (0, 0), (0, 0))).reshape(
        B, n_chunk, CHUNK_C, N_GROUPS_C, HEAD_DIM)
    causal = jnp.tril(jnp.ones((CHUNK_C, CHUNK_C), dtype=bool))
    ws = jnp.where(causal[None], w_s, 0)
    mixed = jnp.einsum('gts,bnsgd->bntgd', ws, vp) + b_s.T[None, None, :, :, None]
    mixed = mixed.reshape(B, n_chunk * CHUNK_C, N_GROUPS_C, HEAD_DIM)[:, :T]
    out = u.reshape(B, T, N_GROUPS_C, HEAD_DIM) * mixed
    return out.reshape(B, T, D_C), vg.reshape(B, T, D_C)


def trunk_layer(x, c, p, rel_bias, k_past, v_past, shift0, wkv0):
    B, T, _ = x.shape
    dt = x.dtype
    pos = k_past.shape[1] + jnp.arange(T, dtype=jnp.int32)
    mod = jax.nn.silu(c.astype(F32)) @ p['w_ada'].astype(F32) + p['b_ada'].astype(F32)
    sh1, sc1, g1, sh2, sc2, g2 = jnp.split(mod[:, None, :], 6, axis=-1)
    h = (rms_norm(x, p['norm_mix']).astype(F32) * (1.0 + sc1) + sh1).astype(dt)
    proj = h @ p['w_in']
    z_a = proj[..., :W_SHIFT]
    q, k, v = jnp.split(proj[..., W_SHIFT:W_SHIFT + 3 * D_B], 3, axis=-1)
    uv = proj[..., W_SHIFT + 3 * D_B:]
    out_a, wkv_new, shift_new = rwkv7_time_mix(z_a, shift0, wkv0, p)
    q = rms_norm(q.reshape(B, T, N_HEADS_B, HEAD_DIM), p['q_norm'])
    k = rms_norm(k.reshape(B, T, N_HEADS_B, HEAD_DIM), p['k_norm'])
    v = v.reshape(B, T, N_HEADS_B, HEAD_DIM)
    k_all = jnp.concatenate([k_past.astype(dt), k], axis=1)
    v_all = jnp.concatenate([v_past.astype(dt), v], axis=1)
    out_b = moba_attention(q, pos, k_all, v_all, rel_bias).reshape(B, T, D_B)
    out_b = rms_norm(out_b, p['out_norm_b'])
    u_c, v_c = jnp.split(jax.nn.gelu(uv, approximate=False), 2, axis=-1)
    out_c, v_c_rows = gmlp_spatial_gate(u_c, v_c, p['w_s'], p['b_s'], p['v_norm'])
    out_c = rms_norm(out_c, p['out_norm_c'])
    mix = jnp.concatenate([out_a.astype(dt), out_b, out_c], axis=-1) @ p['w_out']
    x = (x.astype(F32) + g1 * mix.astype(F32)).astype(dt)
    h2 = (rms_norm(x, p['norm_ffn']).astype(F32) * (1.0 + sc2) + sh2).astype(dt)
    ff = jnp.square(jax.nn.relu(h2 @ p['w_up'])) @ p['w_down']
    x = (x.astype(F32) + g2 * ff.astype(F32)).astype(dt)
    return x, k, v, wkv_new, shift_new, v_c_rows


def setup_inputs(seed: int = 0) -> dict:
    key = jax.random.key(seed)
    ks = iter(jax.random.split(key, 40))

    def nrm(shape, scale=1.0):
        return jax.random.normal(next(ks), shape, F32) * scale

    def gain(shape):
        return 1.0 + nrm(shape, 0.02)

    n_pages = PAST_LEN // PAGE_SIZE
    n_used = DEC_BATCH * n_pages
    n_pool = n_used + max(1, n_used // 4)
    perm = jax.random.permutation(next(ks), n_pool)
    page_table = perm[:n_used].reshape(DEC_BATCH, n_pages).astype(jnp.int32)
    return {
        'x_prompt': nrm((BATCH, SEQ, D_MODEL)),
        'x_sample': nrm((DEC_BATCH, DEC_SEQ, D_MODEL)),
        'c_prompt': nrm((BATCH, D_MODEL)),
        'c_sample': nrm((DEC_BATCH, D_MODEL)),
        'cache_k': nrm((DEPTH, n_pool, PAGE_SIZE, N_HEADS_B, HEAD_DIM)),
        'cache_v': nrm((DEPTH, n_pool, PAGE_SIZE, N_HEADS_B, HEAD_DIM)),
        'page_table': page_table,
        'state_wkv': nrm((DEPTH, DEC_BATCH, N_HEADS_A, HEAD_DIM, HEAD_DIM), 0.5),
        'state_shift': nrm((DEPTH, DEC_BATCH, W_SHIFT)),
        'norm_mix': gain((DEPTH, D_MODEL)),
        'w_ada': nrm((DEPTH, D_MODEL, 6 * D_MODEL), 0.5 * D_MODEL ** -0.5),
        'b_ada': nrm((DEPTH, 6 * D_MODEL), 0.02),
        'w_in': nrm((DEPTH, D_MODEL, N_IN), D_MODEL ** -0.5),
        'mu_shift': jax.random.uniform(next(ks), (DEPTH, W_SHIFT), F32, 0.1, 0.9),
        'w0': nrm((DEPTH, D_A), 0.5),
        'w2': nrm((DEPTH, LORA_W, D_A), LORA_W ** -0.5),
        'a0': nrm((DEPTH, D_A), 0.5),
        'a2': nrm((DEPTH, LORA_A, D_A), 0.5 * LORA_A ** -0.5),
        'g2': nrm((DEPTH, LORA_G, D_A), LORA_G ** -0.5),
        'k_k': gain((DEPTH, D_A)),
        'k_a': gain((DEPTH, D_A)),
        'r_k': nrm((DEPTH, N_HEADS_A, HEAD_DIM), 0.1),
        'lnx_w': gain((DEPTH, D_A)),
        'lnx_b': nrm((DEPTH, D_A), 0.02),
        'q_norm': gain((DEPTH, HEAD_DIM)),
        'k_norm': gain((DEPTH, HEAD_DIM)),
        'rel_bias': nrm((N_BUCKETS, N_HEADS_B), 0.5),
        'out_norm_b': gain((DEPTH, D_B)),
        'v_norm': gain((DEPTH, D_C)),
        'w_s': nrm((DEPTH, N_GROUPS_C, CHUNK_C, CHUNK_C), CHUNK_C ** -0.5),
        'b_s': nrm((DEPTH, N_GROUPS_C, CHUNK_C), 0.02),
        'out_norm_c': gain((DEPTH, D_C)),
        'w_out': nrm((DEPTH, D_MIX, D_MODEL), D_MIX ** -0.5),
        'norm_ffn': gain((DEPTH, D_MODEL)),
        'w_up': nrm((DEPTH, D_MODEL, D_FF), D_MODEL ** -0.5),
        'w_down': nrm((DEPTH, D_FF, D_MODEL), D_FF ** -0.5),
    }


def reference(x_prompt, x_sample, c_prompt, c_sample, cache_k, cache_v, page_table, state_wkv,
              state_shift, norm_mix, w_ada, b_ada, w_in, mu_shift, w0, w2, a0, a2, g2, k_k, k_a,
              r_k, lnx_w, lnx_b, q_norm, k_norm, rel_bias, out_norm_b, v_norm, w_s, b_s,
              out_norm_c, w_out, norm_ffn, w_up, w_down):
    layer_w = {
        'norm_mix': norm_mix, 'w_ada': w_ada, 'b_ada': b_ada, 'w_in': w_in,
        'mu_shift': mu_shift, 'w0': w0, 'w2': w2, 'a0': a0, 'a2': a2, 'g2': g2,
        'k_k': k_k, 'k_a': k_a, 'r_k': r_k, 'lnx_w': lnx_w, 'lnx_b': lnx_b,
        'q_norm': q_norm, 'k_norm': k_norm, 'out_norm_b': out_norm_b,
        'v_norm': v_norm, 'w_s': w_s, 'b_s': b_s, 'out_norm_c': out_norm_c,
        'w_out': w_out, 'norm_ffn': norm_ffn, 'w_up': w_up, 'w_down': w_down,
    }
    n_prompt = x_prompt.shape[0]
    n_dec = x_sample.shape[0]
    past_len = page_table.shape[1] * cache_k.shape[2]
    dt = x_prompt.dtype
    xp, xs = x_prompt, x_sample
    kp_l, vp_l, ks_l, vs_l, wp_l, wsa_l, sp_l, ss_l, gs_l = ([] for _ in range(9))
    for l in range(DEPTH):
        p = {name: arr[l] for name, arr in layer_w.items()}
        empty = jnp.zeros((n_prompt, 0, N_HEADS_B, HEAD_DIM), dt)
        xp, k_new, v_new, wkv_new, sh_new, _ = trunk_layer(
            xp, c_prompt, p, rel_bias, empty, empty,
            jnp.zeros((n_prompt, W_SHIFT), dt),
            jnp.zeros((n_prompt, N_HEADS_A, HEAD_DIM, HEAD_DIM), F32))
        kp_l.append(k_new); vp_l.append(v_new); wp_l.append(wkv_new); sp_l.append(sh_new)
        k_past = cache_k[l][page_table].reshape(n_dec, past_len, N_HEADS_B, HEAD_DIM)
        v_past = cache_v[l][page_table].reshape(n_dec, past_len, N_HEADS_B, HEAD_DIM)
        xs, k_new, v_new, wkv_new, sh_new, vc_rows = trunk_layer(
            xs, c_sample, p, rel_bias, k_past, v_past, state_shift[l], state_wkv[l])
        ks_l.append(k_new); vs_l.append(v_new); wsa_l.append(wkv_new); ss_l.append(sh_new)
        gs_l.append(vc_rows)
    return (xp, xs, jnp.stack(kp_l), jnp.stack(vp_l), jnp.stack(ks_l), jnp.stack(vs_l),
            jnp.stack(wp_l), jnp.stack(wsa_l), jnp.stack(sp_l), jnp.stack(ss_l), jnp.stack(gs_l))
```

```python
import functools
import math

import numpy as np
import jax
import jax.numpy as jnp
from jax import lax
from jax.experimental import pallas as pl
from jax.experimental.pallas import tpu as pltpu

F32 = jnp.float32
BF16 = jnp.bfloat16

D_MODEL = 1024
HEAD_DIM = 64
N_HEADS_A = 6
N_HEADS_B = 6
N_GROUPS_C = 4
D_A = N_HEADS_A * HEAD_DIM
D_B = N_HEADS_B * HEAD_DIM
D_C = N_GROUPS_C * HEAD_DIM
LORA_W = 64
LORA_A = 64
LORA_G = 128
W_SHIFT = 3 * D_A + LORA_W + LORA_A + LORA_G
N_IN = W_SHIFT + 3 * D_B + 2 * D_C
IN_SEGS = (0, W_SHIFT, W_SHIFT + D_B, W_SHIFT + 2 * D_B, W_SHIFT + 3 * D_B, N_IN)
MOBA_BLOCK = 256
MOBA_TOPK = 3
CHUNK_C = 128
N_BUCKETS = 32
REL_MAX_DIST = 4096
D_FF = 4 * D_MODEL
EPS = 1e-6
LNX_EPS = 64e-5
WKV_CHUNK = 64
NEG_BIG = -1e30
V7X_VMEM_LIMIT = 56 * 1024 * 1024


def _bucket_saturation_tiles():
    max_exact = N_BUCKETS // 2
    n = np.arange(1, 2 * REL_MAX_DIST, dtype=np.float64)
    large = max_exact + (np.log(n / max_exact) / math.log(REL_MAX_DIST / max_exact)
                         * (N_BUCKETS - max_exact)).astype(np.int64)
    first_sat = int(n[np.argmax(np.minimum(large, N_BUCKETS - 1) == N_BUCKETS - 1)])
    return -(-(first_sat + MOBA_BLOCK) // MOBA_BLOCK) + 1


N_BIAS_TILES = _bucket_saturation_tiles() + 1

_NN = (((1,), (0,)), ((), ()))
_NT = (((1,), (1,)), ((), ()))
_TN = (((0,), (0,)), ((), ()))


def _mm(a, b, dims=_NN):
    return lax.dot_general(a, b, dims, preferred_element_type=F32)


def _split2(x):
    hi = x.astype(BF16)
    lo = (x - hi.astype(F32)).astype(BF16)
    return hi, lo


def _split3(x):
    h1 = x.astype(BF16)
    r1 = x - h1.astype(F32)
    h2 = r1.astype(BF16)
    h3 = (r1 - h2.astype(F32)).astype(BF16)
    return h1, h2, h3


def _dot1(a, b, dims=_NN):
    return _mm(a.astype(BF16), b.astype(BF16), dims)


def _dot3(a, b, dims=_NN):
    ah, al = _split2(a)
    bh, bl = _split2(b)
    return _mm(ah, bh, dims) + (_mm(ah, bl, dims) + _mm(al, bh, dims))


def _dotx(a, e, dims=_NN):
    ah, al = _split2(a)
    return _mm(ah, e, dims) + _mm(al, e, dims)


def _dotx3(a, e, dims=_NN):
    h1, h2, h3 = _split3(a)
    return _mm(h1, e, dims) + (_mm(h2, e, dims) + _mm(h3, e, dims))


def _params(sem):
    return pltpu.CompilerParams(dimension_semantics=sem, vmem_limit_bytes=V7X_VMEM_LIMIT)


def _sigmoid(x):
    return jax.nn.sigmoid(x)


def _head_ones(width):
    i = np.arange(width) // HEAD_DIM
    return jnp.asarray(i[:, None] == i[None, :], dtype=BF16)


def _ada_body(c_ref, w_ref, b_ref, o_ref):
    c = c_ref[...]
    o_ref[0] = _dot3(c * _sigmoid(c), w_ref[0]) + b_ref[0]


def _ada(c_all, w_ada, b_ada):
    depth, d, n = w_ada.shape
    m = c_all.shape[0]
    tn = 1536
    return pl.pallas_call(
        _ada_body,
        grid=(depth, n // tn),
        in_specs=[pl.BlockSpec((m, d), lambda l, j: (0, 0)),
                  pl.BlockSpec((1, d, tn), lambda l, j: (l, 0, j)),
                  pl.BlockSpec((1, 1, tn), lambda l, j: (l, 0, j))],
        out_specs=pl.BlockSpec((1, m, tn), lambda l, j: (l, 0, j)),
        out_shape=jax.ShapeDtypeStruct((depth, m, n), F32),
        compiler_params=_params(("parallel", "parallel")),
        name="ada_mod",
    )(c_all, w_ada, b_ada.reshape(depth, 1, n))


def _rel_bucket(dist):
    n = jnp.maximum(dist, 0)
    max_exact = N_BUCKETS // 2
    nf = jnp.maximum(n, 1).astype(F32)
    large = max_exact + (jnp.log(nf / max_exact) / math.log(REL_MAX_DIST / max_exact)
                         * (N_BUCKETS - max_exact)).astype(jnp.int32)
    large = jnp.minimum(large, N_BUCKETS - 1)
    return jnp.where(n < max_exact, n, large)


def _bias_tiles_body(rb_ref, o_ref):
    h = pl.program_id(0)
    d = pl.program_id(1)
    ti = lax.broadcasted_iota(jnp.int32, (MOBA_BLOCK, MOBA_BLOCK), 0)
    tj = lax.broadcasted_iota(jnp.int32, (MOBA_BLOCK, MOBA_BLOCK), 1)
    dist = d * MOBA_BLOCK + ti - tj
    bucket = _rel_bucket(dist)
    acc = jnp.zeros((MOBA_BLOCK, MOBA_BLOCK), F32)
    for b in range(N_BUCKETS):
        acc = jnp.where(bucket == b, rb_ref[h * N_BUCKETS + b], acc)
    o_ref[0, 0] = jnp.where(dist >= 0, acc, NEG_BIG)


def _bias_tiles(rel_bias):
    rb = rel_bias.T.reshape(-1)
    return pl.pallas_call(
        _bias_tiles_body,
        grid=(N_HEADS_B, N_BIAS_TILES),
        in_specs=[pl.BlockSpec(memory_space=pltpu.SMEM)],
        out_specs=pl.BlockSpec((1, 1, MOBA_BLOCK, MOBA_BLOCK), lambda h, d: (h, d, 0, 0)),
        out_shape=jax.ShapeDtypeStruct((N_HEADS_B, N_BIAS_TILES, MOBA_BLOCK, MOBA_BLOCK), F32),
        compiler_params=_params(("parallel", "parallel")),
        name="bias_tiles",
    )(rb)


def _bias_rows_body(past_len, rb_ref, o_ref):
    rows = o_ref.shape[0]
    r = lax.broadcasted_iota(jnp.int32, (rows, 128), 0)
    bucket = _rel_bucket(past_len - r)
    acc = jnp.zeros((rows, 128), F32)
    for b in range(N_BUCKETS):
        acc = jnp.where(bucket == b, rb_ref[b:b + 1, :], acc)
    o_ref[...] = acc


def _bias_rows(rel_bias, past_len):
    rb = jnp.pad(rel_bias, ((0, 0), (0, 128 - N_HEADS_B)))
    rows = past_len + 8
    return pl.pallas_call(
        functools.partial(_bias_rows_body, past_len),
        out_shape=jax.ShapeDtypeStruct((rows, 128), F32),
        name="bias_rows",
    )(rb)


def _inproj_body(x_ref, mod_ref, gain_ref, w_ref, *out_refs):
    x = x_ref[0]
    d = x.shape[-1]
    y = x * lax.rsqrt(jnp.mean(x * x, axis=-1, keepdims=True) + EPS) * gain_ref[...]
    sh = mod_ref[0, :, 0:d]
    sc = mod_ref[0, :, d:2 * d]
    h = (y * (1.0 + sc) + sh).astype(BF16)
    for ref, a, b in zip(out_refs, IN_SEGS[:-1], IN_SEGS[1:]):
        ref[0] = _mm(h, w_ref[0, :, a:b])


def _inproj(x, mod, gain, w_in_bf, layer, per_row):
    nb, t, d = x.shape
    tm = min(t, 512)
    mod_rows = tm if per_row else 1
    widths = [b - a for a, b in zip(IN_SEGS[:-1], IN_SEGS[1:])]
    return pl.pallas_call(
        _inproj_body,
        grid=(nb, t // tm),
        in_specs=[pl.BlockSpec((1, tm, d), lambda b, i: (b, i, 0)),
                  pl.BlockSpec((1, mod_rows, 6 * d), lambda b, i: (b, i if per_row else 0, 0)),
                  pl.BlockSpec((1, d), lambda b, i: (0, 0)),
                  pl.BlockSpec((1, d, N_IN), lambda b, i: (layer, 0, 0))],
        out_specs=[pl.BlockSpec((1, tm, w), lambda b, i: (b, i, 0)) for w in widths],
        out_shape=[jax.ShapeDtypeStruct((nb, t, w), F32) for w in widths],
        compiler_params=_params(("parallel", "parallel")),
        name="in_proj",
    )(x, mod, gain.reshape(1, d), w_in_bf)


def _rwkv_prep_body(per_row, z_ref, prev_ref, mu_ref, w0_ref, a0_ref, kk_ref, ka_ref, rk_ref,
                    w2a_ref, g2_ref, e_ref, r_o, ld_o, k_o, v_o, kn_o, b_o, g_o, bon_o, *rest):
    z = z_ref[0]
    tm = z.shape[0]
    if per_row:
        zp = prev_ref[0]
    else:
        sh_o, carry = rest
        @pl.when(pl.program_id(1) == 0)
        def _():
            carry[...] = prev_ref[0]
        row = lax.broadcasted_iota(jnp.int32, z.shape, 0)
        zp = jnp.where(row == 0, carry[...], pltpu.roll(z, 1, axis=0))
        carry[...] = z[tm - 1:tm, :]
        sh_o[0] = z[tm - 1:tm, :]
    zs = z + mu_ref[...] * (zp - z)
    r = zs[:, 0:D_A]
    k = zs[:, D_A:2 * D_A]
    v = zs[:, 2 * D_A:3 * D_A]
    zwa = zs[:, 3 * D_A:3 * D_A + LORA_W + LORA_A]
    zg = zs[:, 3 * D_A + LORA_W + LORA_A:]
    lane = lax.broadcasted_iota(jnp.int32, zwa.shape, 1)
    lwa = _dot3(jnp.where(lane < LORA_W, jnp.tanh(zwa), zwa), w2a_ref[...])
    xw = -(w0_ref[...] + lwa[:, :D_A])
    softplus = jnp.maximum(xw, 0.0) + jnp.log1p(jnp.exp(-jnp.abs(xw)))
    w_log = -softplus - 0.5
    a = _sigmoid(a0_ref[...] + lwa[:, D_A:])
    g = _dot3(_sigmoid(zg), g2_ref[...])
    e = e_ref[...]
    kkr = k * kk_ref[...]
    kn = kkr / jnp.maximum(jnp.sqrt(_dotx(kkr * kkr, e)), 1e-12)
    k2 = k * (1.0 + (a - 1.0) * ka_ref[...])
    r_o[0] = r
    ld_o[0] = -jnp.exp(w_log)
    k_o[0] = k2
    v_o[0] = v
    kn_o[0] = kn
    b_o[0] = kn * a
    g_o[0] = g
    bon_o[0] = _dotx(r * k2 * rk_ref[...], e) * v


def _rwkv_prep(z, prev, p, per_row):
    nb, t, w = z.shape
    tm = min(t, 512)
    row = lambda x: x.reshape(1, -1)
    w2a = jnp.zeros((LORA_W + LORA_A, 2 * D_A), F32)
    w2a = w2a.at[:LORA_W, :D_A].set(p['w2']).at[LORA_W:, D_A:].set(p['a2'])
    const = lambda shape: pl.BlockSpec(shape, lambda b, i: (0,) * len(shape))
    tile = lambda width: pl.BlockSpec((1, tm, width), lambda b, i: (b, i, 0))
    prev_spec = tile(w) if per_row else pl.BlockSpec((1, 1, w), lambda b, i: (b, 0, 0))
    out_specs = [tile(D_A)] * 8
    out_shape = [jax.ShapeDtypeStruct((nb, t, D_A), F32)] * 8
    scratch = []
    if not per_row:
        out_specs = out_specs + [pl.BlockSpec((1, 1, w), lambda b, i: (b, 0, 0))]
        out_shape = out_shape + [jax.ShapeDtypeStruct((nb, 1, w), F32)]
        scratch = [pltpu.VMEM((1, w), F32)]
    return pl.pallas_call(
        functools.partial(_rwkv_prep_body, per_row),
        grid=(nb, t // tm),
        in_specs=[tile(w), prev_spec, const((1, w)), const((1, D_A)), const((1, D_A)), const((1, D_A)),
                  const((1, D_A)), const((1, D_A)), const((LORA_W + LORA_A, 2 * D_A)),
                  const((LORA_G, D_A)), const((D_A, D_A))],
        out_specs=out_specs,
        out_shape=out_shape,
        scratch_shapes=scratch,
        compiler_params=_params(("parallel", "arbitrary")),
        name="rwkv_prep",
    )(z, prev, row(p['mu_shift']), row(p['w0']), row(p['a0']), row(p['k_k']), row(p['k_a']),
      row(p['r_k']), w2a, p['g2'], _head_ones(D_A))


def _wkv_chunk_head(s0, kt, rt, kh, bh, kb, bb, vv, gam, tri_s, tri_i, eye):
    c = kt.shape[0]
    lhs = jnp.concatenate([kt, rt], axis=0)
    aa = _dot3(lhs, jnp.concatenate([kh, bh], axis=0), _NT)
    a_kk = jnp.where(tri_s, aa[:c, :c], 0.0)
    a_kb = jnp.where(tri_s, aa[:c, c:], 0.0)
    a_rk = jnp.where(tri_i, aa[c:, :c], 0.0)
    a_rb = jnp.where(tri_i, aa[c:, c:], 0.0)
    x = -a_kb
    inv = eye + x
    span = 2
    while span < c:
        x = _dot3(x, x)
        inv = inv + _dot3(inv, x)
        span *= 2
    ks = _dot3(lhs, s0, _NT)
    av = _dot3(jnp.concatenate([a_kk, a_rk], axis=0), vv)
    u = _dot3(inv, ks[:c] + av[:c])
    y = ks[c:] + av[c:] - _dot3(a_rb, u)
    s1 = s0 * gam + _dot3(jnp.concatenate([vv, -u], axis=0), jnp.concatenate([kb, bb], axis=0), _TN)
    return y, s1


def _wkv_scan_body(r_ref, ld_ref, k_ref, v_ref, kn_ref, b_ref, g_ref, bon_ref, lw_ref, lb_ref,
                   o_ref, s_out_ref, s_sc):
    nb, tb, _ = r_ref.shape
    c = WKV_CHUNK
    step = pl.program_id(0)

    @pl.when(step == 0)
    def _():
        s_sc[...] = jnp.zeros_like(s_sc)

    ri = lax.broadcasted_iota(jnp.int32, (c, c), 0)
    ci = lax.broadcasted_iota(jnp.int32, (c, c), 1)
    tri_s = ri > ci
    tri_i = ri >= ci
    cum = tri_i.astype(BF16)
    eye = (ri == ci).astype(F32)

    def chunk(ic, carry):
        rows = pl.ds(pl.multiple_of(ic * c, c), c)
        for bi in range(nb):
            ld = ld_ref[bi, rows, :]
            l1, l2, l3 = _split3(ld)
            gcum = _mm(cum, l1) + (_mm(cum, l2) + _mm(cum, l3))
            e_in = jnp.exp(gcum)
            e_ex = jnp.exp(gcum - ld)
            e_ng = jnp.exp(-gcum)
            rt = r_ref[bi, rows, :] * e_in
            kt = kn_ref[bi, rows, :] * e_ex
            kh = k_ref[bi, rows, :] * e_ng
            bh = b_ref[bi, rows, :] * e_ng
            gam = e_in[c - 1:c, :]
            kb = kh * gam
            bb = bh * gam
            vv = v_ref[bi, rows, :]
            for h in range(N_HEADS_A):
                sl = slice(h * HEAD_DIM, (h + 1) * HEAD_DIM)
                y, s1 = _wkv_chunk_head(s_sc[bi, h], kt[:, sl], rt[:, sl], kh[:, sl], bh[:, sl],
                                        kb[:, sl], bb[:, sl], vv[:, sl], gam[:, sl], tri_s, tri_i, eye)
                s_sc[bi, h] = s1
                mu = jnp.mean(y, axis=-1, keepdims=True)
                var = jnp.mean(jnp.square(y - mu), axis=-1, keepdims=True)
                yn = (y - mu) * lax.rsqrt(var + LNX_EPS)
                o_ref[bi, rows, sl] = ((yn * lw_ref[:, sl] + lb_ref[:, sl] + bon_ref[bi, rows, sl])
                                       * g_ref[bi, rows, sl])
        return carry

    lax.fori_loop(0, tb // c, chunk, 0)

    @pl.when(step == pl.num_programs(0) - 1)
    def _():
        s_out_ref[...] = s_sc[...]


def _wkv_scan(r, ld, k, v, kn, b, g, bon, lnx_w, lnx_b):
    nb, t, _ = r.shape
    tb = min(t, 256)
    tile = pl.BlockSpec((nb, tb, D_A), lambda i: (0, i, 0))
    vec = pl.BlockSpec((1, D_A), lambda i: (0, 0))
    st = (nb, N_HEADS_A, HEAD_DIM, HEAD_DIM)
    return pl.pallas_call(
        _wkv_scan_body,
        grid=(t // tb,),
        in_specs=[tile] * 8 + [vec, vec],
        out_specs=[tile, pl.BlockSpec(st, lambda i: (0, 0, 0, 0))],
        out_shape=[jax.ShapeDtypeStruct((nb, t, D_A), F32), jax.ShapeDtypeStruct(st, F32)],
        scratch_shapes=[pltpu.VMEM(st, F32)],
        compiler_params=_params(("arbitrary",)),
        name="wkv_scan",
    )(r, ld, k, v, kn, b, g, bon, lnx_w.reshape(1, D_A), lnx_b.reshape(1, D_A))


def _wkv_step_body(s_ref, r_ref, ld_ref, k_ref, v_ref, kn_ref, b_ref, g_ref, bon_ref, lw_ref, lb_ref,
                   o_ref, s_out_ref):
    s0 = s_ref[0]
    ri = lax.broadcasted_iota(jnp.int32, (HEAD_DIM, HEAD_DIM), 0)
    ci = lax.broadcasted_iota(jnp.int32, (HEAD_DIM, HEAD_DIM), 1)
    eye = (ri == ci).astype(F32)
    u = jnp.sum(s0 * kn_ref[...], axis=-1, keepdims=True)
    v_col = jnp.sum(eye * v_ref[...], axis=-1, keepdims=True)
    s1 = s0 * jnp.exp(ld_ref[...]) - u * b_ref[...] + v_col * k_ref[...]
    s_out_ref[0] = s1
    y_col = jnp.sum(s1 * r_ref[...], axis=-1, keepdims=True)
    y = jnp.sum(eye * y_col, axis=-2, keepdims=True)
    mu = jnp.mean(y, axis=-1, keepdims=True)
    var = jnp.mean(jnp.square(y - mu), axis=-1, keepdims=True)
    yn = (y - mu) * lax.rsqrt(var + LNX_EPS)
    o_ref[...] = (yn * lw_ref[...] + lb_ref[...] + bon_ref[...]) * g_ref[...]


def _wkv_step(state_wkv, layer, r, ld, k, v, kn, b, g, bon, lnx_w, lnx_b):
    n = r.shape[0]
    tb = 8
    heads = lambda x: x.reshape(n, N_HEADS_A, 1, HEAD_DIM)
    row = pl.BlockSpec((tb, N_HEADS_A, 1, HEAD_DIM), lambda i: (i, 0, 0, 0))
    vec = pl.BlockSpec((1, N_HEADS_A, 1, HEAD_DIM), lambda i: (0, 0, 0, 0))
    st_in = pl.BlockSpec((1, tb, N_HEADS_A, HEAD_DIM, HEAD_DIM), lambda i: (layer, i, 0, 0, 0))
    st_out = pl.BlockSpec((1, tb, N_HEADS_A, HEAD_DIM, HEAD_DIM), lambda i: (0, i, 0, 0, 0))
    out, s1 = pl.pallas_call(
        _wkv_step_body,
        grid=(n // tb,),
        in_specs=[st_in] + [row] * 8 + [vec, vec],
        out_specs=[row, st_out],
        out_shape=[jax.ShapeDtypeStruct((n, N_HEADS_A, 1, HEAD_DIM), F32),
                   jax.ShapeDtypeStruct((1, n, N_HEADS_A, HEAD_DIM, HEAD_DIM), F32)],
        compiler_params=_params(("parallel",)),
        name="wkv_step",
    )(state_wkv, *[heads(x) for x in (r, ld, k, v, kn, b, g, bon)],
      lnx_w.reshape(1, N_HEADS_A, 1, HEAD_DIM), lnx_b.reshape(1, N_HEADS_A, 1, HEAD_DIM))
    return out.reshape(n, D_A), s1[0]


def _head_tile(x, h, lane):
    pair = x[:, (h // 2) * 128:(h // 2 + 1) * 128]
    if h % 2:
        pair = pltpu.roll(pair, HEAD_DIM, axis=1)
    return jnp.where(lane < HEAD_DIM, pair, 0.0)


def _qk_norm(x, gain, e):
    return x * lax.rsqrt(_dotx(x * x, e) * (1.0 / HEAD_DIM) + EPS) * gain


def _moba_prep_body(q_ref, k_ref, v_ref, qg_ref, kg_ref, e_ref, qh_o, kh_o, vh_o, kn_o, km_o):
    n = pl.program_id(1)
    e = e_ref[...]
    qn = _qk_norm(q_ref[0], qg_ref[...], e)
    kn = _qk_norm(k_ref[0], kg_ref[...], e)
    v = v_ref[0]
    kn_o[0] = kn
    km_o[0, 0] = jnp.mean(kn, axis=0, keepdims=True)
    lane = lax.broadcasted_iota(jnp.int32, (qn.shape[0], 128), 1)
    onehot = jnp.where(lane == HEAD_DIM + n, 1.0, 0.0)
    for h in range(N_HEADS_B):
        qh_o[0, h] = _head_tile(qn, h, lane)
        kh_o[0, h] = (_head_tile(kn, h, lane) + onehot).astype(BF16)
        vh_o[0, h] = _head_tile(v, h, lane).astype(BF16)


def _moba_prep(q, k, v, q_gain, k_gain):
    nb, t, _ = q.shape
    nblk = t // MOBA_BLOCK
    tile = pl.BlockSpec((1, MOBA_BLOCK, D_B), lambda b, i: (b, i, 0))
    vec = pl.BlockSpec((1, D_B), lambda b, i: (0, 0))
    hm = pl.BlockSpec((1, N_HEADS_B, MOBA_BLOCK, 128), lambda b, i: (b, 0, i, 0))
    return pl.pallas_call(
        _moba_prep_body,
        grid=(nb, nblk),
        in_specs=[tile, tile, tile, vec, vec, pl.BlockSpec((D_B, D_B), lambda b, i: (0, 0))],
        out_specs=[hm, hm, hm, tile, pl.BlockSpec((1, 1, 1, D_B), lambda b, i: (b, i, 0, 0))],
        out_shape=[jax.ShapeDtypeStruct((nb, N_HEADS_B, t, 128), F32),
                   jax.ShapeDtypeStruct((nb, N_HEADS_B, t, 128), BF16),
                   jax.ShapeDtypeStruct((nb, N_HEADS_B, t, 128), BF16),
                   jax.ShapeDtypeStruct((nb, t, D_B), F32),
                   jax.ShapeDtypeStruct((nb, nblk, 1, D_B), F32)],
        compiler_params=_params(("parallel", "parallel")),
        name="moba_prep",
    )(q, k, v, jnp.tile(q_gain, N_HEADS_B).reshape(1, D_B), jnp.tile(k_gain, N_HEADS_B).reshape(1, D_B),
      _head_ones(D_B))


def _moba_attn_body(q_ref, k_ref, v_ref, km_ref, tab_ref, o_ref, m_sc, l_sc, acc_sc):
    a = pl.program_id(2)
    qp = q_ref[0, 0]
    tq = qp.shape[0]
    gate = _dot3(qp, km_ref[0, 0], _NT)
    lane = lax.broadcasted_iota(jnp.int32, (tq, 128), 1)
    blk = lane - HEAD_DIM
    valid = jnp.logical_and(blk >= 0, blk < a)
    gate = jnp.where(valid, gate, -jnp.inf)
    keep = blk == a
    for _ in range(MOBA_TOPK):
        top = jnp.max(gate, axis=1, keepdims=True)
        first = jnp.min(jnp.where(gate == top, lane, 1 << 20), axis=1, keepdims=True)
        pick = lane == first
        keep = jnp.logical_or(keep, jnp.logical_and(pick, valid))
        gate = jnp.where(pick, -jnp.inf, gate)
    pen = jnp.where(jnp.logical_or(keep, blk < 0), 0.0, NEG_BIG)
    qa = jnp.where(lane < HEAD_DIM, qp * (HEAD_DIM ** -0.5), pen).astype(BF16)

    m_sc[...] = jnp.full_like(m_sc, -jnp.inf)
    l_sc[...] = jnp.zeros_like(l_sc)
    acc_sc[...] = jnp.zeros_like(acc_sc)

    def body(j, carry):
        rows = pl.ds(pl.multiple_of(j * MOBA_BLOCK, MOBA_BLOCK), MOBA_BLOCK)
        s = _mm(qa, k_ref[0, 0, rows, :], _NT) + tab_ref[0, jnp.minimum(a - j, N_BIAS_TILES - 1)]
        m_old = m_sc[...]
        m_new = jnp.maximum(m_old, jnp.max(s, axis=1, keepdims=True))
        p = jnp.exp(s - m_new)
        alpha = jnp.exp(m_old - m_new)
        l_sc[...] = alpha * l_sc[...] + jnp.sum(p, axis=1, keepdims=True)
        acc_sc[...] = alpha * acc_sc[...] + _mm(p.astype(BF16), v_ref[0, 0, rows, :])
        m_sc[...] = m_new
        return carry

    lax.fori_loop(0, a + 1, body, 0)
    o_ref[0, 0] = acc_sc[...] / l_sc[...]


def _moba_attn(qh, kh, vh, kmean_pad, bias_tiles):
    nb, nh, t, _ = qh.shape
    nq = t // MOBA_BLOCK
    return pl.pallas_call(
        _moba_attn_body,
        grid=(nh, nb, nq),
        in_specs=[pl.BlockSpec((1, 1, MOBA_BLOCK, 128), lambda h, b, a: (b, h, a, 0)),
                  pl.BlockSpec((1, 1, t, 128), lambda h, b, a: (b, h, 0, 0)),
                  pl.BlockSpec((1, 1, t, 128), lambda h, b, a: (b, h, 0, 0)),
                  pl.BlockSpec((1, 1, 128, 128), lambda h, b, a: (b, h, 0, 0)),
                  pl.BlockSpec((1, N_BIAS_TILES, MOBA_BLOCK, MOBA_BLOCK), lambda h, b, a: (h, 0, 0, 0))],
        out_specs=pl.BlockSpec((1, 1, MOBA_BLOCK, 128), lambda h, b, a: (b, h, a, 0)),
        out_shape=jax.ShapeDtypeStruct((nb, nh, t, 128), F32),
        scratch_shapes=[pltpu.VMEM((MOBA_BLOCK, 1), F32), pltpu.VMEM((MOBA_BLOCK, 1), F32),
                        pltpu.VMEM((MOBA_BLOCK, 128), F32)],
        compiler_params=_params(("parallel", "parallel", "arbitrary")),
        name="moba_attn",
    )(qh, kh, vh, kmean_pad, bias_tiles)


def _qk_norm_rows_body(q_ref, k_ref, qg_ref, kg_ref, e_ref, qn_o, kn_o):
    e = e_ref[...]
    qn_o[...] = _qk_norm(q_ref[...], qg_ref[...], e)
    kn_o[...] = _qk_norm(k_ref[...], kg_ref[...], e)


def _qk_norm_rows(q, k, q_gain, k_gain):
    n = q.shape[0]
    return pl.pallas_call(
        _qk_norm_rows_body,
        out_shape=[jax.ShapeDtypeStruct((n, D_B), F32)] * 2,
        name="qk_norm_rows",
    )(q, k, jnp.tile(q_gain, N_HEADS_B).reshape(1, D_B), jnp.tile(k_gain, N_HEADS_B).reshape(1, D_B),
      _head_ones(D_B))


def _moba_paged_body(n_pages, page_size, pt_ref, q_ref, kn_ref, vn_ref, tab_ref, eh_ref, ee_ref, *rest):
    k_pages = rest[:n_pages]
    v_pages = rest[n_pages:2 * n_pages]
    o_ref = rest[2 * n_pages]
    per_blk = MOBA_BLOCK // page_size
    n_blk = n_pages // per_blk
    past_len = n_pages * page_size
    q = q_ref[0]
    eh = eh_ref[...]
    ee = ee_ref[...]
    scale = HEAD_DIM ** -0.5

    def block(pages, n):
        return jnp.concatenate([pages[n * per_blk + i][0, 0] for i in range(per_blk)], axis=0)

    logits = []
    kmeans = []
    for n in range(n_blk):
        kb = block(k_pages, n)
        kmeans.append(jnp.mean(kb, axis=0, keepdims=True))
        logits.append(_dotx(kb * q, eh) * scale + tab_ref[n * MOBA_BLOCK:(n + 1) * MOBA_BLOCK, :])
    gate = _dotx(jnp.concatenate(kmeans, axis=0) * q, eh)
    blk = lax.broadcasted_iota(jnp.int32, gate.shape, 0)
    keep = jnp.zeros(gate.shape, jnp.bool_)
    for _ in range(min(MOBA_TOPK, n_blk + 1)):
        top = jnp.max(gate, axis=0, keepdims=True)
        first = jnp.min(jnp.where(gate == top, blk, 1 << 20), axis=0, keepdims=True)
        pick = blk == first
        keep = jnp.logical_or(keep, pick)
        gate = jnp.where(pick, -jnp.inf, gate)
    pen = jnp.where(keep, 0.0, NEG_BIG)

    own = jnp.broadcast_to(kn_ref[0] * q, (8, D_B))
    lo = (_dotx(own, eh) * scale)[0:1] + tab_ref[past_len:past_len + 1, :]
    m = lo
    for n in range(n_blk):
        logits[n] = logits[n] + pen[n:n + 1, :]
        m = jnp.maximum(m, jnp.max(logits[n], axis=0, keepdims=True))
    po = jnp.exp(lo - m)
    l = po
    acc = _mm(jnp.broadcast_to(po, (8, 128)).astype(BF16), ee)[0:1] * vn_ref[0]
    for n in range(n_blk):
        p = jnp.exp(logits[n] - m)
        l = l + jnp.sum(p, axis=0, keepdims=True)
        acc = acc + jnp.sum(_mm(p.astype(BF16), ee) * block(v_pages, n), axis=0, keepdims=True)
    o_ref[0] = acc / _dotx3(jnp.broadcast_to(l, (8, 128)), ee)[0:1]


def _moba_paged(qn, kn, v, cache_k, cache_v, page_table, layer, bias_rows):
    n, n_pages = page_table.shape
    depth, n_pool, page_size = cache_k.shape[:3]
    ck = cache_k.reshape(depth, n_pool, page_size, D_B)
    cv = cache_v.reshape(depth, n_pool, page_size, D_B)
    lanes = np.arange(D_B) // HEAD_DIM
    eh = jnp.asarray(lanes[:, None] == np.arange(128)[None, :], dtype=BF16)
    ee = jnp.asarray(np.arange(128)[:, None] == lanes[None, :], dtype=BF16)
    row = pl.BlockSpec((1, 1, D_B), lambda b, pt: (b, 0, 0))
    const = lambda shape: pl.BlockSpec(shape, lambda b, pt: (0,) * len(shape))
    page = lambda p: pl.BlockSpec((1, 1, page_size, D_B), lambda b, pt: (layer, pt[b, p], 0, 0))
    r3 = lambda x: x.reshape(n, 1, D_B)
    out = pl.pallas_call(
        functools.partial(_moba_paged_body, n_pages, page_size),
        grid_spec=pltpu.PrefetchScalarGridSpec(
            num_scalar_prefetch=1,
            grid=(n,),
            in_specs=[row, row, row, const(bias_rows.shape), const((D_B, 128)), const((128, D_B))]
                     + [page(p) for p in range(n_pages)] * 2,
            out_specs=row),
        out_shape=jax.ShapeDtypeStruct((n, 1, D_B), F32),
        compiler_params=_params(("arbitrary",)),
        name="moba_paged",
    )(page_table, r3(qn), r3(kn), r3(v), bias_rows, eh, ee, *([ck] * n_pages), *([cv] * n_pages))
    return out.reshape(n, D_B)


def _gmlp_front(uv, vn_gain, e):
    ge = 0.5 * uv * (1.0 + lax.erf(uv * math.sqrt(0.5)))
    u = ge[:, :D_C]
    v = ge[:, D_C:]
    vg = v * lax.rsqrt(_dotx(v * v, e) * (1.0 / HEAD_DIM) + EPS) * vn_gain
    return u, vg


def _rms(x, gain):
    return x * lax.rsqrt(jnp.mean(x * x, axis=-1, keepdims=True) + EPS) * gain


def _gmlp_body(uv_ref, vn_ref, ws_ref, bs_ref, on_ref, e_ref, o_ref):
    u, vg = _gmlp_front(uv_ref[0], vn_ref[...], e_ref[...])
    tm = u.shape[0]
    ri = lax.broadcasted_iota(jnp.int32, (CHUNK_C, CHUNK_C), 0)
    ci = lax.broadcasted_iota(jnp.int32, (CHUNK_C, CHUNK_C), 1)
    group = lax.broadcasted_iota(jnp.int32, (CHUNK_C, D_C), 1) // HEAD_DIM
    ws = [jnp.where(ri >= ci, ws_ref[g], 0.0).astype(BF16) for g in range(N_GROUPS_C)]
    for c in range(tm // CHUNK_C):
        rows = slice(c * CHUNK_C, (c + 1) * CHUNK_C)
        vc = vg[rows].astype(BF16)
        mixed = bs_ref[...]
        for g in range(N_GROUPS_C):
            mixed = mixed + jnp.where(group == g, _mm(ws[g], vc), 0.0)
        o_ref[0, rows, :] = _rms(u[rows] * mixed, on_ref[...])


def _gmlp(uv, p):
    nb, t, _ = uv.shape
    tm = 512
    const = lambda shape: pl.BlockSpec(shape, lambda b, i: (0,) * len(shape))
    bs = jnp.repeat(p['b_s'].T, HEAD_DIM, axis=1)
    return pl.pallas_call(
        _gmlp_body,
        grid=(nb, t // tm),
        in_specs=[pl.BlockSpec((1, tm, 2 * D_C), lambda b, i: (b, i, 0)), const((1, D_C)),
                  const((N_GROUPS_C, CHUNK_C, CHUNK_C)), const((CHUNK_C, D_C)), const((1, D_C)),
                  const((D_C, D_C))],
        out_specs=pl.BlockSpec((1, tm, D_C), lambda b, i: (b, i, 0)),
        out_shape=jax.ShapeDtypeStruct((nb, t, D_C), F32),
        compiler_params=_params(("parallel", "parallel")),
        name="gmlp",
    )(uv, p['v_norm'].reshape(1, D_C), p['w_s'], bs, p['out_norm_c'].reshape(1, D_C), _head_ones(D_C))


def _gmlp_first_pos_body(uv_ref, vn_ref, w00_ref, b00_ref, on_ref, e_ref, o_ref, vg_ref):
    u, vg = _gmlp_front(uv_ref[...], vn_ref[...], e_ref[...])
    vg_ref[...] = vg
    o_ref[...] = _rms(u * (vg * w00_ref[...] + b00_ref[...]), on_ref[...])


def _gmlp_first_pos(uv, p):
    n = uv.shape[0]
    w00 = jnp.repeat(p['w_s'][:, 0, 0], HEAD_DIM).reshape(1, D_C)
    b00 = jnp.repeat(p['b_s'][:, 0], HEAD_DIM).reshape(1, D_C)
    return pl.pallas_call(
        _gmlp_first_pos_body,
        out_shape=[jax.ShapeDtypeStruct((n, D_C), F32)] * 2,
        name="gmlp_first_pos",
    )(uv, p['v_norm'].reshape(1, D_C), w00, b00, p['out_norm_c'].reshape(1, D_C), _head_ones(D_C))


def _mix_ffn_body(head_major, x_ref, oa_ref, ob_ref, oc_ref, mod_ref, nb_ref, nf_ref, wo_ref, wu_ref, wd_ref,
                  o_ref, x1_sc, h2_sc, acc_sc):
    j = pl.program_id(2)
    d = D_MODEL

    @pl.when(j == 0)
    def _():
        mix = _dot1(oa_ref[0], wo_ref[0, 0:D_A, :]) + _dot1(oc_ref[0], wo_ref[0, D_A + D_B:, :])
        if head_major:
            heads = [ob_ref[0, h][:, :HEAD_DIM] for h in range(N_HEADS_B)]
            ss = sum(jnp.sum(o * o, axis=-1, keepdims=True) for o in heads)
            inv = lax.rsqrt(ss * (1.0 / D_B) + EPS)
            for h, o in enumerate(heads):
                lo = D_A + h * HEAD_DIM
                mix = mix + _dot1(o * inv * nb_ref[:, h * HEAD_DIM:(h + 1) * HEAD_DIM],
                                  wo_ref[0, lo:lo + HEAD_DIM, :])
        else:
            mix = mix + _dot1(_rms(ob_ref[0], nb_ref[...]), wo_ref[0, D_A:D_A + D_B, :])
        x1 = x_ref[0] + mod_ref[0, :, 2 * d:3 * d] * mix
        x1_sc[...] = x1
        h2 = _rms(x1, nf_ref[...]) * (1.0 + mod_ref[0, :, 4 * d:5 * d]) + mod_ref[0, :, 3 * d:4 * d]
        h2_sc[...] = h2.astype(BF16)
        acc_sc[...] = jnp.zeros_like(acc_sc)

    up = _mm(h2_sc[...], wu_ref[0])
    acc_sc[...] += _mm(jnp.square(jnp.maximum(up, 0.0)).astype(BF16), wd_ref[0])

    @pl.when(j == pl.num_programs(2) - 1)
    def _():
        o_ref[0] = x1_sc[...] + mod_ref[0, :, 5 * d:6 * d] * acc_sc[...]


def _mix_ffn(x, oa, ob, oc, mod, p, w_out_bf, w_up_bf, w_down_bf, layer, per_row, head_major):
    nb, t, d = x.shape
    tm = min(t, 512)
    tf = 1024
    mod_rows = tm if per_row else 1
    tile = lambda width: pl.BlockSpec((1, tm, width), lambda b, i, j: (b, i, 0))
    const = lambda shape: pl.BlockSpec(shape, lambda b, i, j: (0,) * len(shape))
    if head_major:
        ob_spec = pl.BlockSpec((1, N_HEADS_B, tm, 128), lambda b, i, j: (b, 0, i, 0))
    else:
        ob_spec = tile(D_B)
    return pl.pallas_call(
        functools.partial(_mix_ffn_body, head_major),
        grid=(nb, t // tm, D_FF // tf),
        in_specs=[tile(d), tile(D_A), ob_spec, tile(D_C),
                  pl.BlockSpec((1, mod_rows, 6 * d), lambda b, i, j: (b, i if per_row else 0, 0)),
                  const((1, D_B)), const((1, d)),
                  pl.BlockSpec((1, d, d), lambda b, i, j: (layer, 0, 0)),
                  pl.BlockSpec((1, d, tf), lambda b, i, j: (layer, 0, j)),
                  pl.BlockSpec((1, tf, d), lambda b, i, j: (layer, j, 0))],
        out_specs=tile(d),
        out_shape=jax.ShapeDtypeStruct((nb, t, d), F32),
        scratch_shapes=[pltpu.VMEM((tm, d), F32), pltpu.VMEM((tm, d), BF16), pltpu.VMEM((tm, d), F32)],
        compiler_params=_params(("parallel", "parallel", "arbitrary")),
        name="mix_ffn",
    )(x, oa, ob, oc, mod, p['out_norm_b'].reshape(1, D_B), p['norm_ffn'].reshape(1, d),
      w_out_bf, w_up_bf, w_down_bf)


def _prompt_layer(x, mod, p, big, layer, bias_tiles):
    nb, t, _ = x.shape
    z, q, k, v, uv = _inproj(x, mod, p['norm_mix'], big['w_in'], layer, per_row=False)
    prep = _rwkv_prep(z, jnp.zeros((nb, 1, W_SHIFT), F32), p, per_row=False)
    shift_new = prep[8].reshape(nb, W_SHIFT)
    out_a, wkv_new = _wkv_scan(*prep[:8], p['lnx_w'], p['lnx_b'])
    qh, kh, vh, k_new, kmean = _moba_prep(q, k, v, p['q_norm'], p['k_norm'])
    nblk = t // MOBA_BLOCK
    km = kmean.reshape(nb, nblk, N_HEADS_B, HEAD_DIM).transpose(0, 2, 1, 3)
    km = jnp.pad(km, ((0, 0), (0, 0), (HEAD_DIM, 128 - HEAD_DIM - nblk), (0, 128 - HEAD_DIM)))
    out_b = _moba_attn(qh, kh, vh, km, bias_tiles)
    out_c = _gmlp(uv, p)
    x = _mix_ffn(x, out_a, out_b, out_c, mod, p, big['w_out'], big['w_up'], big['w_down'], layer,
                 per_row=False, head_major=True)
    return x, k_new, v, wkv_new, shift_new


def _sample_layer(x, mod, p, big, layer, bias_rows, cache_k, cache_v, page_table, state_wkv, shift0):
    n = x.shape[1]
    z, q, k, v, uv = _inproj(x, mod, p['norm_mix'], big['w_in'], layer, per_row=True)
    prep = _rwkv_prep(z, shift0.reshape(1, n, W_SHIFT), p, per_row=True)
    out_a, wkv_new = _wkv_step(state_wkv, layer, *[a.reshape(n, D_A) for a in prep], p['lnx_w'], p['lnx_b'])
    qn, kn = _qk_norm_rows(q.reshape(n, D_B), k.reshape(n, D_B), p['q_norm'], p['k_norm'])
    v = v.reshape(n, D_B)
    out_b = _moba_paged(qn, kn, v, cache_k, cache_v, page_table, layer, bias_rows)
    out_c, vg = _gmlp_first_pos(uv.reshape(n, 2 * D_C), p)
    x = _mix_ffn(x, out_a.reshape(1, n, D_A), out_b.reshape(1, n, D_B), out_c.reshape(1, n, D_C), mod, p,
                 big['w_out'], big['w_up'], big['w_down'], layer, per_row=True, head_major=False)
    return x, kn, v, wkv_new, z.reshape(n, W_SHIFT), vg


def kernel(x_prompt, x_sample, c_prompt, c_sample, cache_k, cache_v, page_table, state_wkv, state_shift, norm_mix, w_ada, b_ada, w_in, mu_shift, w0, w2, a0, a2, g2, k_k, k_a, r_k, lnx_w, lnx_b, q_norm, k_norm, rel_bias, out_norm_b, v_norm, w_s, b_s, out_norm_c, w_out, norm_ffn, w_up, w_down):
    layer_w = {
        'norm_mix': norm_mix, 'mu_shift': mu_shift, 'w0': w0, 'w2': w2, 'a0': a0, 'a2': a2, 'g2': g2,
        'k_k': k_k, 'k_a': k_a, 'r_k': r_k, 'lnx_w': lnx_w, 'lnx_b': lnx_b, 'q_norm': q_norm,
        'k_norm': k_norm, 'out_norm_b': out_norm_b, 'v_norm': v_norm, 'w_s': w_s, 'b_s': b_s,
        'out_norm_c': out_norm_c, 'norm_ffn': norm_ffn,
    }
    depth = w_in.shape[0]
    n_prompt, seq, d = x_prompt.shape
    n_dec = x_sample.shape[0]
    past_len = page_table.shape[1] * cache_k.shape[2]
    big = {'w_in': w_in.astype(BF16), 'w_out': w_out.astype(BF16),
           'w_up': w_up.astype(BF16), 'w_down': w_down.astype(BF16)}
    mod = _ada(jnp.concatenate([c_sample, c_prompt], axis=0), w_ada, b_ada)
    bias_tiles = _bias_tiles(rel_bias)
    bias_rows = _bias_rows(rel_bias, past_len)

    xp = x_prompt
    xs = x_sample.reshape(1, n_dec, d)
    outs = [[] for _ in range(9)]
    for l in range(depth):
        p = {name: arr[l] for name, arr in layer_w.items()}
        mod_s = mod[l, :n_dec].reshape(1, n_dec, 6 * d)
        mod_p = mod[l, n_dec:].reshape(n_prompt, 1, 6 * d)
        xp, k_p, v_p, wkv_p, sh_p = _prompt_layer(xp, mod_p, p, big, l, bias_tiles)
        xs, k_s, v_s, wkv_s, sh_s, vg_s = _sample_layer(
            xs, mod_s, p, big, l, bias_rows, cache_k, cache_v, page_table, state_wkv, state_shift[l])
        for lst, val in zip(outs, (k_p, v_p, k_s, v_s, wkv_p, wkv_s, sh_p, sh_s, vg_s)):
            lst.append(val)
    kp, vp, ks, vs, wp, ws, sp, ss, gs = (jnp.stack(o) for o in outs)
    heads = lambda a, rows: a.reshape(depth, rows, -1, N_HEADS_B, HEAD_DIM)
    return (xp, xs.reshape(n_dec, 1, d),
            heads(kp, n_prompt), heads(vp, n_prompt), heads(ks, n_dec), heads(vs, n_dec),
            wp, ws, sp, ss, gs.reshape(depth, n_dec, 1, D_C))
```

```python
import functools
import math

import numpy as np
import jax
import jax.numpy as jnp
from jax import lax
from jax.experimental import pallas as pl
from jax.experimental.pallas import tpu as pltpu

F32 = jnp.float32
BF16 = jnp.bfloat16

D_MODEL = 1024
HEAD_DIM = 64
N_HEADS_A = 6
N_HEADS_B = 6
N_GROUPS_C = 4
D_A = N_HEADS_A * HEAD_DIM
D_B = N_HEADS_B * HEAD_DIM
D_C = N_GROUPS_C * HEAD_DIM
LORA_W = 64
LORA_A = 64
LORA_G = 128
W_SHIFT = 3 * D_A + LORA_W + LORA_A + LORA_G
N_IN = W_SHIFT + 3 * D_B + 2 * D_C
IN_SEGS = (0, W_SHIFT, W_SHIFT + D_B, W_SHIFT + 2 * D_B, W_SHIFT + 3 * D_B, N_IN)
MOBA_BLOCK = 256
MOBA_TOPK = 3
CHUNK_C = 128
N_BUCKETS = 32
REL_MAX_DIST = 4096
D_FF = 4 * D_MODEL
EPS = 1e-6
LNX_EPS = 64e-5
WKV_CHUNK = 64
NEG_BIG = -1e30
V7X_VMEM_LIMIT = 56 * 1024 * 1024


def _bucket_saturation_tiles():
    max_exact = N_BUCKETS // 2
    n = np.arange(1, 2 * REL_MAX_DIST, dtype=np.float64)
    large = max_exact + (np.log(n / max_exact) / math.log(REL_MAX_DIST / max_exact)
                         * (N_BUCKETS - max_exact)).astype(np.int64)
    first_sat = int(n[np.argmax(np.minimum(large, N_BUCKETS - 1) == N_BUCKETS - 1)])
    return -(-(first_sat + MOBA_BLOCK) // MOBA_BLOCK) + 1


N_BIAS_TILES = _bucket_saturation_tiles() + 1

_NN = (((1,), (0,)), ((), ()))
_NT = (((1,), (1,)), ((), ()))
_TN = (((0,), (0,)), ((), ()))


def _mm(a, b, dims=_NN):
    return lax.dot_general(a, b, dims, preferred_element_type=F32)


def _split2(x):
    hi = x.astype(BF16)
    lo = (x - hi.astype(F32)).astype(BF16)
    return hi, lo


def _split3(x):
    h1 = x.astype(BF16)
    r1 = x - h1.astype(F32)
    h2 = r1.astype(BF16)
    h3 = (r1 - h2.astype(F32)).astype(BF16)
    return h1, h2, h3


def _dot1(a, b, dims=_NN):
    return _mm(a.astype(BF16), b.astype(BF16), dims)


def _dot3(a, b, dims=_NN):
    ah, al = _split2(a)
    bh, bl = _split2(b)
    return _mm(ah, bh, dims) + (_mm(ah, bl, dims) + _mm(al, bh, dims))


def _dotx(a, e, dims=_NN):
    ah, al = _split2(a)
    return _mm(ah, e, dims) + _mm(al, e, dims)


def _dotx3(a, e, dims=_NN):
    h1, h2, h3 = _split3(a)
    return _mm(h1, e, dims) + (_mm(h2, e, dims) + _mm(h3, e, dims))


def _params(sem):
    return pltpu.CompilerParams(dimension_semantics=sem, vmem_limit_bytes=V7X_VMEM_LIMIT)


def _sigmoid(x):
    return jax.nn.sigmoid(x)


def _head_ones(width):
    i = np.arange(width) // HEAD_DIM
    return jnp.asarray(i[:, None] == i[None, :], dtype=BF16)


def _ada_body(c_ref, w_ref, b_ref, o_ref):
    c = c_ref[...]
    o_ref[0] = _dot3(c * _sigmoid(c), w_ref[0]) + b_ref[0]


def _ada(c_all, w_ada, b_ada):
    depth, d, n = w_ada.shape
    m = c_all.shape[0]
    tn = 1536
    return pl.pallas_call(
        _ada_body,
        grid=(depth, n // tn),
        in_specs=[pl.BlockSpec((m, d), lambda l, j: (0, 0)),
                  pl.BlockSpec((1, d, tn), lambda l, j: (l, 0, j)),
                  pl.BlockSpec((1, 1, tn), lambda l, j: (l, 0, j))],
        out_specs=pl.BlockSpec((1, m, tn), lambda l, j: (l, 0, j)),
        out_shape=jax.ShapeDtypeStruct((depth, m, n), F32),
        compiler_params=_params(("parallel", "parallel")),
        name="ada_mod",
    )(c_all, w_ada, b_ada.reshape(depth, 1, n))


def _rel_bucket(dist):
    n = jnp.maximum(dist, 0)
    max_exact = N_BUCKETS // 2
    nf = jnp.maximum(n, 1).astype(F32)
    large = max_exact + (jnp.log(nf / max_exact) / math.log(REL_MAX_DIST / max_exact)
                         * (N_BUCKETS - max_exact)).astype(jnp.int32)
    large = jnp.minimum(large, N_BUCKETS - 1)
    return jnp.where(n < max_exact, n, large)


def _bias_tiles_body(rb_ref, o_ref):
    h = pl.program_id(0)
    d = pl.program_id(1)
    ti = lax.broadcasted_iota(jnp.int32, (MOBA_BLOCK, MOBA_BLOCK), 0)
    tj = lax.broadcasted_iota(jnp.int32, (MOBA_BLOCK, MOBA_BLOCK), 1)
    dist = d * MOBA_BLOCK + ti - tj
    bucket = _rel_bucket(dist)
    acc = jnp.zeros((MOBA_BLOCK, MOBA_BLOCK), F32)
    for b in range(N_BUCKETS):
        acc = jnp.where(bucket == b, rb_ref[h * N_BUCKETS + b], acc)
    o_ref[0, 0] = jnp.where(dist >= 0, acc, NEG_BIG)


def _bias_tiles(rel_bias):
    rb = rel_bias.T.reshape(-1)
    return pl.pallas_call(
        _bias_tiles_body,
        grid=(N_HEADS_B, N_BIAS_TILES),
        in_specs=[pl.BlockSpec(memory_space=pltpu.SMEM)],
        out_specs=pl.BlockSpec((1, 1, MOBA_BLOCK, MOBA_BLOCK), lambda h, d: (h, d, 0, 0)),
        out_shape=jax.ShapeDtypeStruct((N_HEADS_B, N_BIAS_TILES, MOBA_BLOCK, MOBA_BLOCK), F32),
        compiler_params=_params(("parallel", "parallel")),
        name="bias_tiles",
    )(rb)


def _bias_pages_body(past_len, rb_ref, o_ref):
    page_size = o_ref.shape[-1]
    pos = pl.program_id(0) * page_size + lax.broadcasted_iota(jnp.int32, o_ref.shape[1:], 2)
    bucket = _rel_bucket(past_len - pos)
    acc = jnp.zeros(o_ref.shape[1:], F32)
    for b in range(N_BUCKETS):
        acc = jnp.where(bucket == b, rb_ref[b], acc)
    o_ref[0] = acc


def _bias_pages(rel_bias, past_len, page_size):
    n_pages = past_len // page_size + 1
    rb = jnp.broadcast_to(rel_bias[:, :, None, None], (N_BUCKETS, N_HEADS_B, 1, page_size))
    return pl.pallas_call(
        functools.partial(_bias_pages_body, past_len),
        grid=(n_pages,),
        in_specs=[pl.BlockSpec((N_BUCKETS, N_HEADS_B, 1, page_size), lambda p: (0, 0, 0, 0))],
        out_specs=pl.BlockSpec((1, N_HEADS_B, 1, page_size), lambda p: (p, 0, 0, 0)),
        out_shape=jax.ShapeDtypeStruct((n_pages, N_HEADS_B, 1, page_size), F32),
        compiler_params=_params(("parallel",)),
        name="bias_pages",
    )(rb)


def _inproj_body(x_ref, mod_ref, gain_ref, w_ref, *out_refs):
    x = x_ref[0]
    d = x.shape[-1]
    y = x * lax.rsqrt(jnp.mean(x * x, axis=-1, keepdims=True) + EPS) * gain_ref[...]
    sh = mod_ref[0, :, 0:d]
    sc = mod_ref[0, :, d:2 * d]
    h = (y * (1.0 + sc) + sh).astype(BF16)
    for ref, a, b in zip(out_refs, IN_SEGS[:-1], IN_SEGS[1:]):
        ref[0] = _mm(h, w_ref[0, :, a:b])


def _inproj(x, mod, gain, w_in_bf, layer, per_row):
    nb, t, d = x.shape
    tm = min(t, 512)
    mod_rows = tm if per_row else 1
    widths = [b - a for a, b in zip(IN_SEGS[:-1], IN_SEGS[1:])]
    return pl.pallas_call(
        _inproj_body,
        grid=(nb, t // tm),
        in_specs=[pl.BlockSpec((1, tm, d), lambda b, i: (b, i, 0)),
                  pl.BlockSpec((1, mod_rows, 6 * d), lambda b, i: (b, i if per_row else 0, 0)),
                  pl.BlockSpec((1, d), lambda b, i: (0, 0)),
                  pl.BlockSpec((1, d, N_IN), lambda b, i: (layer, 0, 0))],
        out_specs=[pl.BlockSpec((1, tm, w), lambda b, i: (b, i, 0)) for w in widths],
        out_shape=[jax.ShapeDtypeStruct((nb, t, w), F32) for w in widths],
        compiler_params=_params(("parallel", "parallel")),
        name="in_proj",
    )(x, mod, gain.reshape(1, d), w_in_bf)


def _rwkv_prep_body(per_row, z_ref, prev_ref, mu_ref, w0_ref, a0_ref, kk_ref, ka_ref, rk_ref,
                    w2a_ref, g2_ref, e_ref, r_o, ld_o, k_o, v_o, kn_o, b_o, g_o, bon_o, *rest):
    z = z_ref[0]
    tm = z.shape[0]
    if per_row:
        zp = prev_ref[0]
    else:
        sh_o, carry = rest
        @pl.when(pl.program_id(1) == 0)
        def _():
            carry[...] = prev_ref[0]
        row = lax.broadcasted_iota(jnp.int32, z.shape, 0)
        zp = jnp.where(row == 0, carry[...], pltpu.roll(z, 1, axis=0))
        carry[...] = z[tm - 1:tm, :]
        sh_o[0] = z[tm - 1:tm, :]
    zs = z + mu_ref[...] * (zp - z)
    r = zs[:, 0:D_A]
    k = zs[:, D_A:2 * D_A]
    v = zs[:, 2 * D_A:3 * D_A]
    zwa = zs[:, 3 * D_A:3 * D_A + LORA_W + LORA_A]
    zg = zs[:, 3 * D_A + LORA_W + LORA_A:]
    lane = lax.broadcasted_iota(jnp.int32, zwa.shape, 1)
    lwa = _dot3(jnp.where(lane < LORA_W, jnp.tanh(zwa), zwa), w2a_ref[...])
    xw = -(w0_ref[...] + lwa[:, :D_A])
    softplus = jnp.maximum(xw, 0.0) + jnp.log1p(jnp.exp(-jnp.abs(xw)))
    w_log = -softplus - 0.5
    a = _sigmoid(a0_ref[...] + lwa[:, D_A:])
    g = _dot3(_sigmoid(zg), g2_ref[...])
    e = e_ref[...]
    kkr = k * kk_ref[...]
    kn = kkr / jnp.maximum(jnp.sqrt(_dotx(kkr * kkr, e)), 1e-12)
    k2 = k * (1.0 + (a - 1.0) * ka_ref[...])
    r_o[0] = r
    ld_o[0] = -jnp.exp(w_log)
    k_o[0] = k2
    v_o[0] = v
    kn_o[0] = kn
    b_o[0] = kn * a
    g_o[0] = g
    bon_o[0] = _dotx(r * k2 * rk_ref[...], e) * v


def _rwkv_prep(z, prev, p, per_row):
    nb, t, w = z.shape
    tm = min(t, 512)
    row = lambda x: x.reshape(1, -1)
    w2a = jnp.zeros((LORA_W + LORA_A, 2 * D_A), F32)
    w2a = w2a.at[:LORA_W, :D_A].set(p['w2']).at[LORA_W:, D_A:].set(p['a2'])
    const = lambda shape: pl.BlockSpec(shape, lambda b, i: (0,) * len(shape))
    tile = lambda width: pl.BlockSpec((1, tm, width), lambda b, i: (b, i, 0))
    prev_spec = tile(w) if per_row else pl.BlockSpec((1, 1, w), lambda b, i: (b, 0, 0))
    out_specs = [tile(D_A)] * 8
    out_shape = [jax.ShapeDtypeStruct((nb, t, D_A), F32)] * 8
    scratch = []
    if not per_row:
        out_specs = out_specs + [pl.BlockSpec((1, 1, w), lambda b, i: (b, 0, 0))]
        out_shape = out_shape + [jax.ShapeDtypeStruct((nb, 1, w), F32)]
        scratch = [pltpu.VMEM((1, w), F32)]
    return pl.pallas_call(
        functools.partial(_rwkv_prep_body, per_row),
        grid=(nb, t // tm),
        in_specs=[tile(w), prev_spec, const((1, w)), const((1, D_A)), const((1, D_A)), const((1, D_A)),
                  const((1, D_A)), const((1, D_A)), const((LORA_W + LORA_A, 2 * D_A)),
                  const((LORA_G, D_A)), const((D_A, D_A))],
        out_specs=out_specs,
        out_shape=out_shape,
        scratch_shapes=scratch,
        compiler_params=_params(("parallel", "arbitrary")),
        name="rwkv_prep",
    )(z, prev, row(p['mu_shift']), row(p['w0']), row(p['a0']), row(p['k_k']), row(p['k_a']),
      row(p['r_k']), w2a, p['g2'], _head_ones(D_A))


_BNN = (((2,), (1,)), ((0,), (0,)))
_BNT = (((2,), (2,)), ((0,), (0,)))
_BTN = (((1,), (1,)), ((0,), (0,)))


def _wkv_chunk(s0, kt, rt, kh, bh, kb, bb, vv, gam, tri_s, tri_i, eye):
    c = kt.shape[1]
    lhs = jnp.concatenate([kt, rt], axis=1)
    aa = _dot3(lhs, jnp.concatenate([kh, bh], axis=1), _BNT)
    a_kk = jnp.where(tri_s, aa[:, :c, :c], 0.0)
    a_kb = jnp.where(tri_s, aa[:, :c, c:], 0.0)
    a_rk = jnp.where(tri_i, aa[:, c:, :c], 0.0)
    a_rb = jnp.where(tri_i, aa[:, c:, c:], 0.0)
    x = -a_kb
    inv = eye + x
    span = 2
    while span < c:
        x = _dot3(x, x, _BNN)
        inv = inv + _dot3(inv, x, _BNN)
        span *= 2
    ks = _dot3(lhs, s0, _BNT)
    av = _dot3(jnp.concatenate([a_kk, a_rk], axis=1), vv, _BNN)
    u = _dot3(inv, ks[:, :c] + av[:, :c], _BNN)
    y = ks[:, c:] + av[:, c:] - _dot3(a_rb, u, _BNN)
    s1 = s0 * gam + _dot3(jnp.concatenate([vv, -u], axis=1), jnp.concatenate([kb, bb], axis=1), _BTN)
    return y, s1


def _wkv_scan_body(r_ref, ld_ref, k_ref, v_ref, kn_ref, b_ref, g_ref, bon_ref, lw_ref, lb_ref,
                   o_ref, s_out_ref, s_sc):
    nb, tb, _ = r_ref.shape
    c = WKV_CHUNK
    step = pl.program_id(0)

    @pl.when(step == 0)
    def _():
        s_sc[...] = jnp.zeros_like(s_sc)

    ri = lax.broadcasted_iota(jnp.int32, (c, c), 0)
    ci = lax.broadcasted_iota(jnp.int32, (c, c), 1)
    tri_s = ri > ci
    tri_i = ri >= ci
    cum = tri_i.astype(BF16)
    eye = (ri == ci).astype(F32)

    def heads(x):
        return jnp.stack([x[bi][:, h * HEAD_DIM:(h + 1) * HEAD_DIM]
                          for bi in range(nb) for h in range(N_HEADS_A)])

    def chunk(ic, carry):
        rows = pl.ds(pl.multiple_of(ic * c, c), c)
        ld = ld_ref[:, rows, :]
        gcum = []
        for bi in range(nb):
            l1, l2, l3 = _split3(ld[bi])
            gcum.append(_mm(cum, l1) + (_mm(cum, l2) + _mm(cum, l3)))
        gcum = jnp.stack(gcum)
        e_in = jnp.exp(gcum)
        e_ex = jnp.exp(gcum - ld)
        e_ng = jnp.exp(-gcum)
        rt = r_ref[:, rows, :] * e_in
        kt = kn_ref[:, rows, :] * e_ex
        kh = k_ref[:, rows, :] * e_ng
        bh = b_ref[:, rows, :] * e_ng
        gam = e_in[:, c - 1:c, :]
        y, s1 = _wkv_chunk(s_sc[...], heads(kt), heads(rt), heads(kh), heads(bh), heads(kh * gam),
                           heads(bh * gam), heads(v_ref[:, rows, :]), heads(gam), tri_s, tri_i, eye)
        s_sc[...] = s1
        mu = jnp.mean(y, axis=-1, keepdims=True)
        var = jnp.mean(jnp.square(y - mu), axis=-1, keepdims=True)
        yn = (y - mu) * lax.rsqrt(var + LNX_EPS)
        for bi in range(nb):
            for h in range(N_HEADS_A):
                sl = slice(h * HEAD_DIM, (h + 1) * HEAD_DIM)
                o_ref[bi, rows, sl] = ((yn[bi * N_HEADS_A + h] * lw_ref[:, sl] + lb_ref[:, sl]
                                        + bon_ref[bi, rows, sl]) * g_ref[bi, rows, sl])
        return carry

    lax.fori_loop(0, tb // c, chunk, 0)

    @pl.when(step == pl.num_programs(0) - 1)
    def _():
        s_out_ref[...] = s_sc[...]


def _wkv_scan(r, ld, k, v, kn, b, g, bon, lnx_w, lnx_b):
    nb, t, _ = r.shape
    tb = min(t, 256)
    tile = pl.BlockSpec((nb, tb, D_A), lambda i: (0, i, 0))
    vec = pl.BlockSpec((1, D_A), lambda i: (0, 0))
    st = (nb * N_HEADS_A, HEAD_DIM, HEAD_DIM)
    return pl.pallas_call(
        _wkv_scan_body,
        grid=(t // tb,),
        in_specs=[tile] * 8 + [vec, vec],
        out_specs=[tile, pl.BlockSpec(st, lambda i: (0, 0, 0))],
        out_shape=[jax.ShapeDtypeStruct((nb, t, D_A), F32), jax.ShapeDtypeStruct(st, F32)],
        scratch_shapes=[pltpu.VMEM(st, F32)],
        compiler_params=_params(("arbitrary",)),
        name="wkv_scan",
    )(r, ld, k, v, kn, b, g, bon, lnx_w.reshape(1, D_A), lnx_b.reshape(1, D_A))


def _wkv_step_body(s_ref, r_ref, ld_ref, k_ref, v_ref, kn_ref, b_ref, g_ref, bon_ref, lw_ref, lb_ref,
                   o_ref, s_out_ref):
    s0 = s_ref[0]
    ri = lax.broadcasted_iota(jnp.int32, (HEAD_DIM, HEAD_DIM), 0)
    ci = lax.broadcasted_iota(jnp.int32, (HEAD_DIM, HEAD_DIM), 1)
    eye = (ri == ci).astype(F32)
    u = jnp.sum(s0 * kn_ref[...], axis=-1, keepdims=True)
    v_col = jnp.sum(eye * v_ref[...], axis=-1, keepdims=True)
    s1 = s0 * jnp.exp(ld_ref[...]) - u * b_ref[...] + v_col * k_ref[...]
    s_out_ref[0] = s1
    y_col = jnp.sum(s1 * r_ref[...], axis=-1, keepdims=True)
    y = jnp.sum(eye * y_col, axis=-2, keepdims=True)
    mu = jnp.mean(y, axis=-1, keepdims=True)
    var = jnp.mean(jnp.square(y - mu), axis=-1, keepdims=True)
    yn = (y - mu) * lax.rsqrt(var + LNX_EPS)
    o_ref[...] = (yn * lw_ref[...] + lb_ref[...] + bon_ref[...]) * g_ref[...]


def _wkv_step(state_wkv, layer, r, ld, k, v, kn, b, g, bon, lnx_w, lnx_b):
    n = r.shape[0]
    tb = 8
    heads = lambda x: x.reshape(n, N_HEADS_A, 1, HEAD_DIM)
    row = pl.BlockSpec((tb, N_HEADS_A, 1, HEAD_DIM), lambda i: (i, 0, 0, 0))
    vec = pl.BlockSpec((1, N_HEADS_A, 1, HEAD_DIM), lambda i: (0, 0, 0, 0))
    st_in = pl.BlockSpec((1, tb, N_HEADS_A, HEAD_DIM, HEAD_DIM), lambda i: (layer, i, 0, 0, 0))
    st_out = pl.BlockSpec((1, tb, N_HEADS_A, HEAD_DIM, HEAD_DIM), lambda i: (0, i, 0, 0, 0))
    out, s1 = pl.pallas_call(
        _wkv_step_body,
        grid=(n // tb,),
        in_specs=[st_in] + [row] * 8 + [vec, vec],
        out_specs=[row, st_out],
        out_shape=[jax.ShapeDtypeStruct((n, N_HEADS_A, 1, HEAD_DIM), F32),
                   jax.ShapeDtypeStruct((1, n, N_HEADS_A, HEAD_DIM, HEAD_DIM), F32)],
        compiler_params=_params(("parallel",)),
        name="wkv_step",
    )(state_wkv, *[heads(x) for x in (r, ld, k, v, kn, b, g, bon)],
      lnx_w.reshape(1, N_HEADS_A, 1, HEAD_DIM), lnx_b.reshape(1, N_HEADS_A, 1, HEAD_DIM))
    return out.reshape(n, D_A), s1[0]


def _head_tile(x, h, lane):
    pair = x[:, (h // 2) * 128:(h // 2 + 1) * 128]
    if h % 2:
        pair = pltpu.roll(pair, HEAD_DIM, axis=1)
    return jnp.where(lane < HEAD_DIM, pair, 0.0)


def _qk_norm(x, gain, e):
    return x * lax.rsqrt(_dotx(x * x, e) * (1.0 / HEAD_DIM) + EPS) * gain


MOBA_MAX_LANE = 126
MOBA_DEN_LANE = HEAD_DIM
MOBA_MAX_BLOCKS = MOBA_MAX_LANE - HEAD_DIM
MOBA_GROUP = 4


def _moba_prep_body(q_ref, k_ref, v_ref, qg_ref, kg_ref, e_ref, qa_o, kh_o, vh_o, kn_o, km_sc):
    n = pl.program_id(1)

    @pl.when(n == 0)
    def _():
        km_sc[...] = jnp.zeros_like(km_sc)

    e = e_ref[...]
    qn = _qk_norm(q_ref[0], qg_ref[...], e)
    kn = _qk_norm(k_ref[0], kg_ref[...], e)
    v = v_ref[0]
    kn_o[0] = kn
    tq = qn.shape[0]
    lane = lax.broadcasted_iota(jnp.int32, (tq, 128), 1)
    lane_km = lax.broadcasted_iota(jnp.int32, (HEAD_DIM, 128), 1)
    blk = lane - HEAD_DIM
    valid = jnp.logical_and(blk >= 0, blk < n)
    no_pen = jnp.logical_or(blk < 0, lane >= MOBA_MAX_LANE)
    key_marks = jnp.where(jnp.logical_or(lane == HEAD_DIM + n, lane >= MOBA_MAX_LANE), 1.0, 0.0)
    val_marks = jnp.where(lane == MOBA_DEN_LANE, 1.0, 0.0)
    km = km_sc[...]
    for h in range(N_HEADS_B):
        qt = _head_tile(qn, h, lane)
        kmt = jnp.concatenate([jnp.zeros((HEAD_DIM, 128), F32), _head_tile(km, h, lane_km)], axis=0)
        gate = jnp.where(valid, _dot3(qt, kmt, _NT), -jnp.inf)
        keep = blk == n
        for _ in range(MOBA_TOPK):
            top = jnp.max(gate, axis=1, keepdims=True)
            first = jnp.min(jnp.where(gate == top, lane, 1 << 20), axis=1, keepdims=True)
            pick = lane == first
            keep = jnp.logical_or(keep, jnp.logical_and(pick, valid))
            gate = jnp.where(pick, -jnp.inf, gate)
        pen = jnp.where(jnp.logical_or(keep, no_pen), 0.0, NEG_BIG)
        qa_o[0, h] = jnp.where(lane < HEAD_DIM, qt * (HEAD_DIM ** -0.5), pen).astype(BF16)
        kh_o[0, h] = (_head_tile(kn, h, lane) + key_marks).astype(BF16)
        vh_o[0, h] = (_head_tile(v, h, lane) + val_marks).astype(BF16)
    km_sc[pl.ds(n, 1), :] = jnp.mean(kn, axis=0, keepdims=True)


def _moba_prep(q, k, v, q_gain, k_gain):
    nb, t, _ = q.shape
    nblk = t // MOBA_BLOCK
    assert nblk <= MOBA_MAX_BLOCKS and nblk % MOBA_GROUP == 0
    tile = pl.BlockSpec((1, MOBA_BLOCK, D_B), lambda b, i: (b, i, 0))
    vec = pl.BlockSpec((1, D_B), lambda b, i: (0, 0))
    hm = pl.BlockSpec((1, N_HEADS_B, MOBA_BLOCK, 128), lambda b, i: (b, 0, i, 0))
    return pl.pallas_call(
        _moba_prep_body,
        grid=(nb, nblk),
        in_specs=[tile, tile, tile, vec, vec, pl.BlockSpec((D_B, D_B), lambda b, i: (0, 0))],
        out_specs=[hm, hm, hm, tile],
        out_shape=[jax.ShapeDtypeStruct((nb, N_HEADS_B, t, 128), BF16)] * 3
                  + [jax.ShapeDtypeStruct((nb, t, D_B), F32)],
        scratch_shapes=[pltpu.VMEM((HEAD_DIM, D_B), F32)],
        compiler_params=_params(("parallel", "arbitrary")),
        name="moba_prep",
    )(q, k, v, jnp.tile(q_gain, N_HEADS_B).reshape(1, D_B), jnp.tile(k_gain, N_HEADS_B).reshape(1, D_B),
      _head_ones(D_B))


def _moba_attn_body(q_ref, k_ref, v_ref, tab_ref, o_ref, mx_sc, acc_sc):
    a = pl.program_id(2)
    qa = q_ref[0, 0]
    n_groups = (a + MOBA_GROUP) // MOBA_GROUP

    def key_rows(j):
        return pl.ds(pl.multiple_of(j * MOBA_BLOCK, MOBA_BLOCK), MOBA_BLOCK)

    def logits(q, j):
        return _mm(q, k_ref[0, 0, key_rows(j), :], _NT) + tab_ref[0, jnp.clip(a - j, 0, N_BIAS_TILES - 1)]

    mx_sc[...] = jnp.full_like(mx_sc, -jnp.inf)

    def pass1(g, carry):
        m = mx_sc[...]
        for i in range(MOBA_GROUP):
            m = jnp.maximum(m, logits(qa, g * MOBA_GROUP + i))
        mx_sc[...] = m
        return carry

    lax.fori_loop(0, n_groups, pass1, 0)
    m = jnp.max(mx_sc[...], axis=1, keepdims=True)
    m_hi = m.astype(BF16).astype(F32)
    lane = lax.broadcasted_iota(jnp.int32, qa.shape, 1)
    q2 = jnp.where(lane == MOBA_MAX_LANE, -m_hi, jnp.where(lane == MOBA_MAX_LANE + 1, m_hi - m, qa.astype(F32)))
    q2 = q2.astype(BF16)
    acc_sc[...] = jnp.zeros_like(acc_sc)

    def pass2(g, carry):
        acc = acc_sc[...]
        for i in range(MOBA_GROUP):
            j = g * MOBA_GROUP + i
            acc = acc + _mm(jnp.exp(logits(q2, j)).astype(BF16), v_ref[0, 0, key_rows(j), :])
        acc_sc[...] = acc
        return carry

    lax.fori_loop(0, n_groups, pass2, 0)
    acc = acc_sc[...]
    o_ref[0, 0] = acc / acc[:, MOBA_DEN_LANE:MOBA_DEN_LANE + 1]


def _moba_attn(qa, kh, vh, bias_tiles):
    nb, nh, t, _ = qa.shape
    nq = t // MOBA_BLOCK
    return pl.pallas_call(
        _moba_attn_body,
        grid=(nh, nb, nq),
        in_specs=[pl.BlockSpec((1, 1, MOBA_BLOCK, 128), lambda h, b, a: (b, h, a, 0)),
                  pl.BlockSpec((1, 1, t, 128), lambda h, b, a: (b, h, 0, 0)),
                  pl.BlockSpec((1, 1, t, 128), lambda h, b, a: (b, h, 0, 0)),
                  pl.BlockSpec((1, N_BIAS_TILES, MOBA_BLOCK, MOBA_BLOCK), lambda h, b, a: (h, 0, 0, 0))],
        out_specs=pl.BlockSpec((1, 1, MOBA_BLOCK, 128), lambda h, b, a: (b, h, a, 0)),
        out_shape=jax.ShapeDtypeStruct((nb, nh, t, 128), F32),
        scratch_shapes=[pltpu.VMEM((MOBA_BLOCK, MOBA_BLOCK), F32), pltpu.VMEM((MOBA_BLOCK, 128), F32)],
        compiler_params=_params(("parallel", "parallel", "arbitrary")),
        name="moba_attn",
    )(qa, kh, vh, bias_tiles)


def _qk_norm_rows_body(q_ref, k_ref, qg_ref, kg_ref, e_ref, qn_o, kn_o):
    e = e_ref[...]
    qn_o[...] = _qk_norm(q_ref[...], qg_ref[...], e)
    kn_o[...] = _qk_norm(k_ref[...], kg_ref[...], e)


def _qk_norm_rows(q, k, q_gain, k_gain):
    n = q.shape[0]
    return pl.pallas_call(
        _qk_norm_rows_body,
        out_shape=[jax.ShapeDtypeStruct((n, D_B), F32)] * 2,
        name="qk_norm_rows",
    )(q, k, jnp.tile(q_gain, N_HEADS_B).reshape(1, D_B), jnp.tile(k_gain, N_HEADS_B).reshape(1, D_B),
      _head_ones(D_B))


def _moba_paged_body(n_pages, pt_ref, qb_ref, knb_ref, vnb_ref, tab_ref, *rest):
    k_pages = rest[:n_pages]
    v_pages = rest[n_pages:2 * n_pages]
    o_ref = rest[2 * n_pages]
    page_size = qb_ref.shape[-1]
    per_blk = MOBA_BLOCK // page_size
    n_blk = n_pages // per_blk
    qb = qb_ref[0]
    scale = HEAD_DIM ** -0.5
    raw = [jnp.sum(k_pages[p][0, 0] * qb, axis=1, keepdims=True) for p in range(n_pages)]

    gates = []
    for n in range(n_blk):
        tot = raw[n * per_blk]
        for i in range(1, per_blk):
            tot = tot + raw[n * per_blk + i]
        gates.append(jnp.sum(tot, axis=2, keepdims=True) * (1.0 / MOBA_BLOCK))
    keep = [jnp.zeros(gates[0].shape, jnp.bool_)] * n_blk
    for _ in range(min(MOBA_TOPK, n_blk + 1)):
        top = functools.reduce(jnp.maximum, gates)
        found = jnp.zeros(top.shape, jnp.bool_)
        for n in range(n_blk):
            hit = jnp.logical_and(gates[n] == top, jnp.logical_not(found))
            keep[n] = jnp.logical_or(keep[n], hit)
            found = jnp.logical_or(found, hit)
            gates[n] = jnp.where(hit, -jnp.inf, gates[n])

    logits = [raw[p] * scale + tab_ref[p] + jnp.where(keep[p // per_blk], 0.0, NEG_BIG)
              for p in range(n_pages)]
    own = jnp.sum(knb_ref[0] * qb, axis=1, keepdims=True) * scale + tab_ref[n_pages]
    m = jnp.max(functools.reduce(jnp.maximum, logits + [own]), axis=2, keepdims=True)
    lane = lax.broadcasted_iota(jnp.int32, own.shape, 2)
    p_own = jnp.where(lane == 0, jnp.exp(own - m), 0.0)
    probs = [jnp.exp(s - m) for s in logits]
    inv_l = 1.0 / jnp.sum(functools.reduce(jnp.add, probs + [p_own]), axis=2, keepdims=True)
    acc = (p_own * inv_l) * vnb_ref[0]
    for p in range(n_pages):
        acc = acc + (probs[p] * inv_l) * v_pages[p][0, 0]
    h1, h2, h3 = _split3(acc.reshape(N_HEADS_B * HEAD_DIM, page_size))
    ones = jnp.ones((8, page_size), BF16)
    o_ref[0] = (_mm(ones, h1, _NT) + (_mm(ones, h2, _NT) + _mm(ones, h3, _NT)))[0:1]


def _moba_paged(qn, kn, v, cache_kt, cache_vt, page_table, layer, bias_pages):
    n, n_pages = page_table.shape
    page_size = cache_kt.shape[-1]
    tile = (N_HEADS_B, HEAD_DIM, page_size)
    lanes = lambda x: jnp.broadcast_to(x.reshape(n, N_HEADS_B, HEAD_DIM, 1), (n,) + tile)
    row = pl.BlockSpec((1,) + tile, lambda b, pt: (b, 0, 0, 0))
    page = lambda p: pl.BlockSpec((1, 1) + tile, lambda b, pt: (layer, pt[b, p], 0, 0, 0))
    out = pl.pallas_call(
        functools.partial(_moba_paged_body, n_pages),
        grid_spec=pltpu.PrefetchScalarGridSpec(
            num_scalar_prefetch=1,
            grid=(n,),
            in_specs=[row, row, row, pl.BlockSpec(bias_pages.shape, lambda b, pt: (0, 0, 0, 0))]
                     + [page(p) for p in range(n_pages)] * 2,
            out_specs=pl.BlockSpec((1, 1, D_B), lambda b, pt: (b, 0, 0))),
        out_shape=jax.ShapeDtypeStruct((n, 1, D_B), F32),
        compiler_params=_params(("arbitrary",)),
        name="moba_paged",
    )(page_table, lanes(qn), lanes(kn), lanes(v), bias_pages, *([cache_kt] * n_pages), *([cache_vt] * n_pages))
    return out.reshape(n, D_B)


def _gmlp_front(uv, vn_gain, e):
    ge = 0.5 * uv * (1.0 + lax.erf(uv * math.sqrt(0.5)))
    u = ge[:, :D_C]
    v = ge[:, D_C:]
    vg = v * lax.rsqrt(_dotx(v * v, e) * (1.0 / HEAD_DIM) + EPS) * vn_gain
    return u, vg


def _rms(x, gain):
    return x * lax.rsqrt(jnp.mean(x * x, axis=-1, keepdims=True) + EPS) * gain


def _gmlp_body(uv_ref, vn_ref, ws_ref, bs_ref, on_ref, e_ref, o_ref):
    u, vg = _gmlp_front(uv_ref[0], vn_ref[...], e_ref[...])
    tm = u.shape[0]
    ri = lax.broadcasted_iota(jnp.int32, (CHUNK_C, CHUNK_C), 0)
    ci = lax.broadcasted_iota(jnp.int32, (CHUNK_C, CHUNK_C), 1)
    group = lax.broadcasted_iota(jnp.int32, (CHUNK_C, D_C), 1) // HEAD_DIM
    ws = [jnp.where(ri >= ci, ws_ref[g], 0.0).astype(BF16) for g in range(N_GROUPS_C)]
    for c in range(tm // CHUNK_C):
        rows = slice(c * CHUNK_C, (c + 1) * CHUNK_C)
        vc = vg[rows].astype(BF16)
        mixed = bs_ref[...]
        for g in range(N_GROUPS_C):
            mixed = mixed + jnp.where(group == g, _mm(ws[g], vc), 0.0)
        o_ref[0, rows, :] = _rms(u[rows] * mixed, on_ref[...])


def _gmlp(uv, p):
    nb, t, _ = uv.shape
    tm = 512
    const = lambda shape: pl.BlockSpec(shape, lambda b, i: (0,) * len(shape))
    bs = jnp.repeat(p['b_s'].T, HEAD_DIM, axis=1)
    return pl.pallas_call(
        _gmlp_body,
        grid=(nb, t // tm),
        in_specs=[pl.BlockSpec((1, tm, 2 * D_C), lambda b, i: (b, i, 0)), const((1, D_C)),
                  const((N_GROUPS_C, CHUNK_C, CHUNK_C)), const((CHUNK_C, D_C)), const((1, D_C)),
                  const((D_C, D_C))],
        out_specs=pl.BlockSpec((1, tm, D_C), lambda b, i: (b, i, 0)),
        out_shape=jax.ShapeDtypeStruct((nb, t, D_C), F32),
        compiler_params=_params(("parallel", "parallel")),
        name="gmlp",
    )(uv, p['v_norm'].reshape(1, D_C), p['w_s'], bs, p['out_norm_c'].reshape(1, D_C), _head_ones(D_C))


def _gmlp_first_pos_body(uv_ref, vn_ref, w00_ref, b00_ref, on_ref, e_ref, o_ref, vg_ref):
    u, vg = _gmlp_front(uv_ref[...], vn_ref[...], e_ref[...])
    vg_ref[...] = vg
    o_ref[...] = _rms(u * (vg * w00_ref[...] + b00_ref[...]), on_ref[...])


def _gmlp_first_pos(uv, p):
    n = uv.shape[0]
    w00 = jnp.repeat(p['w_s'][:, 0, 0], HEAD_DIM).reshape(1, D_C)
    b00 = jnp.repeat(p['b_s'][:, 0], HEAD_DIM).reshape(1, D_C)
    return pl.pallas_call(
        _gmlp_first_pos_body,
        out_shape=[jax.ShapeDtypeStruct((n, D_C), F32)] * 2,
        name="gmlp_first_pos",
    )(uv, p['v_norm'].reshape(1, D_C), w00, b00, p['out_norm_c'].reshape(1, D_C), _head_ones(D_C))


def _mix_ffn_body(head_major, x_ref, oa_ref, ob_ref, oc_ref, mod_ref, nb_ref, nf_ref, wo_ref, wu_ref, wd_ref,
                  o_ref, x1_sc, h2_sc, acc_sc):
    j = pl.program_id(2)
    d = D_MODEL

    @pl.when(j == 0)
    def _():
        mix = _dot1(oa_ref[0], wo_ref[0, 0:D_A, :]) + _dot1(oc_ref[0], wo_ref[0, D_A + D_B:, :])
        if head_major:
            heads = [ob_ref[0, h][:, :HEAD_DIM] for h in range(N_HEADS_B)]
            ss = sum(jnp.sum(o * o, axis=-1, keepdims=True) for o in heads)
            inv = lax.rsqrt(ss * (1.0 / D_B) + EPS)
            for h, o in enumerate(heads):
                lo = D_A + h * HEAD_DIM
                mix = mix + _dot1(o * inv * nb_ref[:, h * HEAD_DIM:(h + 1) * HEAD_DIM],
                                  wo_ref[0, lo:lo + HEAD_DIM, :])
        else:
            mix = mix + _dot1(_rms(ob_ref[0], nb_ref[...]), wo_ref[0, D_A:D_A + D_B, :])
        x1 = x_ref[0] + mod_ref[0, :, 2 * d:3 * d] * mix
        x1_sc[...] = x1
        h2 = _rms(x1, nf_ref[...]) * (1.0 + mod_ref[0, :, 4 * d:5 * d]) + mod_ref[0, :, 3 * d:4 * d]
        h2_sc[...] = h2.astype(BF16)
        acc_sc[...] = jnp.zeros_like(acc_sc)

    up = _mm(h2_sc[...], wu_ref[0])
    acc_sc[...] += _mm(jnp.square(jnp.maximum(up, 0.0)).astype(BF16), wd_ref[0])

    @pl.when(j == pl.num_programs(2) - 1)
    def _():
        o_ref[0] = x1_sc[...] + mod_ref[0, :, 5 * d:6 * d] * acc_sc[...]


def _mix_ffn(x, oa, ob, oc, mod, p, w_out_bf, w_up_bf, w_down_bf, layer, per_row, head_major):
    nb, t, d = x.shape
    tm = min(t, 512)
    tf = 1024
    mod_rows = tm if per_row else 1
    tile = lambda width: pl.BlockSpec((1, tm, width), lambda b, i, j: (b, i, 0))
    const = lambda shape: pl.BlockSpec(shape, lambda b, i, j: (0,) * len(shape))
    if head_major:
        ob_spec = pl.BlockSpec((1, N_HEADS_B, tm, 128), lambda b, i, j: (b, 0, i, 0))
    else:
        ob_spec = tile(D_B)
    return pl.pallas_call(
        functools.partial(_mix_ffn_body, head_major),
        grid=(nb, t // tm, D_FF // tf),
        in_specs=[tile(d), tile(D_A), ob_spec, tile(D_C),
                  pl.BlockSpec((1, mod_rows, 6 * d), lambda b, i, j: (b, i if per_row else 0, 0)),
                  const((1, D_B)), const((1, d)),
                  pl.BlockSpec((1, d, d), lambda b, i, j: (layer, 0, 0)),
                  pl.BlockSpec((1, d, tf), lambda b, i, j: (layer, 0, j)),
                  pl.BlockSpec((1, tf, d), lambda b, i, j: (layer, j, 0))],
        out_specs=tile(d),
        out_shape=jax.ShapeDtypeStruct((nb, t, d), F32),
        scratch_shapes=[pltpu.VMEM((tm, d), F32), pltpu.VMEM((tm, d), BF16), pltpu.VMEM((tm, d), F32)],
        compiler_params=_params(("parallel", "parallel", "arbitrary")),
        name="mix_ffn",
    )(x, oa, ob, oc, mod, p['out_norm_b'].reshape(1, D_B), p['norm_ffn'].reshape(1, d),
      w_out_bf, w_up_bf, w_down_bf)


def _prompt_layer(x, mod, p, big, layer, bias_tiles):
    nb, t, _ = x.shape
    z, q, k, v, uv = _inproj(x, mod, p['norm_mix'], big['w_in'], layer, per_row=False)
    prep = _rwkv_prep(z, jnp.zeros((nb, 1, W_SHIFT), F32), p, per_row=False)
    shift_new = prep[8].reshape(nb, W_SHIFT)
    out_a, wkv_new = _wkv_scan(*prep[:8], p['lnx_w'], p['lnx_b'])
    wkv_new = wkv_new.reshape(nb, N_HEADS_A, HEAD_DIM, HEAD_DIM)
    qa, kh, vh, k_new = _moba_prep(q, k, v, p['q_norm'], p['k_norm'])
    out_b = _moba_attn(qa, kh, vh, bias_tiles)
    out_c = _gmlp(uv, p)
    x = _mix_ffn(x, out_a, out_b, out_c, mod, p, big['w_out'], big['w_up'], big['w_down'], layer,
                 per_row=False, head_major=True)
    return x, k_new, v, wkv_new, shift_new


def _sample_layer(x, mod, p, big, layer, bias_pages, cache_kt, cache_vt, page_table, state_wkv, shift0):
    n = x.shape[1]
    z, q, k, v, uv = _inproj(x, mod, p['norm_mix'], big['w_in'], layer, per_row=True)
    prep = _rwkv_prep(z, shift0.reshape(1, n, W_SHIFT), p, per_row=True)
    out_a, wkv_new = _wkv_step(state_wkv, layer, *[a.reshape(n, D_A) for a in prep], p['lnx_w'], p['lnx_b'])
    qn, kn = _qk_norm_rows(q.reshape(n, D_B), k.reshape(n, D_B), p['q_norm'], p['k_norm'])
    v = v.reshape(n, D_B)
    out_b = _moba_paged(qn, kn, v, cache_kt, cache_vt, page_table, layer, bias_pages)
    out_c, vg = _gmlp_first_pos(uv.reshape(n, 2 * D_C), p)
    x = _mix_ffn(x, out_a.reshape(1, n, D_A), out_b.reshape(1, n, D_B), out_c.reshape(1, n, D_C), mod, p,
                 big['w_out'], big['w_up'], big['w_down'], layer, per_row=True, head_major=False)
    return x, kn, v, wkv_new, z.reshape(n, W_SHIFT), vg


def kernel(x_prompt, x_sample, c_prompt, c_sample, cache_k, cache_v, page_table, state_wkv, state_shift, norm_mix, w_ada, b_ada, w_in, mu_shift, w0, w2, a0, a2, g2, k_k, k_a, r_k, lnx_w, lnx_b, q_norm, k_norm, rel_bias, out_norm_b, v_norm, w_s, b_s, out_norm_c, w_out, norm_ffn, w_up, w_down):
    layer_w = {
        'norm_mix': norm_mix, 'mu_shift': mu_shift, 'w0': w0, 'w2': w2, 'a0': a0, 'a2': a2, 'g2': g2,
        'k_k': k_k, 'k_a': k_a, 'r_k': r_k, 'lnx_w': lnx_w, 'lnx_b': lnx_b, 'q_norm': q_norm,
        'k_norm': k_norm, 'out_norm_b': out_norm_b, 'v_norm': v_norm, 'w_s': w_s, 'b_s': b_s,
        'out_norm_c': out_norm_c, 'norm_ffn': norm_ffn,
    }
    depth = w_in.shape[0]
    n_prompt, seq, d = x_prompt.shape
    n_dec = x_sample.shape[0]
    past_len = page_table.shape[1] * cache_k.shape[2]
    big = {'w_in': w_in.astype(BF16), 'w_out': w_out.astype(BF16),
           'w_up': w_up.astype(BF16), 'w_down': w_down.astype(BF16)}
    mod = _ada(jnp.concatenate([c_sample, c_prompt], axis=0), w_ada, b_ada)
    bias_tiles = _bias_tiles(rel_bias)
    bias_pages = _bias_pages(rel_bias, past_len, cache_k.shape[2])
    cache_kt = cache_k.transpose(0, 1, 3, 4, 2)
    cache_vt = cache_v.transpose(0, 1, 3, 4, 2)

    xp = x_prompt
    xs = x_sample.reshape(1, n_dec, d)
    outs = [[] for _ in range(9)]
    for l in range(depth):
        p = {name: arr[l] for name, arr in layer_w.items()}
        mod_s = mod[l, :n_dec].reshape(1, n_dec, 6 * d)
        mod_p = mod[l, n_dec:].reshape(n_prompt, 1, 6 * d)
        xp, k_p, v_p, wkv_p, sh_p = _prompt_layer(xp, mod_p, p, big, l, bias_tiles)
        xs, k_s, v_s, wkv_s, sh_s, vg_s = _sample_layer(
            xs, mod_s, p, big, l, bias_pages, cache_kt, cache_vt, page_table, state_wkv, state_shift[l])
        for lst, val in zip(outs, (k_p, v_p, k_s, v_s, wkv_p, wkv_s, sh_p, sh_s, vg_s)):
            lst.append(val)
    kp, vp, ks, vs, wp, ws, sp, ss, gs = (jnp.stack(o) for o in outs)
    heads = lambda a, rows: a.reshape(depth, rows, -1, N_HEADS_B, HEAD_DIM)
    return (xp, xs.reshape(n_dec, 1, d),
            heads(kp, n_prompt), heads(vp, n_prompt), heads(ks, n_dec), heads(vs, n_dec),
            wp, ws, sp, ss, gs.reshape(depth, n_dec, 1, D_C))
```

```python
import functools
import math

import numpy as np
import jax
import jax.numpy as jnp
from jax import lax
from jax.experimental import pallas as pl
from jax.experimental.pallas import tpu as pltpu

F32 = jnp.float32
BF16 = jnp.bfloat16

D_MODEL = 1024
HEAD_DIM = 64
N_HEADS_A = 6
N_HEADS_B = 6
N_GROUPS_C = 4
D_A = N_HEADS_A * HEAD_DIM
D_B = N_HEADS_B * HEAD_DIM
D_C = N_GROUPS_C * HEAD_DIM
LORA_W = 64
LORA_A = 64
LORA_G = 128
W_SHIFT = 3 * D_A + LORA_W + LORA_A + LORA_G
N_IN = W_SHIFT + 3 * D_B + 2 * D_C
IN_SEGS = (0, W_SHIFT, W_SHIFT + D_B, W_SHIFT + 2 * D_B, W_SHIFT + 3 * D_B, N_IN)
MOBA_BLOCK = 256
MOBA_TOPK = 3
CHUNK_C = 128
N_BUCKETS = 32
REL_MAX_DIST = 4096
D_FF = 4 * D_MODEL
EPS = 1e-6
LNX_EPS = 64e-5
WKV_CHUNK = 64
NEG_BIG = -1e30
V7X_VMEM_LIMIT = 56 * 1024 * 1024


def _bucket_saturation_tiles():
    max_exact = N_BUCKETS // 2
    n = np.arange(1, 2 * REL_MAX_DIST, dtype=np.float64)
    large = max_exact + (np.log(n / max_exact) / math.log(REL_MAX_DIST / max_exact)
                         * (N_BUCKETS - max_exact)).astype(np.int64)
    first_sat = int(n[np.argmax(np.minimum(large, N_BUCKETS - 1) == N_BUCKETS - 1)])
    return -(-(first_sat + MOBA_BLOCK) // MOBA_BLOCK) + 1


N_BIAS_TILES = _bucket_saturation_tiles() + 1

_NN = (((1,), (0,)), ((), ()))
_NT = (((1,), (1,)), ((), ()))
_TN = (((0,), (0,)), ((), ()))


def _mm(a, b, dims=_NN):
    return lax.dot_general(a, b, dims, preferred_element_type=F32)


def _split2(x):
    hi = x.astype(BF16)
    lo = (x - hi.astype(F32)).astype(BF16)
    return hi, lo


def _split3(x):
    h1 = x.astype(BF16)
    r1 = x - h1.astype(F32)
    h2 = r1.astype(BF16)
    h3 = (r1 - h2.astype(F32)).astype(BF16)
    return h1, h2, h3


def _dot1(a, b, dims=_NN):
    return _mm(a.astype(BF16), b.astype(BF16), dims)


def _dot3(a, b, dims=_NN):
    ah, al = _split2(a)
    bh, bl = _split2(b)
    return _mm(ah, bh, dims) + (_mm(ah, bl, dims) + _mm(al, bh, dims))


def _dotx(a, e, dims=_NN):
    ah, al = _split2(a)
    return _mm(ah, e, dims) + _mm(al, e, dims)


def _dotx3(a, e, dims=_NN):
    h1, h2, h3 = _split3(a)
    return _mm(h1, e, dims) + (_mm(h2, e, dims) + _mm(h3, e, dims))


def _params(sem):
    return pltpu.CompilerParams(dimension_semantics=sem, vmem_limit_bytes=V7X_VMEM_LIMIT)


def _sigmoid(x):
    return jax.nn.sigmoid(x)


def _head_ones(width):
    i = np.arange(width) // HEAD_DIM
    return jnp.asarray(i[:, None] == i[None, :], dtype=BF16)


def _ada_body(c_ref, w_ref, b_ref, o_ref):
    c = c_ref[...]
    o_ref[0] = _dot3(c * _sigmoid(c), w_ref[0]) + b_ref[0]


def _ada(c_all, w_ada, b_ada):
    depth, d, n = w_ada.shape
    m = c_all.shape[0]
    tn = 1536
    return pl.pallas_call(
        _ada_body,
        grid=(depth, n // tn),
        in_specs=[pl.BlockSpec((m, d), lambda l, j: (0, 0)),
                  pl.BlockSpec((1, d, tn), lambda l, j: (l, 0, j)),
                  pl.BlockSpec((1, 1, tn), lambda l, j: (l, 0, j))],
        out_specs=pl.BlockSpec((1, m, tn), lambda l, j: (l, 0, j)),
        out_shape=jax.ShapeDtypeStruct((depth, m, n), F32),
        compiler_params=_params(("parallel", "parallel")),
        name="ada_mod",
    )(c_all, w_ada, b_ada.reshape(depth, 1, n))


def _rel_bucket(dist):
    n = jnp.maximum(dist, 0)
    max_exact = N_BUCKETS // 2
    nf = jnp.maximum(n, 1).astype(F32)
    large = max_exact + (jnp.log(nf / max_exact) / math.log(REL_MAX_DIST / max_exact)
                         * (N_BUCKETS - max_exact)).astype(jnp.int32)
    large = jnp.minimum(large, N_BUCKETS - 1)
    return jnp.where(n < max_exact, n, large)


def _bias_tiles_body(rb_ref, o_ref):
    h = pl.program_id(0)
    d = pl.program_id(1)
    ti = lax.broadcasted_iota(jnp.int32, (MOBA_BLOCK, MOBA_BLOCK), 0)
    tj = lax.broadcasted_iota(jnp.int32, (MOBA_BLOCK, MOBA_BLOCK), 1)
    dist = d * MOBA_BLOCK + ti - tj
    bucket = _rel_bucket(dist)
    acc = jnp.zeros((MOBA_BLOCK, MOBA_BLOCK), F32)
    for b in range(N_BUCKETS):
        acc = jnp.where(bucket == b, rb_ref[h * N_BUCKETS + b], acc)
    o_ref[0, 0] = jnp.where(dist >= 0, acc, NEG_BIG)


def _bias_tiles(rel_bias):
    rb = rel_bias.T.reshape(-1)
    return pl.pallas_call(
        _bias_tiles_body,
        grid=(N_HEADS_B, N_BIAS_TILES),
        in_specs=[pl.BlockSpec(memory_space=pltpu.SMEM)],
        out_specs=pl.BlockSpec((1, 1, MOBA_BLOCK, MOBA_BLOCK), lambda h, d: (h, d, 0, 0)),
        out_shape=jax.ShapeDtypeStruct((N_HEADS_B, N_BIAS_TILES, MOBA_BLOCK, MOBA_BLOCK), F32),
        compiler_params=_params(("parallel", "parallel")),
        name="bias_tiles",
    )(rb)


def _bias_pages_body(past_len, rb_ref, o_ref):
    page_size = o_ref.shape[-1]
    pos = pl.program_id(0) * page_size + lax.broadcasted_iota(jnp.int32, o_ref.shape[1:], 2)
    bucket = _rel_bucket(past_len - pos)
    acc = jnp.zeros(o_ref.shape[1:], F32)
    for b in range(N_BUCKETS):
        acc = jnp.where(bucket == b, rb_ref[b], acc)
    o_ref[0] = acc


def _bias_pages(rel_bias, past_len, page_size):
    n_pages = past_len // page_size + 1
    rb = jnp.broadcast_to(rel_bias[:, :, None, None], (N_BUCKETS, N_HEADS_B, 1, page_size))
    return pl.pallas_call(
        functools.partial(_bias_pages_body, past_len),
        grid=(n_pages,),
        in_specs=[pl.BlockSpec((N_BUCKETS, N_HEADS_B, 1, page_size), lambda p: (0, 0, 0, 0))],
        out_specs=pl.BlockSpec((1, N_HEADS_B, 1, page_size), lambda p: (p, 0, 0, 0)),
        out_shape=jax.ShapeDtypeStruct((n_pages, N_HEADS_B, 1, page_size), F32),
        compiler_params=_params(("parallel",)),
        name="bias_pages",
    )(rb)


def _inproj_body(x_ref, mod_ref, gain_ref, w_ref, *out_refs):
    x = x_ref[0]
    d = x.shape[-1]
    y = x * lax.rsqrt(jnp.mean(x * x, axis=-1, keepdims=True) + EPS) * gain_ref[...]
    sh = mod_ref[0, :, 0:d]
    sc = mod_ref[0, :, d:2 * d]
    h = (y * (1.0 + sc) + sh).astype(BF16)
    for ref, a, b in zip(out_refs, IN_SEGS[:-1], IN_SEGS[1:]):
        ref[0] = _mm(h, w_ref[0, :, a:b])


def _inproj(x, mod, gain, w_in_bf, layer, per_row):
    nb, t, d = x.shape
    tm = min(t, 512)
    mod_rows = tm if per_row else 1
    widths = [b - a for a, b in zip(IN_SEGS[:-1], IN_SEGS[1:])]
    return pl.pallas_call(
        _inproj_body,
        grid=(nb, t // tm),
        in_specs=[pl.BlockSpec((1, tm, d), lambda b, i: (b, i, 0)),
                  pl.BlockSpec((1, mod_rows, 6 * d), lambda b, i: (b, i if per_row else 0, 0)),
                  pl.BlockSpec((1, d), lambda b, i: (0, 0)),
                  pl.BlockSpec((1, d, N_IN), lambda b, i: (layer, 0, 0))],
        out_specs=[pl.BlockSpec((1, tm, w), lambda b, i: (b, i, 0)) for w in widths],
        out_shape=[jax.ShapeDtypeStruct((nb, t, w), F32) for w in widths],
        compiler_params=_params(("parallel", "parallel")),
        name="in_proj",
    )(x, mod, gain.reshape(1, d), w_in_bf)


def _rwkv_prep_body(per_row, z_ref, prev_ref, mu_ref, w0_ref, a0_ref, kk_ref, ka_ref, rk_ref,
                    w2a_ref, g2_ref, e_ref, r_o, ld_o, k_o, v_o, kn_o, b_o, g_o, bon_o, *rest):
    z = z_ref[0]
    tm = z.shape[0]
    if per_row:
        zp = prev_ref[0]
    else:
        sh_o, carry = rest
        @pl.when(pl.program_id(1) == 0)
        def _():
            carry[...] = prev_ref[0]
        row = lax.broadcasted_iota(jnp.int32, z.shape, 0)
        zp = jnp.where(row == 0, carry[...], pltpu.roll(z, 1, axis=0))
        carry[...] = z[tm - 1:tm, :]
        sh_o[0] = z[tm - 1:tm, :]
    zs = z + mu_ref[...] * (zp - z)
    r = zs[:, 0:D_A]
    k = zs[:, D_A:2 * D_A]
    v = zs[:, 2 * D_A:3 * D_A]
    zwa = zs[:, 3 * D_A:3 * D_A + LORA_W + LORA_A]
    zg = zs[:, 3 * D_A + LORA_W + LORA_A:]
    lane = lax.broadcasted_iota(jnp.int32, zwa.shape, 1)
    lwa = _dot3(jnp.where(lane < LORA_W, jnp.tanh(zwa), zwa), w2a_ref[...])
    xw = -(w0_ref[...] + lwa[:, :D_A])
    softplus = jnp.maximum(xw, 0.0) + jnp.log1p(jnp.exp(-jnp.abs(xw)))
    w_log = -softplus - 0.5
    a = _sigmoid(a0_ref[...] + lwa[:, D_A:])
    g = _dot3(_sigmoid(zg), g2_ref[...])
    e = e_ref[...]
    kkr = k * kk_ref[...]
    kn = kkr / jnp.maximum(jnp.sqrt(_dotx(kkr * kkr, e)), 1e-12)
    k2 = k * (1.0 + (a - 1.0) * ka_ref[...])
    r_o[0] = r
    ld_o[0] = -jnp.exp(w_log)
    k_o[0] = k2
    v_o[0] = v
    kn_o[0] = kn
    b_o[0] = kn * a
    g_o[0] = g
    bon_o[0] = _dotx(r * k2 * rk_ref[...], e) * v


def _rwkv_prep(z, prev, p, per_row):
    nb, t, w = z.shape
    tm = min(t, 512)
    row = lambda x: x.reshape(1, -1)
    w2a = jnp.zeros((LORA_W + LORA_A, 2 * D_A), F32)
    w2a = w2a.at[:LORA_W, :D_A].set(p['w2']).at[LORA_W:, D_A:].set(p['a2'])
    const = lambda shape: pl.BlockSpec(shape, lambda b, i: (0,) * len(shape))
    tile = lambda width: pl.BlockSpec((1, tm, width), lambda b, i: (b, i, 0))
    prev_spec = tile(w) if per_row else pl.BlockSpec((1, 1, w), lambda b, i: (b, 0, 0))
    out_specs = [tile(D_A)] * 8
    out_shape = [jax.ShapeDtypeStruct((nb, t, D_A), F32)] * 8
    scratch = []
    if not per_row:
        out_specs = out_specs + [pl.BlockSpec((1, 1, w), lambda b, i: (b, 0, 0))]
        out_shape = out_shape + [jax.ShapeDtypeStruct((nb, 1, w), F32)]
        scratch = [pltpu.VMEM((1, w), F32)]
    return pl.pallas_call(
        functools.partial(_rwkv_prep_body, per_row),
        grid=(nb, t // tm),
        in_specs=[tile(w), prev_spec, const((1, w)), const((1, D_A)), const((1, D_A)), const((1, D_A)),
                  const((1, D_A)), const((1, D_A)), const((LORA_W + LORA_A, 2 * D_A)),
                  const((LORA_G, D_A)), const((D_A, D_A))],
        out_specs=out_specs,
        out_shape=out_shape,
        scratch_shapes=scratch,
        compiler_params=_params(("parallel", "arbitrary")),
        name="rwkv_prep",
    )(z, prev, row(p['mu_shift']), row(p['w0']), row(p['a0']), row(p['k_k']), row(p['k_a']),
      row(p['r_k']), w2a, p['g2'], _head_ones(D_A))


_BNN = (((2,), (1,)), ((0,), (0,)))
_BNT = (((2,), (2,)), ((0,), (0,)))
_BTN = (((1,), (1,)), ((0,), (0,)))


def _wkv_chunk(s0, kt, rt, kh, bh, kb, bb, vv, gam, tri_s, tri_i, eye):
    c = kt.shape[1]
    lhs = jnp.concatenate([kt, rt], axis=1)
    aa = _dot1(lhs, jnp.concatenate([kh, bh], axis=1), _BNT)
    a_kk = jnp.where(tri_s, aa[:, :c, :c], 0.0)
    a_kb = jnp.where(tri_s, aa[:, :c, c:], 0.0)
    a_rk = jnp.where(tri_i, aa[:, c:, :c], 0.0)
    a_rb = jnp.where(tri_i, aa[:, c:, c:], 0.0)
    x = -a_kb
    inv = eye + x
    span = 2
    while span < c:
        x = _dot1(x, x, _BNN)
        inv = inv + _dot1(inv, x, _BNN)
        span *= 2
    ks = _dot1(lhs, s0, _BNT)
    av = _dot1(jnp.concatenate([a_kk, a_rk], axis=1), vv, _BNN)
    u = _dot1(inv, ks[:, :c] + av[:, :c], _BNN)
    y = ks[:, c:] + av[:, c:] - _dot1(a_rb, u, _BNN)
    s1 = s0 * gam + _dot1(jnp.concatenate([vv, -u], axis=1), jnp.concatenate([kb, bb], axis=1), _BTN)
    return y, s1


def _wkv_scan_body(r_ref, ld_ref, k_ref, v_ref, kn_ref, b_ref, g_ref, bon_ref, lw_ref, lb_ref,
                   o_ref, s_out_ref, s_sc):
    nb, tb, _ = r_ref.shape
    c = WKV_CHUNK
    step = pl.program_id(0)

    @pl.when(step == 0)
    def _():
        s_sc[...] = jnp.zeros_like(s_sc)

    ri = lax.broadcasted_iota(jnp.int32, (c, c), 0)
    ci = lax.broadcasted_iota(jnp.int32, (c, c), 1)
    tri_s = ri > ci
    tri_i = ri >= ci
    cum = tri_i.astype(BF16)
    eye = (ri == ci).astype(F32)

    def heads(x):
        return jnp.stack([x[bi][:, h * HEAD_DIM:(h + 1) * HEAD_DIM]
                          for bi in range(nb) for h in range(N_HEADS_A)])

    def chunk(ic, carry):
        rows = pl.ds(pl.multiple_of(ic * c, c), c)
        ld = ld_ref[:, rows, :]
        gcum = []
        for bi in range(nb):
            l1, l2, l3 = _split3(ld[bi])
            gcum.append(_mm(cum, l1) + (_mm(cum, l2) + _mm(cum, l3)))
        gcum = jnp.stack(gcum)
        e_in = jnp.exp(gcum)
        e_ex = jnp.exp(gcum - ld)
        e_ng = jnp.exp(-gcum)
        rt = r_ref[:, rows, :] * e_in
        kt = kn_ref[:, rows, :] * e_ex
        kh = k_ref[:, rows, :] * e_ng
        bh = b_ref[:, rows, :] * e_ng
        gam = e_in[:, c - 1:c, :]
        y, s1 = _wkv_chunk(s_sc[...], heads(kt), heads(rt), heads(kh), heads(bh), heads(kh * gam),
                           heads(bh * gam), heads(v_ref[:, rows, :]), heads(gam), tri_s, tri_i, eye)
        s_sc[...] = s1
        mu = jnp.mean(y, axis=-1, keepdims=True)
        var = jnp.mean(jnp.square(y - mu), axis=-1, keepdims=True)
        yn = (y - mu) * lax.rsqrt(var + LNX_EPS)
        for bi in range(nb):
            for h in range(N_HEADS_A):
                sl = slice(h * HEAD_DIM, (h + 1) * HEAD_DIM)
                o_ref[bi, rows, sl] = ((yn[bi * N_HEADS_A + h] * lw_ref[:, sl] + lb_ref[:, sl]
                                        + bon_ref[bi, rows, sl]) * g_ref[bi, rows, sl])
        return carry

    lax.fori_loop(0, tb // c, chunk, 0, unroll=True)

    @pl.when(step == pl.num_programs(0) - 1)
    def _():
        s_out_ref[...] = s_sc[...]


def _wkv_scan(r, ld, k, v, kn, b, g, bon, lnx_w, lnx_b):
    nb, t, _ = r.shape
    tb = min(t, 256)
    tile = pl.BlockSpec((nb, tb, D_A), lambda i: (0, i, 0))
    vec = pl.BlockSpec((1, D_A), lambda i: (0, 0))
    st = (nb * N_HEADS_A, HEAD_DIM, HEAD_DIM)
    return pl.pallas_call(
        _wkv_scan_body,
        grid=(t // tb,),
        in_specs=[tile] * 8 + [vec, vec],
        out_specs=[tile, pl.BlockSpec(st, lambda i: (0, 0, 0))],
        out_shape=[jax.ShapeDtypeStruct((nb, t, D_A), F32), jax.ShapeDtypeStruct(st, F32)],
        scratch_shapes=[pltpu.VMEM(st, F32)],
        compiler_params=_params(("arbitrary",)),
        name="wkv_scan",
    )(r, ld, k, v, kn, b, g, bon, lnx_w.reshape(1, D_A), lnx_b.reshape(1, D_A))


def _wkv_step_body(s_ref, r_ref, ld_ref, k_ref, v_ref, kn_ref, b_ref, g_ref, bon_ref, lw_ref, lb_ref,
                   o_ref, s_out_ref):
    s0 = s_ref[0]
    ri = lax.broadcasted_iota(jnp.int32, (HEAD_DIM, HEAD_DIM), 0)
    ci = lax.broadcasted_iota(jnp.int32, (HEAD_DIM, HEAD_DIM), 1)
    eye = (ri == ci).astype(F32)
    u = jnp.sum(s0 * kn_ref[...], axis=-1, keepdims=True)
    v_col = jnp.sum(eye * v_ref[...], axis=-1, keepdims=True)
    s1 = s0 * jnp.exp(ld_ref[...]) - u * b_ref[...] + v_col * k_ref[...]
    s_out_ref[0] = s1
    y_col = jnp.sum(s1 * r_ref[...], axis=-1, keepdims=True)
    y = jnp.sum(eye * y_col, axis=-2, keepdims=True)
    mu = jnp.mean(y, axis=-1, keepdims=True)
    var = jnp.mean(jnp.square(y - mu), axis=-1, keepdims=True)
    yn = (y - mu) * lax.rsqrt(var + LNX_EPS)
    o_ref[...] = (yn * lw_ref[...] + lb_ref[...] + bon_ref[...]) * g_ref[...]


def _wkv_step(state_wkv, layer, r, ld, k, v, kn, b, g, bon, lnx_w, lnx_b):
    n = r.shape[0]
    tb = 8
    heads = lambda x: x.reshape(n, N_HEADS_A, 1, HEAD_DIM)
    row = pl.BlockSpec((tb, N_HEADS_A, 1, HEAD_DIM), lambda i: (i, 0, 0, 0))
    vec = pl.BlockSpec((1, N_HEADS_A, 1, HEAD_DIM), lambda i: (0, 0, 0, 0))
    st_in = pl.BlockSpec((1, tb, N_HEADS_A, HEAD_DIM, HEAD_DIM), lambda i: (layer, i, 0, 0, 0))
    st_out = pl.BlockSpec((1, tb, N_HEADS_A, HEAD_DIM, HEAD_DIM), lambda i: (0, i, 0, 0, 0))
    out, s1 = pl.pallas_call(
        _wkv_step_body,
        grid=(n // tb,),
        in_specs=[st_in] + [row] * 8 + [vec, vec],
        out_specs=[row, st_out],
        out_shape=[jax.ShapeDtypeStruct((n, N_HEADS_A, 1, HEAD_DIM), F32),
                   jax.ShapeDtypeStruct((1, n, N_HEADS_A, HEAD_DIM, HEAD_DIM), F32)],
        compiler_params=_params(("parallel",)),
        name="wkv_step",
    )(state_wkv, *[heads(x) for x in (r, ld, k, v, kn, b, g, bon)],
      lnx_w.reshape(1, N_HEADS_A, 1, HEAD_DIM), lnx_b.reshape(1, N_HEADS_A, 1, HEAD_DIM))
    return out.reshape(n, D_A), s1[0]


def _head_tile(x, h, lane):
    pair = x[:, (h // 2) * 128:(h // 2 + 1) * 128]
    if h % 2:
        pair = pltpu.roll(pair, HEAD_DIM, axis=1)
    return jnp.where(lane < HEAD_DIM, pair, 0.0)


def _qk_norm(x, gain, e):
    return x * lax.rsqrt(_dotx(x * x, e) * (1.0 / HEAD_DIM) + EPS) * gain


MOBA_DEN_LANE = HEAD_DIM
MOBA_MAX_BLOCKS = 128 - HEAD_DIM
MOBA_GROUP = 4


def _moba_prep_body(q_ref, k_ref, v_ref, qg_ref, kg_ref, e_ref, qa_o, kh_o, vh_o, kn_o, km_sc):
    n = pl.program_id(1)

    @pl.when(n == 0)
    def _():
        km_sc[...] = jnp.zeros_like(km_sc)

    e = e_ref[...]
    qn = _qk_norm(q_ref[0], qg_ref[...], e)
    kn = _qk_norm(k_ref[0], kg_ref[...], e)
    v = v_ref[0]
    kn_o[0] = kn
    tq = qn.shape[0]
    lane = lax.broadcasted_iota(jnp.int32, (tq, 128), 1)
    lane_km = lax.broadcasted_iota(jnp.int32, (MOBA_MAX_BLOCKS, 128), 1)
    blk = lax.broadcasted_iota(jnp.int32, (MOBA_MAX_BLOCKS, tq), 0)
    valid = blk < n
    key_marks = jnp.where(lane == HEAD_DIM + n, 1.0, 0.0)
    val_marks = jnp.where(lane == MOBA_DEN_LANE, 1.0, 0.0)
    km = km_sc[...]
    for h in range(N_HEADS_B):
        qt = _head_tile(qn, h, lane)
        gate = jnp.where(valid, _dot3(_head_tile(km, h, lane_km), qt, _NT), -jnp.inf)
        keep = blk == n
        for _ in range(MOBA_TOPK):
            top = jnp.max(gate, axis=0, keepdims=True)
            first = jnp.min(jnp.where(gate == top, blk, 1 << 20), axis=0, keepdims=True)
            pick = blk == first
            keep = jnp.logical_or(keep, jnp.logical_and(pick, valid))
            gate = jnp.where(pick, -jnp.inf, gate)
        pen = jnp.where(keep, 0.0, NEG_BIG)
        pen = jnp.concatenate([jnp.zeros((HEAD_DIM, tq), F32), pen], axis=0).T
        qa_o[0, h] = jnp.where(lane < HEAD_DIM, qt * (HEAD_DIM ** -0.5), pen).astype(BF16)
        kh_o[0, h] = (_head_tile(kn, h, lane) + key_marks).astype(BF16)
        vh_o[0, h] = (_head_tile(v, h, lane) + val_marks).astype(BF16)
    km_sc[pl.ds(n, 1), :] = jnp.mean(kn, axis=0, keepdims=True)


def _moba_prep(q, k, v, q_gain, k_gain):
    nb, t, _ = q.shape
    nblk = t // MOBA_BLOCK
    assert nblk <= MOBA_MAX_BLOCKS and nblk % MOBA_GROUP == 0
    tile = pl.BlockSpec((1, MOBA_BLOCK, D_B), lambda b, i: (b, i, 0))
    vec = pl.BlockSpec((1, D_B), lambda b, i: (0, 0))
    hm = pl.BlockSpec((1, N_HEADS_B, MOBA_BLOCK, 128), lambda b, i: (b, 0, i, 0))
    return pl.pallas_call(
        _moba_prep_body,
        grid=(nb, nblk),
        in_specs=[tile, tile, tile, vec, vec, pl.BlockSpec((D_B, D_B), lambda b, i: (0, 0))],
        out_specs=[hm, hm, hm, tile],
        out_shape=[jax.ShapeDtypeStruct((nb, N_HEADS_B, t, 128), BF16)] * 3
                  + [jax.ShapeDtypeStruct((nb, t, D_B), F32)],
        scratch_shapes=[pltpu.VMEM((MOBA_MAX_BLOCKS, D_B), F32)],
        compiler_params=_params(("parallel", "arbitrary")),
        name="moba_prep",
    )(q, k, v, jnp.tile(q_gain, N_HEADS_B).reshape(1, D_B), jnp.tile(k_gain, N_HEADS_B).reshape(1, D_B),
      _head_ones(D_B))


def _moba_attn_body(q_ref, k_ref, v_ref, tab_ref, o_ref, s_sc, mx_sc, acc_sc):
    a = pl.program_id(2)
    qa = q_ref[0, 0]
    n_groups = (a + MOBA_GROUP) // MOBA_GROUP
    span = MOBA_GROUP * MOBA_BLOCK

    def key_rows(g):
        return pl.ds(pl.multiple_of(g * span, span), span)

    mx_sc[...] = jnp.full_like(mx_sc, -jnp.inf)

    def pass1(g, carry):
        s = _mm(qa, k_ref[0, 0, key_rows(g), :], _NT)
        m = mx_sc[...]
        for i in range(MOBA_GROUP):
            cols = slice(i * MOBA_BLOCK, (i + 1) * MOBA_BLOCK)
            si = s[:, cols] + tab_ref[0, jnp.clip(a - (g * MOBA_GROUP + i), 0, N_BIAS_TILES - 1)]
            s_sc[g, :, cols] = si
            m = jnp.maximum(m, si)
        mx_sc[...] = m
        return carry

    lax.fori_loop(0, n_groups, pass1, 0)
    m = jnp.max(mx_sc[...], axis=1, keepdims=True)
    acc_sc[...] = jnp.zeros_like(acc_sc)

    def pass2(g, carry):
        p = jnp.exp(s_sc[g] - m).astype(BF16)
        acc_sc[...] += _mm(p, v_ref[0, 0, key_rows(g), :])
        return carry

    lax.fori_loop(0, n_groups, pass2, 0)
    acc = acc_sc[...]
    o_ref[0, 0] = acc / acc[:, MOBA_DEN_LANE:MOBA_DEN_LANE + 1]


def _moba_attn(qa, kh, vh, bias_tiles):
    nb, nh, t, _ = qa.shape
    nq = t // MOBA_BLOCK
    return pl.pallas_call(
        _moba_attn_body,
        grid=(nh, nb, nq),
        in_specs=[pl.BlockSpec((1, 1, MOBA_BLOCK, 128), lambda h, b, a: (b, h, a, 0)),
                  pl.BlockSpec((1, 1, t, 128), lambda h, b, a: (b, h, 0, 0)),
                  pl.BlockSpec((1, 1, t, 128), lambda h, b, a: (b, h, 0, 0)),
                  pl.BlockSpec((1, N_BIAS_TILES, MOBA_BLOCK, MOBA_BLOCK), lambda h, b, a: (h, 0, 0, 0))],
        out_specs=pl.BlockSpec((1, 1, MOBA_BLOCK, 128), lambda h, b, a: (b, h, a, 0)),
        out_shape=jax.ShapeDtypeStruct((nb, nh, t, 128), F32),
        scratch_shapes=[pltpu.VMEM((nq // MOBA_GROUP, MOBA_BLOCK, MOBA_GROUP * MOBA_BLOCK), F32),
                        pltpu.VMEM((MOBA_BLOCK, MOBA_BLOCK), F32), pltpu.VMEM((MOBA_BLOCK, 128), F32)],
        compiler_params=_params(("parallel", "parallel", "arbitrary")),
        name="moba_attn",
    )(qa, kh, vh, bias_tiles)


def _qk_norm_rows_body(q_ref, k_ref, qg_ref, kg_ref, e_ref, qn_o, kn_o):
    e = e_ref[...]
    qn_o[...] = _qk_norm(q_ref[...], qg_ref[...], e)
    kn_o[...] = _qk_norm(k_ref[...], kg_ref[...], e)


def _qk_norm_rows(q, k, q_gain, k_gain):
    n = q.shape[0]
    return pl.pallas_call(
        _qk_norm_rows_body,
        out_shape=[jax.ShapeDtypeStruct((n, D_B), F32)] * 2,
        name="qk_norm_rows",
    )(q, k, jnp.tile(q_gain, N_HEADS_B).reshape(1, D_B), jnp.tile(k_gain, N_HEADS_B).reshape(1, D_B),
      _head_ones(D_B))


def _moba_paged_body(n_pages, pt_ref, qb_ref, knb_ref, vnb_ref, tab_ref, *rest):
    k_pages = rest[:n_pages]
    v_pages = rest[n_pages:2 * n_pages]
    o_ref = rest[2 * n_pages]
    page_size = qb_ref.shape[-1]
    per_blk = MOBA_BLOCK // page_size
    n_blk = n_pages // per_blk
    qb = qb_ref[0]
    scale = HEAD_DIM ** -0.5
    raw = [jnp.sum(k_pages[p][0, 0] * qb, axis=1, keepdims=True) for p in range(n_pages)]

    gates = []
    for n in range(n_blk):
        tot = raw[n * per_blk]
        for i in range(1, per_blk):
            tot = tot + raw[n * per_blk + i]
        gates.append(jnp.sum(tot, axis=2, keepdims=True) * (1.0 / MOBA_BLOCK))
    keep = [jnp.zeros(gates[0].shape, jnp.bool_)] * n_blk
    for _ in range(min(MOBA_TOPK, n_blk + 1)):
        top = functools.reduce(jnp.maximum, gates)
        found = jnp.zeros(top.shape, jnp.bool_)
        for n in range(n_blk):
            hit = jnp.logical_and(gates[n] == top, jnp.logical_not(found))
            keep[n] = jnp.logical_or(keep[n], hit)
            found = jnp.logical_or(found, hit)
            gates[n] = jnp.where(hit, -jnp.inf, gates[n])

    logits = [raw[p] * scale + tab_ref[p] + jnp.where(keep[p // per_blk], 0.0, NEG_BIG)
              for p in range(n_pages)]
    own = jnp.sum(knb_ref[0] * qb, axis=1, keepdims=True) * scale + tab_ref[n_pages]
    m = jnp.max(functools.reduce(jnp.maximum, logits + [own]), axis=2, keepdims=True)
    lane = lax.broadcasted_iota(jnp.int32, own.shape, 2)
    p_own = jnp.where(lane == 0, jnp.exp(own - m), 0.0)
    probs = [jnp.exp(s - m) for s in logits]
    inv_l = 1.0 / jnp.sum(functools.reduce(jnp.add, probs + [p_own]), axis=2, keepdims=True)
    acc = (p_own * inv_l) * vnb_ref[0]
    for p in range(n_pages):
        acc = acc + (probs[p] * inv_l) * v_pages[p][0, 0]
    h1, h2, h3 = _split3(acc.reshape(N_HEADS_B * HEAD_DIM, page_size))
    ones = jnp.ones((8, page_size), BF16)
    o_ref[0] = (_mm(ones, h1, _NT) + (_mm(ones, h2, _NT) + _mm(ones, h3, _NT)))[0:1]


def _moba_paged(qn, kn, v, cache_kt, cache_vt, page_table, layer, bias_pages):
    n, n_pages = page_table.shape
    page_size = cache_kt.shape[-1]
    tile = (N_HEADS_B, HEAD_DIM, page_size)
    lanes = lambda x: jnp.broadcast_to(x.reshape(n, N_HEADS_B, HEAD_DIM, 1), (n,) + tile)
    row = pl.BlockSpec((1,) + tile, lambda b, pt: (b, 0, 0, 0))
    page = lambda p: pl.BlockSpec((1, 1) + tile, lambda b, pt: (layer, pt[b, p], 0, 0, 0))
    out = pl.pallas_call(
        functools.partial(_moba_paged_body, n_pages),
        grid_spec=pltpu.PrefetchScalarGridSpec(
            num_scalar_prefetch=1,
            grid=(n,),
            in_specs=[row, row, row, pl.BlockSpec(bias_pages.shape, lambda b, pt: (0, 0, 0, 0))]
                     + [page(p) for p in range(n_pages)] * 2,
            out_specs=pl.BlockSpec((1, 1, D_B), lambda b, pt: (b, 0, 0))),
        out_shape=jax.ShapeDtypeStruct((n, 1, D_B), F32),
        compiler_params=_params(("arbitrary",)),
        name="moba_paged",
    )(page_table, lanes(qn), lanes(kn), lanes(v), bias_pages, *([cache_kt] * n_pages), *([cache_vt] * n_pages))
    return out.reshape(n, D_B)


def _gmlp_front(uv, vn_gain, e):
    ge = 0.5 * uv * (1.0 + lax.erf(uv * math.sqrt(0.5)))
    u = ge[:, :D_C]
    v = ge[:, D_C:]
    vg = v * lax.rsqrt(_dotx(v * v, e) * (1.0 / HEAD_DIM) + EPS) * vn_gain
    return u, vg


def _rms(x, gain):
    return x * lax.rsqrt(jnp.mean(x * x, axis=-1, keepdims=True) + EPS) * gain


def _gmlp_body(uv_ref, vn_ref, ws_ref, bs_ref, on_ref, e_ref, o_ref):
    u, vg = _gmlp_front(uv_ref[0], vn_ref[...], e_ref[...])
    tm = u.shape[0]
    ri = lax.broadcasted_iota(jnp.int32, (CHUNK_C, CHUNK_C), 0)
    ci = lax.broadcasted_iota(jnp.int32, (CHUNK_C, CHUNK_C), 1)
    group = lax.broadcasted_iota(jnp.int32, (CHUNK_C, D_C), 1) // HEAD_DIM
    ws = [jnp.where(ri >= ci, ws_ref[g], 0.0).astype(BF16) for g in range(N_GROUPS_C)]
    for c in range(tm // CHUNK_C):
        rows = slice(c * CHUNK_C, (c + 1) * CHUNK_C)
        vc = vg[rows].astype(BF16)
        mixed = bs_ref[...]
        for g in range(N_GROUPS_C):
            mixed = mixed + jnp.where(group == g, _mm(ws[g], vc), 0.0)
        o_ref[0, rows, :] = _rms(u[rows] * mixed, on_ref[...])


def _gmlp(uv, p):
    nb, t, _ = uv.shape
    tm = 512
    const = lambda shape: pl.BlockSpec(shape, lambda b, i: (0,) * len(shape))
    bs = jnp.repeat(p['b_s'].T, HEAD_DIM, axis=1)
    return pl.pallas_call(
        _gmlp_body,
        grid=(nb, t // tm),
        in_specs=[pl.BlockSpec((1, tm, 2 * D_C), lambda b, i: (b, i, 0)), const((1, D_C)),
                  const((N_GROUPS_C, CHUNK_C, CHUNK_C)), const((CHUNK_C, D_C)), const((1, D_C)),
                  const((D_C, D_C))],
        out_specs=pl.BlockSpec((1, tm, D_C), lambda b, i: (b, i, 0)),
        out_shape=jax.ShapeDtypeStruct((nb, t, D_C), F32),
        compiler_params=_params(("parallel", "parallel")),
        name="gmlp",
    )(uv, p['v_norm'].reshape(1, D_C), p['w_s'], bs, p['out_norm_c'].reshape(1, D_C), _head_ones(D_C))


def _gmlp_first_pos_body(uv_ref, vn_ref, w00_ref, b00_ref, on_ref, e_ref, o_ref, vg_ref):
    u, vg = _gmlp_front(uv_ref[...], vn_ref[...], e_ref[...])
    vg_ref[...] = vg
    o_ref[...] = _rms(u * (vg * w00_ref[...] + b00_ref[...]), on_ref[...])


def _gmlp_first_pos(uv, p):
    n = uv.shape[0]
    w00 = jnp.repeat(p['w_s'][:, 0, 0], HEAD_DIM).reshape(1, D_C)
    b00 = jnp.repeat(p['b_s'][:, 0], HEAD_DIM).reshape(1, D_C)
    return pl.pallas_call(
        _gmlp_first_pos_body,
        out_shape=[jax.ShapeDtypeStruct((n, D_C), F32)] * 2,
        name="gmlp_first_pos",
    )(uv, p['v_norm'].reshape(1, D_C), w00, b00, p['out_norm_c'].reshape(1, D_C), _head_ones(D_C))


def _mix_ffn_body(head_major, x_ref, oa_ref, ob_ref, oc_ref, mod_ref, nb_ref, nf_ref, wo_ref, wu_ref, wd_ref,
                  o_ref, x1_sc, h2_sc, acc_sc):
    j = pl.program_id(2)
    d = D_MODEL

    @pl.when(j == 0)
    def _():
        mix = _dot1(oa_ref[0], wo_ref[0, 0:D_A, :]) + _dot1(oc_ref[0], wo_ref[0, D_A + D_B:, :])
        if head_major:
            heads = [ob_ref[0, h][:, :HEAD_DIM] for h in range(N_HEADS_B)]
            ss = sum(jnp.sum(o * o, axis=-1, keepdims=True) for o in heads)
            inv = lax.rsqrt(ss * (1.0 / D_B) + EPS)
            for h, o in enumerate(heads):
                lo = D_A + h * HEAD_DIM
                mix = mix + _dot1(o * inv * nb_ref[:, h * HEAD_DIM:(h + 1) * HEAD_DIM],
                                  wo_ref[0, lo:lo + HEAD_DIM, :])
        else:
            mix = mix + _dot1(_rms(ob_ref[0], nb_ref[...]), wo_ref[0, D_A:D_A + D_B, :])
        x1 = x_ref[0] + mod_ref[0, :, 2 * d:3 * d] * mix
        x1_sc[...] = x1
        h2 = _rms(x1, nf_ref[...]) * (1.0 + mod_ref[0, :, 4 * d:5 * d]) + mod_ref[0, :, 3 * d:4 * d]
        h2_sc[...] = h2.astype(BF16)
        acc_sc[...] = jnp.zeros_like(acc_sc)

    up = _mm(h2_sc[...], wu_ref[0])
    acc_sc[...] += _mm(jnp.square(jnp.maximum(up, 0.0)).astype(BF16), wd_ref[0])

    @pl.when(j == pl.num_programs(2) - 1)
    def _():
        o_ref[0] = x1_sc[...] + mod_ref[0, :, 5 * d:6 * d] * acc_sc[...]


def _mix_ffn(x, oa, ob, oc, mod, p, w_out_bf, w_up_bf, w_down_bf, layer, per_row, head_major):
    nb, t, d = x.shape
    tm = min(t, 512)
    tf = 1024
    mod_rows = tm if per_row else 1
    tile = lambda width: pl.BlockSpec((1, tm, width), lambda b, i, j: (b, i, 0))
    const = lambda shape: pl.BlockSpec(shape, lambda b, i, j: (0,) * len(shape))
    if head_major:
        ob_spec = pl.BlockSpec((1, N_HEADS_B, tm, 128), lambda b, i, j: (b, 0, i, 0))
    else:
        ob_spec = tile(D_B)
    return pl.pallas_call(
        functools.partial(_mix_ffn_body, head_major),
        grid=(nb, t // tm, D_FF // tf),
        in_specs=[tile(d), tile(D_A), ob_spec, tile(D_C),
                  pl.BlockSpec((1, mod_rows, 6 * d), lambda b, i, j: (b, i if per_row else 0, 0)),
                  const((1, D_B)), const((1, d)),
                  pl.BlockSpec((1, d, d), lambda b, i, j: (layer, 0, 0)),
                  pl.BlockSpec((1, d, tf), lambda b, i, j: (layer, 0, j)),
                  pl.BlockSpec((1, tf, d), lambda b, i, j: (layer, j, 0))],
        out_specs=tile(d),
        out_shape=jax.ShapeDtypeStruct((nb, t, d), F32),
        scratch_shapes=[pltpu.VMEM((tm, d), F32), pltpu.VMEM((tm, d), BF16), pltpu.VMEM((tm, d), F32)],
        compiler_params=_params(("parallel", "parallel", "arbitrary")),
        name="mix_ffn",
    )(x, oa, ob, oc, mod, p['out_norm_b'].reshape(1, D_B), p['norm_ffn'].reshape(1, d),
      w_out_bf, w_up_bf, w_down_bf)


def _prompt_layer(x, mod, p, big, layer, bias_tiles):
    nb, t, _ = x.shape
    z, q, k, v, uv = _inproj(x, mod, p['norm_mix'], big['w_in'], layer, per_row=False)
    prep = _rwkv_prep(z, jnp.zeros((nb, 1, W_SHIFT), F32), p, per_row=False)
    shift_new = prep[8].reshape(nb, W_SHIFT)
    out_a, wkv_new = _wkv_scan(*prep[:8], p['lnx_w'], p['lnx_b'])
    wkv_new = wkv_new.reshape(nb, N_HEADS_A, HEAD_DIM, HEAD_DIM)
    qa, kh, vh, k_new = _moba_prep(q, k, v, p['q_norm'], p['k_norm'])
    out_b = _moba_attn(qa, kh, vh, bias_tiles)
    out_c = _gmlp(uv, p)
    x = _mix_ffn(x, out_a, out_b, out_c, mod, p, big['w_out'], big['w_up'], big['w_down'], layer,
                 per_row=False, head_major=True)
    return x, k_new, v, wkv_new, shift_new


def _sample_layer(x, mod, p, big, layer, bias_pages, cache_kt, cache_vt, page_table, state_wkv, shift0):
    n = x.shape[1]
    z, q, k, v, uv = _inproj(x, mod, p['norm_mix'], big['w_in'], layer, per_row=True)
    prep = _rwkv_prep(z, shift0.reshape(1, n, W_SHIFT), p, per_row=True)
    out_a, wkv_new = _wkv_step(state_wkv, layer, *[a.reshape(n, D_A) for a in prep], p['lnx_w'], p['lnx_b'])
    qn, kn = _qk_norm_rows(q.reshape(n, D_B), k.reshape(n, D_B), p['q_norm'], p['k_norm'])
    v = v.reshape(n, D_B)
    out_b = _moba_paged(qn, kn, v, cache_kt, cache_vt, page_table, layer, bias_pages)
    out_c, vg = _gmlp_first_pos(uv.reshape(n, 2 * D_C), p)
    x = _mix_ffn(x, out_a.reshape(1, n, D_A), out_b.reshape(1, n, D_B), out_c.reshape(1, n, D_C), mod, p,
                 big['w_out'], big['w_up'], big['w_down'], layer, per_row=True, head_major=False)
    return x, kn, v, wkv_new, z.reshape(n, W_SHIFT), vg


def kernel(x_prompt, x_sample, c_prompt, c_sample, cache_k, cache_v, page_table, state_wkv, state_shift, norm_mix, w_ada, b_ada, w_in, mu_shift, w0, w2, a0, a2, g2, k_k, k_a, r_k, lnx_w, lnx_b, q_norm, k_norm, rel_bias, out_norm_b, v_norm, w_s, b_s, out_norm_c, w_out, norm_ffn, w_up, w_down):
    layer_w = {
        'norm_mix': norm_mix, 'mu_shift': mu_shift, 'w0': w0, 'w2': w2, 'a0': a0, 'a2': a2, 'g2': g2,
        'k_k': k_k, 'k_a': k_a, 'r_k': r_k, 'lnx_w': lnx_w, 'lnx_b': lnx_b, 'q_norm': q_norm,
        'k_norm': k_norm, 'out_norm_b': out_norm_b, 'v_norm': v_norm, 'w_s': w_s, 'b_s': b_s,
        'out_norm_c': out_norm_c, 'norm_ffn': norm_ffn,
    }
    depth = w_in.shape[0]
    n_prompt, seq, d = x_prompt.shape
    n_dec = x_sample.shape[0]
    past_len = page_table.shape[1] * cache_k.shape[2]
    big = {'w_in': w_in.astype(BF16), 'w_out': w_out.astype(BF16),
           'w_up': w_up.astype(BF16), 'w_down': w_down.astype(BF16)}
    mod = _ada(jnp.concatenate([c_sample, c_prompt], axis=0), w_ada, b_ada)
    bias_tiles = _bias_tiles(rel_bias)
    bias_pages = _bias_pages(rel_bias, past_len, cache_k.shape[2])
    cache_kt = cache_k.transpose(0, 1, 3, 4, 2)
    cache_vt = cache_v.transpose(0, 1, 3, 4, 2)

    xp = x_prompt
    xs = x_sample.reshape(1, n_dec, d)
    outs = [[] for _ in range(9)]
    for l in range(depth):
        p = {name: arr[l] for name, arr in layer_w.items()}
        mod_s = mod[l, :n_dec].reshape(1, n_dec, 6 * d)
        mod_p = mod[l, n_dec:].reshape(n_prompt, 1, 6 * d)
        xp, k_p, v_p, wkv_p, sh_p = _prompt_layer(xp, mod_p, p, big, l, bias_tiles)
        xs, k_s, v_s, wkv_s, sh_s, vg_s = _sample_layer(
            xs, mod_s, p, big, l, bias_pages, cache_kt, cache_vt, page_table, state_wkv, state_shift[l])
        for lst, val in zip(outs, (k_p, v_p, k_s, v_s, wkv_p, wkv_s, sh_p, sh_s, vg_s)):
            lst.append(val)
    kp, vp, ks, vs, wp, ws, sp, ss, gs = (jnp.stack(o) for o in outs)
    heads = lambda a, rows: a.reshape(depth, rows, -1, N_HEADS_B, HEAD_DIM)
    return (xp, xs.reshape(n_dec, 1, d),
            heads(kp, n_prompt), heads(vp, n_prompt), heads(ks, n_dec), heads(vs, n_dec),
            wp, ws, sp, ss, gs.reshape(depth, n_dec, 1, D_C))
```

```python
import functools
import math

import numpy as np
import jax
import jax.numpy as jnp
from jax import lax
from jax.experimental import pallas as pl
from jax.experimental.pallas import tpu as pltpu

F32 = jnp.float32
BF16 = jnp.bfloat16

D_MODEL = 1024
HEAD_DIM = 64
N_HEADS_A = 6
N_HEADS_B = 6
N_GROUPS_C = 4
D_A = N_HEADS_A * HEAD_DIM
D_B = N_HEADS_B * HEAD_DIM
D_C = N_GROUPS_C * HEAD_DIM
LORA_W = 64
LORA_A = 64
LORA_G = 128
W_SHIFT = 3 * D_A + LORA_W + LORA_A + LORA_G
N_IN = W_SHIFT + 3 * D_B + 2 * D_C
IN_SEGS = (0, W_SHIFT, W_SHIFT + D_B, W_SHIFT + 2 * D_B, W_SHIFT + 3 * D_B, N_IN)
MOBA_BLOCK = 256
MOBA_TOPK = 3
CHUNK_C = 128
N_BUCKETS = 32
REL_MAX_DIST = 4096
D_FF = 4 * D_MODEL
EPS = 1e-6
LNX_EPS = 64e-5
WKV_CHUNK = 64
NEG_BIG = -1e30
LOG2_E = math.log2(math.e)
V7X_VMEM_LIMIT = 56 * 1024 * 1024


def _bucket_saturation_tiles():
    max_exact = N_BUCKETS // 2
    n = np.arange(1, 2 * REL_MAX_DIST, dtype=np.float64)
    large = max_exact + (np.log(n / max_exact) / math.log(REL_MAX_DIST / max_exact)
                         * (N_BUCKETS - max_exact)).astype(np.int64)
    first_sat = int(n[np.argmax(np.minimum(large, N_BUCKETS - 1) == N_BUCKETS - 1)])
    return -(-(first_sat + MOBA_BLOCK) // MOBA_BLOCK) + 1


N_BIAS_TILES = _bucket_saturation_tiles() + 1

_NN = (((1,), (0,)), ((), ()))
_NT = (((1,), (1,)), ((), ()))
_TN = (((0,), (0,)), ((), ()))


def _mm(a, b, dims=_NN):
    return lax.dot_general(a, b, dims, preferred_element_type=F32)


def _split2(x):
    hi = x.astype(BF16)
    lo = (x - hi.astype(F32)).astype(BF16)
    return hi, lo


def _split3(x):
    h1 = x.astype(BF16)
    r1 = x - h1.astype(F32)
    h2 = r1.astype(BF16)
    h3 = (r1 - h2.astype(F32)).astype(BF16)
    return h1, h2, h3


def _dot1(a, b, dims=_NN):
    return _mm(a.astype(BF16), b.astype(BF16), dims)


def _dot3(a, b, dims=_NN):
    ah, al = _split2(a)
    bh, bl = _split2(b)
    return _mm(ah, bh, dims) + (_mm(ah, bl, dims) + _mm(al, bh, dims))


def _dotx(a, e, dims=_NN):
    ah, al = _split2(a)
    return _mm(ah, e, dims) + _mm(al, e, dims)


def _dotx3(a, e, dims=_NN):
    h1, h2, h3 = _split3(a)
    return _mm(h1, e, dims) + (_mm(h2, e, dims) + _mm(h3, e, dims))


def _params(sem):
    return pltpu.CompilerParams(dimension_semantics=sem, vmem_limit_bytes=V7X_VMEM_LIMIT)


def _sigmoid(x):
    return jax.nn.sigmoid(x)


def _head_ones(width):
    i = np.arange(width) // HEAD_DIM
    return jnp.asarray(i[:, None] == i[None, :], dtype=BF16)


def _ada_body(c_ref, w_ref, b_ref, o_ref):
    c = c_ref[...]
    o_ref[0] = _dot3(c * _sigmoid(c), w_ref[0]) + b_ref[0]


def _ada(c_all, w_ada, b_ada):
    depth, d, n = w_ada.shape
    m = c_all.shape[0]
    tn = 1536
    return pl.pallas_call(
        _ada_body,
        grid=(depth, n // tn),
        in_specs=[pl.BlockSpec((m, d), lambda l, j: (0, 0)),
                  pl.BlockSpec((1, d, tn), lambda l, j: (l, 0, j)),
                  pl.BlockSpec((1, 1, tn), lambda l, j: (l, 0, j))],
        out_specs=pl.BlockSpec((1, m, tn), lambda l, j: (l, 0, j)),
        out_shape=jax.ShapeDtypeStruct((depth, m, n), F32),
        compiler_params=_params(("parallel", "parallel")),
        name="ada_mod",
    )(c_all, w_ada, b_ada.reshape(depth, 1, n))


def _rel_bucket(dist):
    n = jnp.maximum(dist, 0)
    max_exact = N_BUCKETS // 2
    nf = jnp.maximum(n, 1).astype(F32)
    large = max_exact + (jnp.log(nf / max_exact) / math.log(REL_MAX_DIST / max_exact)
                         * (N_BUCKETS - max_exact)).astype(jnp.int32)
    large = jnp.minimum(large, N_BUCKETS - 1)
    return jnp.where(n < max_exact, n, large)


def _bias_tiles_body(rb_ref, o_ref):
    h = pl.program_id(0)
    d = pl.program_id(1)
    ti = lax.broadcasted_iota(jnp.int32, (MOBA_BLOCK, MOBA_BLOCK), 0)
    tj = lax.broadcasted_iota(jnp.int32, (MOBA_BLOCK, MOBA_BLOCK), 1)
    dist = d * MOBA_BLOCK + ti - tj
    bucket = _rel_bucket(dist)
    acc = jnp.zeros((MOBA_BLOCK, MOBA_BLOCK), F32)
    for b in range(N_BUCKETS):
        acc = jnp.where(bucket == b, rb_ref[h * N_BUCKETS + b], acc)
    o_ref[0, 0] = jnp.where(dist >= 0, acc * LOG2_E, NEG_BIG)


def _bias_tiles(rel_bias):
    rb = rel_bias.T.reshape(-1)
    return pl.pallas_call(
        _bias_tiles_body,
        grid=(N_HEADS_B, N_BIAS_TILES),
        in_specs=[pl.BlockSpec(memory_space=pltpu.SMEM)],
        out_specs=pl.BlockSpec((1, 1, MOBA_BLOCK, MOBA_BLOCK), lambda h, d: (h, d, 0, 0)),
        out_shape=jax.ShapeDtypeStruct((N_HEADS_B, N_BIAS_TILES, MOBA_BLOCK, MOBA_BLOCK), F32),
        compiler_params=_params(("parallel", "parallel")),
        name="bias_tiles",
    )(rb)


def _bias_pages_body(past_len, rb_ref, o_ref):
    page_size = o_ref.shape[-1]
    pos = pl.program_id(0) * page_size + lax.broadcasted_iota(jnp.int32, o_ref.shape[1:], 2)
    bucket = _rel_bucket(past_len - pos)
    acc = jnp.zeros(o_ref.shape[1:], F32)
    for b in range(N_BUCKETS):
        acc = jnp.where(bucket == b, rb_ref[b], acc)
    o_ref[0] = acc


def _bias_pages(rel_bias, past_len, page_size):
    n_pages = past_len // page_size + 1
    rb = jnp.broadcast_to(rel_bias[:, :, None, None], (N_BUCKETS, N_HEADS_B, 1, page_size))
    return pl.pallas_call(
        functools.partial(_bias_pages_body, past_len),
        grid=(n_pages,),
        in_specs=[pl.BlockSpec((N_BUCKETS, N_HEADS_B, 1, page_size), lambda p: (0, 0, 0, 0))],
        out_specs=pl.BlockSpec((1, N_HEADS_B, 1, page_size), lambda p: (p, 0, 0, 0)),
        out_shape=jax.ShapeDtypeStruct((n_pages, N_HEADS_B, 1, page_size), F32),
        compiler_params=_params(("parallel",)),
        name="bias_pages",
    )(rb)


def _inproj_body(x_ref, mod_ref, gain_ref, w_ref, *out_refs):
    x = x_ref[0]
    d = x.shape[-1]
    y = x * lax.rsqrt(jnp.mean(x * x, axis=-1, keepdims=True) + EPS) * gain_ref[...]
    sh = mod_ref[0, :, 0:d]
    sc = mod_ref[0, :, d:2 * d]
    h = (y * (1.0 + sc) + sh).astype(BF16)
    for ref, a, b in zip(out_refs, IN_SEGS[:-1], IN_SEGS[1:]):
        ref[0] = _mm(h, w_ref[0, :, a:b])


def _inproj(x, mod, gain, w_in_bf, layer, per_row):
    nb, t, d = x.shape
    tm = min(t, 512)
    mod_rows = tm if per_row else 1
    widths = [b - a for a, b in zip(IN_SEGS[:-1], IN_SEGS[1:])]
    return pl.pallas_call(
        _inproj_body,
        grid=(nb, t // tm),
        in_specs=[pl.BlockSpec((1, tm, d), lambda b, i: (b, i, 0)),
                  pl.BlockSpec((1, mod_rows, 6 * d), lambda b, i: (b, i if per_row else 0, 0)),
                  pl.BlockSpec((1, d), lambda b, i: (0, 0)),
                  pl.BlockSpec((1, d, N_IN), lambda b, i: (layer, 0, 0))],
        out_specs=[pl.BlockSpec((1, tm, w), lambda b, i: (b, i, 0)) for w in widths],
        out_shape=[jax.ShapeDtypeStruct((nb, t, w), F32) for w in widths],
        compiler_params=_params(("parallel", "parallel")),
        name="in_proj",
    )(x, mod, gain.reshape(1, d), w_in_bf)


def _rwkv_prep_body(per_row, z_ref, prev_ref, mu_ref, w0_ref, a0_ref, kk_ref, ka_ref, rk_ref,
                    w2a_ref, g2_ref, e_ref, r_o, ld_o, k_o, v_o, kn_o, b_o, g_o, bon_o, *rest):
    z = z_ref[0]
    tm = z.shape[0]
    if per_row:
        zp = prev_ref[0]
    else:
        sh_o, carry = rest
        @pl.when(pl.program_id(1) == 0)
        def _():
            carry[...] = prev_ref[0]
        row = lax.broadcasted_iota(jnp.int32, z.shape, 0)
        zp = jnp.where(row == 0, carry[...], pltpu.roll(z, 1, axis=0))
        carry[...] = z[tm - 1:tm, :]
        sh_o[0] = z[tm - 1:tm, :]
    zs = z + mu_ref[...] * (zp - z)
    r = zs[:, 0:D_A]
    k = zs[:, D_A:2 * D_A]
    v = zs[:, 2 * D_A:3 * D_A]
    zwa = zs[:, 3 * D_A:3 * D_A + LORA_W + LORA_A]
    zg = zs[:, 3 * D_A + LORA_W + LORA_A:]
    lane = lax.broadcasted_iota(jnp.int32, zwa.shape, 1)
    lwa = _dot3(jnp.where(lane < LORA_W, jnp.tanh(zwa), zwa), w2a_ref[...])
    xw = -(w0_ref[...] + lwa[:, :D_A])
    softplus = jnp.maximum(xw, 0.0) + jnp.log1p(jnp.exp(-jnp.abs(xw)))
    w_log = -softplus - 0.5
    a = _sigmoid(a0_ref[...] + lwa[:, D_A:])
    g = _dot3(_sigmoid(zg), g2_ref[...])
    e = e_ref[...]
    kkr = k * kk_ref[...]
    kn = kkr / jnp.maximum(jnp.sqrt(_dotx(kkr * kkr, e)), 1e-12)
    k2 = k * (1.0 + (a - 1.0) * ka_ref[...])
    r_o[0] = r
    ld_o[0] = -jnp.exp(w_log)
    k_o[0] = k2
    v_o[0] = v
    kn_o[0] = kn
    b_o[0] = kn * a
    g_o[0] = g
    bon_o[0] = _dotx(r * k2 * rk_ref[...], e) * v


def _rwkv_prep(z, prev, p, per_row):
    nb, t, w = z.shape
    tm = min(t, 512)
    row = lambda x: x.reshape(1, -1)
    w2a = jnp.zeros((LORA_W + LORA_A, 2 * D_A), F32)
    w2a = w2a.at[:LORA_W, :D_A].set(p['w2']).at[LORA_W:, D_A:].set(p['a2'])
    const = lambda shape: pl.BlockSpec(shape, lambda b, i: (0,) * len(shape))
    tile = lambda width: pl.BlockSpec((1, tm, width), lambda b, i: (b, i, 0))
    prev_spec = tile(w) if per_row else pl.BlockSpec((1, 1, w), lambda b, i: (b, 0, 0))
    out_specs = [tile(D_A)] * 8
    out_shape = [jax.ShapeDtypeStruct((nb, t, D_A), F32)] * 8
    scratch = []
    if not per_row:
        out_specs = out_specs + [pl.BlockSpec((1, 1, w), lambda b, i: (b, 0, 0))]
        out_shape = out_shape + [jax.ShapeDtypeStruct((nb, 1, w), F32)]
        scratch = [pltpu.VMEM((1, w), F32)]
    return pl.pallas_call(
        functools.partial(_rwkv_prep_body, per_row),
        grid=(nb, t // tm),
        in_specs=[tile(w), prev_spec, const((1, w)), const((1, D_A)), const((1, D_A)), const((1, D_A)),
                  const((1, D_A)), const((1, D_A)), const((LORA_W + LORA_A, 2 * D_A)),
                  const((LORA_G, D_A)), const((D_A, D_A))],
        out_specs=out_specs,
        out_shape=out_shape,
        scratch_shapes=scratch,
        compiler_params=_params(("parallel", "arbitrary")),
        name="rwkv_prep",
    )(z, prev, row(p['mu_shift']), row(p['w0']), row(p['a0']), row(p['k_k']), row(p['k_a']),
      row(p['r_k']), w2a, p['g2'], _head_ones(D_A))


_BNN = (((2,), (1,)), ((0,), (0,)))
_BNT = (((2,), (2,)), ((0,), (0,)))
_BTN = (((1,), (1,)), ((0,), (0,)))


def _wkv_chunk(s0, kt, rt, kh, bh, kb, bb, vv, gam, tri_s, tri_i, eye):
    c = kt.shape[1]
    lhs = jnp.concatenate([kt, rt], axis=1)
    aa = _dot1(lhs, jnp.concatenate([kh, bh], axis=1), _BNT)
    a_kk = jnp.where(tri_s, aa[:, :c, :c], 0.0)
    a_kb = jnp.where(tri_s, aa[:, :c, c:], 0.0)
    a_rk = jnp.where(tri_i, aa[:, c:, :c], 0.0)
    a_rb = jnp.where(tri_i, aa[:, c:, c:], 0.0)
    x = -a_kb
    inv = eye + x
    span = 2
    while span < c:
        x = _dot1(x, x, _BNN)
        inv = inv + _dot1(inv, x, _BNN)
        span *= 2
    ks = _dot1(lhs, s0, _BNT)
    av = _dot1(jnp.concatenate([a_kk, a_rk], axis=1), vv, _BNN)
    u = _dot1(inv, ks[:, :c] + av[:, :c], _BNN)
    y = ks[:, c:] + av[:, c:] - _dot1(a_rb, u, _BNN)
    s1 = s0 * gam + _dot1(jnp.concatenate([vv, -u], axis=1), jnp.concatenate([kb, bb], axis=1), _BTN)
    return y, s1


def _wkv_scan_body(r_ref, ld_ref, k_ref, v_ref, kn_ref, b_ref, g_ref, bon_ref, lw_ref, lb_ref,
                   o_ref, s_out_ref, s_sc):
    nb, tb, _ = r_ref.shape
    c = WKV_CHUNK
    step = pl.program_id(0)

    @pl.when(step == 0)
    def _():
        s_sc[...] = jnp.zeros_like(s_sc)

    ri = lax.broadcasted_iota(jnp.int32, (c, c), 0)
    ci = lax.broadcasted_iota(jnp.int32, (c, c), 1)
    tri_s = ri > ci
    tri_i = ri >= ci
    cum = tri_i.astype(BF16)
    eye = (ri == ci).astype(F32)

    def heads(x):
        return jnp.stack([x[bi][:, h * HEAD_DIM:(h + 1) * HEAD_DIM]
                          for bi in range(nb) for h in range(N_HEADS_A)])

    def chunk(ic, carry):
        rows = pl.ds(pl.multiple_of(ic * c, c), c)
        ld = ld_ref[:, rows, :]
        gcum = []
        for bi in range(nb):
            l1, l2, l3 = _split3(ld[bi])
            gcum.append(_mm(cum, l1) + (_mm(cum, l2) + _mm(cum, l3)))
        gcum = jnp.stack(gcum)
        e_in = jnp.exp(gcum)
        e_ex = jnp.exp(gcum - ld)
        e_ng = jnp.exp(-gcum)
        rt = r_ref[:, rows, :] * e_in
        kt = kn_ref[:, rows, :] * e_ex
        kh = k_ref[:, rows, :] * e_ng
        bh = b_ref[:, rows, :] * e_ng
        gam = e_in[:, c - 1:c, :]
        y, s1 = _wkv_chunk(s_sc[...], heads(kt), heads(rt), heads(kh), heads(bh), heads(kh * gam),
                           heads(bh * gam), heads(v_ref[:, rows, :]), heads(gam), tri_s, tri_i, eye)
        s_sc[...] = s1
        mu = jnp.mean(y, axis=-1, keepdims=True)
        var = jnp.mean(jnp.square(y - mu), axis=-1, keepdims=True)
        yn = (y - mu) * lax.rsqrt(var + LNX_EPS)
        for bi in range(nb):
            for h in range(N_HEADS_A):
                sl = slice(h * HEAD_DIM, (h + 1) * HEAD_DIM)
                o_ref[bi, rows, sl] = ((yn[bi * N_HEADS_A + h] * lw_ref[:, sl] + lb_ref[:, sl]
                                        + bon_ref[bi, rows, sl]) * g_ref[bi, rows, sl])
        return carry

    lax.fori_loop(0, tb // c, chunk, 0, unroll=True)

    @pl.when(step == pl.num_programs(0) - 1)
    def _():
        s_out_ref[...] = s_sc[...]


def _wkv_scan(r, ld, k, v, kn, b, g, bon, lnx_w, lnx_b):
    nb, t, _ = r.shape
    tb = min(t, 256)
    tile = pl.BlockSpec((nb, tb, D_A), lambda i: (0, i, 0))
    vec = pl.BlockSpec((1, D_A), lambda i: (0, 0))
    st = (nb * N_HEADS_A, HEAD_DIM, HEAD_DIM)
    return pl.pallas_call(
        _wkv_scan_body,
        grid=(t // tb,),
        in_specs=[tile] * 8 + [vec, vec],
        out_specs=[tile, pl.BlockSpec(st, lambda i: (0, 0, 0))],
        out_shape=[jax.ShapeDtypeStruct((nb, t, D_A), F32), jax.ShapeDtypeStruct(st, F32)],
        scratch_shapes=[pltpu.VMEM(st, F32)],
        compiler_params=_params(("arbitrary",)),
        name="wkv_scan",
    )(r, ld, k, v, kn, b, g, bon, lnx_w.reshape(1, D_A), lnx_b.reshape(1, D_A))


def _wkv_step_body(s_ref, r_ref, ld_ref, k_ref, v_ref, kn_ref, b_ref, g_ref, bon_ref, lw_ref, lb_ref,
                   o_ref, s_out_ref):
    s0 = s_ref[0]
    ri = lax.broadcasted_iota(jnp.int32, (HEAD_DIM, HEAD_DIM), 0)
    ci = lax.broadcasted_iota(jnp.int32, (HEAD_DIM, HEAD_DIM), 1)
    eye = (ri == ci).astype(F32)
    u = jnp.sum(s0 * kn_ref[...], axis=-1, keepdims=True)
    v_col = jnp.sum(eye * v_ref[...], axis=-1, keepdims=True)
    s1 = s0 * jnp.exp(ld_ref[...]) - u * b_ref[...] + v_col * k_ref[...]
    s_out_ref[0] = s1
    y_col = jnp.sum(s1 * r_ref[...], axis=-1, keepdims=True)
    y = jnp.sum(eye * y_col, axis=-2, keepdims=True)
    mu = jnp.mean(y, axis=-1, keepdims=True)
    var = jnp.mean(jnp.square(y - mu), axis=-1, keepdims=True)
    yn = (y - mu) * lax.rsqrt(var + LNX_EPS)
    o_ref[...] = (yn * lw_ref[...] + lb_ref[...] + bon_ref[...]) * g_ref[...]


def _wkv_step(state_wkv, layer, r, ld, k, v, kn, b, g, bon, lnx_w, lnx_b):
    n = r.shape[0]
    tb = 8
    heads = lambda x: x.reshape(n, N_HEADS_A, 1, HEAD_DIM)
    row = pl.BlockSpec((tb, N_HEADS_A, 1, HEAD_DIM), lambda i: (i, 0, 0, 0))
    vec = pl.BlockSpec((1, N_HEADS_A, 1, HEAD_DIM), lambda i: (0, 0, 0, 0))
    st_in = pl.BlockSpec((1, tb, N_HEADS_A, HEAD_DIM, HEAD_DIM), lambda i: (layer, i, 0, 0, 0))
    st_out = pl.BlockSpec((1, tb, N_HEADS_A, HEAD_DIM, HEAD_DIM), lambda i: (0, i, 0, 0, 0))
    out, s1 = pl.pallas_call(
        _wkv_step_body,
        grid=(n // tb,),
        in_specs=[st_in] + [row] * 8 + [vec, vec],
        out_specs=[row, st_out],
        out_shape=[jax.ShapeDtypeStruct((n, N_HEADS_A, 1, HEAD_DIM), F32),
                   jax.ShapeDtypeStruct((1, n, N_HEADS_A, HEAD_DIM, HEAD_DIM), F32)],
        compiler_params=_params(("parallel",)),
        name="wkv_step",
    )(state_wkv, *[heads(x) for x in (r, ld, k, v, kn, b, g, bon)],
      lnx_w.reshape(1, N_HEADS_A, 1, HEAD_DIM), lnx_b.reshape(1, N_HEADS_A, 1, HEAD_DIM))
    return out.reshape(n, D_A), s1[0]


def _head_tile(x, h, lane):
    pair = x[:, (h // 2) * 128:(h // 2 + 1) * 128]
    if h % 2:
        pair = pltpu.roll(pair, HEAD_DIM, axis=1)
    return jnp.where(lane < HEAD_DIM, pair, 0.0)


def _qk_norm(x, gain, e):
    return x * lax.rsqrt(_dotx(x * x, e) * (1.0 / HEAD_DIM) + EPS) * gain


MOBA_DEN_LANE = HEAD_DIM
MOBA_MAX_BLOCKS = 128 - HEAD_DIM
MOBA_GROUP = 8
MOBA_QBLOCKS = 2


def _moba_prep_body(q_ref, k_ref, v_ref, qg_ref, kg_ref, e_ref, qa_o, kh_o, vh_o, kn_o, km_sc):
    n = pl.program_id(1)

    @pl.when(n == 0)
    def _():
        km_sc[...] = jnp.zeros_like(km_sc)

    e = e_ref[...]
    qn = _qk_norm(q_ref[0], qg_ref[...], e)
    kn = _qk_norm(k_ref[0], kg_ref[...], e)
    v = v_ref[0]
    kn_o[0] = kn
    tq = qn.shape[0]
    lane = lax.broadcasted_iota(jnp.int32, (tq, 128), 1)
    lane_km = lax.broadcasted_iota(jnp.int32, (MOBA_MAX_BLOCKS, 128), 1)
    blk = lax.broadcasted_iota(jnp.int32, (MOBA_MAX_BLOCKS, tq), 0)
    valid = blk < n
    key_marks = jnp.where(lane == HEAD_DIM + n, 1.0, 0.0)
    val_marks = jnp.where(lane == MOBA_DEN_LANE, 1.0, 0.0)
    km = km_sc[...]
    for h in range(N_HEADS_B):
        qt = _head_tile(qn, h, lane)
        gate = jnp.where(valid, _dot3(_head_tile(km, h, lane_km), qt, _NT), -jnp.inf)
        keep = blk == n
        for _ in range(MOBA_TOPK):
            top = jnp.max(gate, axis=0, keepdims=True)
            first = jnp.min(jnp.where(gate == top, blk, 1 << 20), axis=0, keepdims=True)
            pick = blk == first
            keep = jnp.logical_or(keep, jnp.logical_and(pick, valid))
            gate = jnp.where(pick, -jnp.inf, gate)
        pen = jnp.where(keep, 0.0, NEG_BIG)
        pen = jnp.concatenate([jnp.zeros((HEAD_DIM, tq), F32), pen], axis=0).T
        qa_o[0, h] = jnp.where(lane < HEAD_DIM, qt * (HEAD_DIM ** -0.5 * LOG2_E), pen).astype(BF16)
        kh_o[0, h] = (_head_tile(kn, h, lane) + key_marks).astype(BF16)
        vh_o[0, h] = (_head_tile(v, h, lane) + val_marks).astype(BF16)
    km_sc[pl.ds(n, 1), :] = jnp.mean(kn, axis=0, keepdims=True)


def _moba_prep(q, k, v, q_gain, k_gain):
    nb, t, _ = q.shape
    nblk = t // MOBA_BLOCK
    assert nblk <= MOBA_MAX_BLOCKS and nblk % MOBA_GROUP == 0 and nblk % MOBA_QBLOCKS == 0
    tile = pl.BlockSpec((1, MOBA_BLOCK, D_B), lambda b, i: (b, i, 0))
    vec = pl.BlockSpec((1, D_B), lambda b, i: (0, 0))
    hm = pl.BlockSpec((1, N_HEADS_B, MOBA_BLOCK, 128), lambda b, i: (b, 0, i, 0))
    return pl.pallas_call(
        _moba_prep_body,
        grid=(nb, nblk),
        in_specs=[tile, tile, tile, vec, vec, pl.BlockSpec((D_B, D_B), lambda b, i: (0, 0))],
        out_specs=[hm, hm, hm, tile],
        out_shape=[jax.ShapeDtypeStruct((nb, N_HEADS_B, t, 128), BF16)] * 3
                  + [jax.ShapeDtypeStruct((nb, t, D_B), F32)],
        scratch_shapes=[pltpu.VMEM((MOBA_MAX_BLOCKS, D_B), F32)],
        compiler_params=_params(("parallel", "arbitrary")),
        name="moba_prep",
    )(q, k, v, jnp.tile(q_gain, N_HEADS_B).reshape(1, D_B), jnp.tile(k_gain, N_HEADS_B).reshape(1, D_B),
      _head_ones(D_B))


def _moba_attn_body(q_ref, k_ref, v_ref, tab_ref, o_ref, s_sc, mx_sc, acc_sc):
    first = pl.program_id(2) * MOBA_QBLOCKS
    n_groups = (first + MOBA_QBLOCKS - 1 + MOBA_GROUP) // MOBA_GROUP
    span = MOBA_GROUP * MOBA_BLOCK

    def key_rows(g):
        return pl.ds(pl.multiple_of(g * span, span), span)

    def q_rows(r):
        return slice(r * MOBA_BLOCK, (r + 1) * MOBA_BLOCK)

    mx_sc[...] = jnp.full_like(mx_sc, -jnp.inf)

    def pass1(g, carry):
        kg = k_ref[0, 0, key_rows(g), :]
        for r in range(MOBA_QBLOCKS):
            s = _mm(q_ref[0, 0, q_rows(r), :], kg, _NT)
            m = mx_sc[r]
            for i in range(MOBA_GROUP):
                cols = slice(i * MOBA_BLOCK, (i + 1) * MOBA_BLOCK)
                offset = first + r - (g * MOBA_GROUP + i)
                si = s[:, cols] + tab_ref[0, jnp.clip(offset, 0, N_BIAS_TILES - 1)]
                s_sc[g, q_rows(r), cols] = si
                m = jnp.maximum(m, si)
            mx_sc[r] = m
        return carry

    lax.fori_loop(0, n_groups, pass1, 0)
    m = [jnp.max(mx_sc[r], axis=1, keepdims=True) for r in range(MOBA_QBLOCKS)]
    acc_sc[...] = jnp.zeros_like(acc_sc)

    def pass2(g, carry):
        vg = v_ref[0, 0, key_rows(g), :]
        for r in range(MOBA_QBLOCKS):
            p = jnp.exp2(s_sc[g, q_rows(r), :] - m[r]).astype(BF16)
            acc_sc[q_rows(r), :] += _mm(p, vg)
        return carry

    lax.fori_loop(0, n_groups, pass2, 0)
    acc = acc_sc[...]
    o_ref[0, 0] = acc / acc[:, MOBA_DEN_LANE:MOBA_DEN_LANE + 1]


def _moba_attn(qa, kh, vh, bias_tiles):
    nb, nh, t, _ = qa.shape
    tq = MOBA_QBLOCKS * MOBA_BLOCK
    n_groups_max = t // (MOBA_GROUP * MOBA_BLOCK)
    return pl.pallas_call(
        _moba_attn_body,
        grid=(nh, nb, t // tq),
        in_specs=[pl.BlockSpec((1, 1, tq, 128), lambda h, b, a: (b, h, a, 0)),
                  pl.BlockSpec((1, 1, t, 128), lambda h, b, a: (b, h, 0, 0)),
                  pl.BlockSpec((1, 1, t, 128), lambda h, b, a: (b, h, 0, 0)),
                  pl.BlockSpec((1, N_BIAS_TILES, MOBA_BLOCK, MOBA_BLOCK), lambda h, b, a: (h, 0, 0, 0))],
        out_specs=pl.BlockSpec((1, 1, tq, 128), lambda h, b, a: (b, h, a, 0)),
        out_shape=jax.ShapeDtypeStruct((nb, nh, t, 128), F32),
        scratch_shapes=[pltpu.VMEM((n_groups_max, tq, MOBA_GROUP * MOBA_BLOCK), F32),
                        pltpu.VMEM((MOBA_QBLOCKS, MOBA_BLOCK, MOBA_BLOCK), F32),
                        pltpu.VMEM((tq, 128), F32)],
        compiler_params=_params(("parallel", "parallel", "arbitrary")),
        name="moba_attn",
    )(qa, kh, vh, bias_tiles)


def _qk_norm_rows_body(q_ref, k_ref, qg_ref, kg_ref, e_ref, qn_o, kn_o):
    e = e_ref[...]
    qn_o[...] = _qk_norm(q_ref[...], qg_ref[...], e)
    kn_o[...] = _qk_norm(k_ref[...], kg_ref[...], e)


def _qk_norm_rows(q, k, q_gain, k_gain):
    n = q.shape[0]
    return pl.pallas_call(
        _qk_norm_rows_body,
        out_shape=[jax.ShapeDtypeStruct((n, D_B), F32)] * 2,
        name="qk_norm_rows",
    )(q, k, jnp.tile(q_gain, N_HEADS_B).reshape(1, D_B), jnp.tile(k_gain, N_HEADS_B).reshape(1, D_B),
      _head_ones(D_B))


def _moba_paged_body(n_pages, pt_ref, qb_ref, knb_ref, vnb_ref, tab_ref, *rest):
    k_pages = rest[:n_pages]
    v_pages = rest[n_pages:2 * n_pages]
    o_ref = rest[2 * n_pages]
    page_size = qb_ref.shape[-1]
    per_blk = MOBA_BLOCK // page_size
    n_blk = n_pages // per_blk
    qb = qb_ref[0]
    scale = HEAD_DIM ** -0.5
    raw = [jnp.sum(k_pages[p][0, 0] * qb, axis=1, keepdims=True) for p in range(n_pages)]

    gates = []
    for n in range(n_blk):
        tot = raw[n * per_blk]
        for i in range(1, per_blk):
            tot = tot + raw[n * per_blk + i]
        gates.append(jnp.sum(tot, axis=2, keepdims=True) * (1.0 / MOBA_BLOCK))
    keep = [jnp.zeros(gates[0].shape, jnp.bool_)] * n_blk
    for _ in range(min(MOBA_TOPK, n_blk + 1)):
        top = functools.reduce(jnp.maximum, gates)
        found = jnp.zeros(top.shape, jnp.bool_)
        for n in range(n_blk):
            hit = jnp.logical_and(gates[n] == top, jnp.logical_not(found))
            keep[n] = jnp.logical_or(keep[n], hit)
            found = jnp.logical_or(found, hit)
            gates[n] = jnp.where(hit, -jnp.inf, gates[n])

    logits = [raw[p] * scale + tab_ref[p] + jnp.where(keep[p // per_blk], 0.0, NEG_BIG)
              for p in range(n_pages)]
    own = jnp.sum(knb_ref[0] * qb, axis=1, keepdims=True) * scale + tab_ref[n_pages]
    m = jnp.max(functools.reduce(jnp.maximum, logits + [own]), axis=2, keepdims=True)
    lane = lax.broadcasted_iota(jnp.int32, own.shape, 2)
    p_own = jnp.where(lane == 0, jnp.exp(own - m), 0.0)
    probs = [jnp.exp(s - m) for s in logits]
    inv_l = 1.0 / jnp.sum(functools.reduce(jnp.add, probs + [p_own]), axis=2, keepdims=True)
    acc = (p_own * inv_l) * vnb_ref[0]
    for p in range(n_pages):
        acc = acc + (probs[p] * inv_l) * v_pages[p][0, 0]
    h1, h2, h3 = _split3(acc.reshape(N_HEADS_B * HEAD_DIM, page_size))
    ones = jnp.ones((8, page_size), BF16)
    o_ref[0] = (_mm(ones, h1, _NT) + (_mm(ones, h2, _NT) + _mm(ones, h3, _NT)))[0:1]


def _moba_paged(qn, kn, v, cache_kt, cache_vt, page_table, layer, bias_pages):
    n, n_pages = page_table.shape
    page_size = cache_kt.shape[-1]
    tile = (N_HEADS_B, HEAD_DIM, page_size)
    lanes = lambda x: jnp.broadcast_to(x.reshape(n, N_HEADS_B, HEAD_DIM, 1), (n,) + tile)
    row = pl.BlockSpec((1,) + tile, lambda b, pt: (b, 0, 0, 0))
    page = lambda p: pl.BlockSpec((1, 1) + tile, lambda b, pt: (layer, pt[b, p], 0, 0, 0))
    out = pl.pallas_call(
        functools.partial(_moba_paged_body, n_pages),
        grid_spec=pltpu.PrefetchScalarGridSpec(
            num_scalar_prefetch=1,
            grid=(n,),
            in_specs=[row, row, row, pl.BlockSpec(bias_pages.shape, lambda b, pt: (0, 0, 0, 0))]
                     + [page(p) for p in range(n_pages)] * 2,
            out_specs=pl.BlockSpec((1, 1, D_B), lambda b, pt: (b, 0, 0))),
        out_shape=jax.ShapeDtypeStruct((n, 1, D_B), F32),
        compiler_params=_params(("arbitrary",)),
        name="moba_paged",
    )(page_table, lanes(qn), lanes(kn), lanes(v), bias_pages, *([cache_kt] * n_pages), *([cache_vt] * n_pages))
    return out.reshape(n, D_B)


def _gmlp_front(uv, vn_gain, e):
    ge = 0.5 * uv * (1.0 + lax.erf(uv * math.sqrt(0.5)))
    u = ge[:, :D_C]
    v = ge[:, D_C:]
    vg = v * lax.rsqrt(_dotx(v * v, e) * (1.0 / HEAD_DIM) + EPS) * vn_gain
    return u, vg


def _rms(x, gain):
    return x * lax.rsqrt(jnp.mean(x * x, axis=-1, keepdims=True) + EPS) * gain


def _gmlp_body(uv_ref, vn_ref, ws_ref, bs_ref, on_ref, e_ref, o_ref):
    u, vg = _gmlp_front(uv_ref[0], vn_ref[...], e_ref[...])
    tm = u.shape[0]
    ri = lax.broadcasted_iota(jnp.int32, (CHUNK_C, CHUNK_C), 0)
    ci = lax.broadcasted_iota(jnp.int32, (CHUNK_C, CHUNK_C), 1)
    group = lax.broadcasted_iota(jnp.int32, (CHUNK_C, D_C), 1) // HEAD_DIM
    ws = [jnp.where(ri >= ci, ws_ref[g], 0.0).astype(BF16) for g in range(N_GROUPS_C)]
    for c in range(tm // CHUNK_C):
        rows = slice(c * CHUNK_C, (c + 1) * CHUNK_C)
        vc = vg[rows].astype(BF16)
        mixed = bs_ref[...]
        for g in range(N_GROUPS_C):
            mixed = mixed + jnp.where(group == g, _mm(ws[g], vc), 0.0)
        o_ref[0, rows, :] = _rms(u[rows] * mixed, on_ref[...])


def _gmlp(uv, p):
    nb, t, _ = uv.shape
    tm = 512
    const = lambda shape: pl.BlockSpec(shape, lambda b, i: (0,) * len(shape))
    bs = jnp.repeat(p['b_s'].T, HEAD_DIM, axis=1)
    return pl.pallas_call(
        _gmlp_body,
        grid=(nb, t // tm),
        in_specs=[pl.BlockSpec((1, tm, 2 * D_C), lambda b, i: (b, i, 0)), const((1, D_C)),
                  const((N_GROUPS_C, CHUNK_C, CHUNK_C)), const((CHUNK_C, D_C)), const((1, D_C)),
                  const((D_C, D_C))],
        out_specs=pl.BlockSpec((1, tm, D_C), lambda b, i: (b, i, 0)),
        out_shape=jax.ShapeDtypeStruct((nb, t, D_C), F32),
        compiler_params=_params(("parallel", "parallel")),
        name="gmlp",
    )(uv, p['v_norm'].reshape(1, D_C), p['w_s'], bs, p['out_norm_c'].reshape(1, D_C), _head_ones(D_C))


def _gmlp_first_pos_body(uv_ref, vn_ref, w00_ref, b00_ref, on_ref, e_ref, o_ref, vg_ref):
    u, vg = _gmlp_front(uv_ref[...], vn_ref[...], e_ref[...])
    vg_ref[...] = vg
    o_ref[...] = _rms(u * (vg * w00_ref[...] + b00_ref[...]), on_ref[...])


def _gmlp_first_pos(uv, p):
    n = uv.shape[0]
    w00 = jnp.repeat(p['w_s'][:, 0, 0], HEAD_DIM).reshape(1, D_C)
    b00 = jnp.repeat(p['b_s'][:, 0], HEAD_DIM).reshape(1, D_C)
    return pl.pallas_call(
        _gmlp_first_pos_body,
        out_shape=[jax.ShapeDtypeStruct((n, D_C), F32)] * 2,
        name="gmlp_first_pos",
    )(uv, p['v_norm'].reshape(1, D_C), w00, b00, p['out_norm_c'].reshape(1, D_C), _head_ones(D_C))


def _mix_ffn_body(head_major, x_ref, oa_ref, ob_ref, oc_ref, mod_ref, nb_ref, nf_ref, wo_ref, wu_ref, wd_ref,
                  o_ref, x1_sc, h2_sc, acc_sc):
    j = pl.program_id(2)
    d = D_MODEL

    @pl.when(j == 0)
    def _():
        if head_major:
            lane = lax.broadcasted_iota(jnp.int32, ob_ref.shape[2:], 1)
            ob = jnp.concatenate(
                [jnp.where(lane < HEAD_DIM, ob_ref[0, h], pltpu.roll(ob_ref[0, h + 1], HEAD_DIM, axis=1))
                 for h in range(0, N_HEADS_B, 2)], axis=1)
        else:
            ob = ob_ref[0]
        cat = jnp.concatenate([oa_ref[0], _rms(ob, nb_ref[...]), oc_ref[0]], axis=1)
        x1 = x_ref[0] + mod_ref[0, :, 2 * d:3 * d] * _dot1(cat, wo_ref[0])
        x1_sc[...] = x1
        h2 = _rms(x1, nf_ref[...]) * (1.0 + mod_ref[0, :, 4 * d:5 * d]) + mod_ref[0, :, 3 * d:4 * d]
        h2_sc[...] = h2.astype(BF16)
        acc_sc[...] = jnp.zeros_like(acc_sc)

    up = _mm(h2_sc[...], wu_ref[0])
    acc_sc[...] += _mm(jnp.square(jnp.maximum(up, 0.0)).astype(BF16), wd_ref[0])

    @pl.when(j == pl.num_programs(2) - 1)
    def _():
        o_ref[0] = x1_sc[...] + mod_ref[0, :, 5 * d:6 * d] * acc_sc[...]


def _mix_ffn(x, oa, ob, oc, mod, p, w_out_bf, w_up_bf, w_down_bf, layer, per_row, head_major):
    nb, t, d = x.shape
    tm = min(t, 512)
    tf = 1024
    mod_rows = tm if per_row else 1
    tile = lambda width: pl.BlockSpec((1, tm, width), lambda b, i, j: (b, i, 0))
    const = lambda shape: pl.BlockSpec(shape, lambda b, i, j: (0,) * len(shape))
    if head_major:
        ob_spec = pl.BlockSpec((1, N_HEADS_B, tm, 128), lambda b, i, j: (b, 0, i, 0))
    else:
        ob_spec = tile(D_B)
    return pl.pallas_call(
        functools.partial(_mix_ffn_body, head_major),
        grid=(nb, t // tm, D_FF // tf),
        in_specs=[tile(d), tile(D_A), ob_spec, tile(D_C),
                  pl.BlockSpec((1, mod_rows, 6 * d), lambda b, i, j: (b, i if per_row else 0, 0)),
                  const((1, D_B)), const((1, d)),
                  pl.BlockSpec((1, d, d), lambda b, i, j: (layer, 0, 0)),
                  pl.BlockSpec((1, d, tf), lambda b, i, j: (layer, 0, j)),
                  pl.BlockSpec((1, tf, d), lambda b, i, j: (layer, j, 0))],
        out_specs=tile(d),
        out_shape=jax.ShapeDtypeStruct((nb, t, d), F32),
        scratch_shapes=[pltpu.VMEM((tm, d), F32), pltpu.VMEM((tm, d), BF16), pltpu.VMEM((tm, d), F32)],
        compiler_params=_params(("parallel", "parallel", "arbitrary")),
        name="mix_ffn",
    )(x, oa, ob, oc, mod, p['out_norm_b'].reshape(1, D_B), p['norm_ffn'].reshape(1, d),
      w_out_bf, w_up_bf, w_down_bf)


def _prompt_layer(x, mod, p, big, layer, bias_tiles):
    nb, t, _ = x.shape
    z, q, k, v, uv = _inproj(x, mod, p['norm_mix'], big['w_in'], layer, per_row=False)
    prep = _rwkv_prep(z, jnp.zeros((nb, 1, W_SHIFT), F32), p, per_row=False)
    shift_new = prep[8].reshape(nb, W_SHIFT)
    out_a, wkv_new = _wkv_scan(*prep[:8], p['lnx_w'], p['lnx_b'])
    wkv_new = wkv_new.reshape(nb, N_HEADS_A, HEAD_DIM, HEAD_DIM)
    qa, kh, vh, k_new = _moba_prep(q, k, v, p['q_norm'], p['k_norm'])
    out_b = _moba_attn(qa, kh, vh, bias_tiles)
    out_c = _gmlp(uv, p)
    x = _mix_ffn(x, out_a, out_b, out_c, mod, p, big['w_out'], big['w_up'], big['w_down'], layer,
                 per_row=False, head_major=True)
    return x, k_new, v, wkv_new, shift_new


def _sample_layer(x, mod, p, big, layer, bias_pages, cache_kt, cache_vt, page_table, state_wkv, shift0):
    n = x.shape[1]
    z, q, k, v, uv = _inproj(x, mod, p['norm_mix'], big['w_in'], layer, per_row=True)
    prep = _rwkv_prep(z, shift0.reshape(1, n, W_SHIFT), p, per_row=True)
    out_a, wkv_new = _wkv_step(state_wkv, layer, *[a.reshape(n, D_A) for a in prep], p['lnx_w'], p['lnx_b'])
    qn, kn = _qk_norm_rows(q.reshape(n, D_B), k.reshape(n, D_B), p['q_norm'], p['k_norm'])
    v = v.reshape(n, D_B)
    out_b = _moba_paged(qn, kn, v, cache_kt, cache_vt, page_table, layer, bias_pages)
    out_c, vg = _gmlp_first_pos(uv.reshape(n, 2 * D_C), p)
    x = _mix_ffn(x, out_a.reshape(1, n, D_A), out_b.reshape(1, n, D_B), out_c.reshape(1, n, D_C), mod, p,
                 big['w_out'], big['w_up'], big['w_down'], layer, per_row=True, head_major=False)
    return x, kn, v, wkv_new, z.reshape(n, W_SHIFT), vg


def kernel(x_prompt, x_sample, c_prompt, c_sample, cache_k, cache_v, page_table, state_wkv, state_shift, norm_mix, w_ada, b_ada, w_in, mu_shift, w0, w2, a0, a2, g2, k_k, k_a, r_k, lnx_w, lnx_b, q_norm, k_norm, rel_bias, out_norm_b, v_norm, w_s, b_s, out_norm_c, w_out, norm_ffn, w_up, w_down):
    layer_w = {
        'norm_mix': norm_mix, 'mu_shift': mu_shift, 'w0': w0, 'w2': w2, 'a0': a0, 'a2': a2, 'g2': g2,
        'k_k': k_k, 'k_a': k_a, 'r_k': r_k, 'lnx_w': lnx_w, 'lnx_b': lnx_b, 'q_norm': q_norm,
        'k_norm': k_norm, 'out_norm_b': out_norm_b, 'v_norm': v_norm, 'w_s': w_s, 'b_s': b_s,
        'out_norm_c': out_norm_c, 'norm_ffn': norm_ffn,
    }
    depth = w_in.shape[0]
    n_prompt, seq, d = x_prompt.shape
    n_dec = x_sample.shape[0]
    past_len = page_table.shape[1] * cache_k.shape[2]
    big = {'w_in': w_in.astype(BF16), 'w_out': w_out.astype(BF16),
           'w_up': w_up.astype(BF16), 'w_down': w_down.astype(BF16)}
    mod = _ada(jnp.concatenate([c_sample, c_prompt], axis=0), w_ada, b_ada)
    bias_tiles = _bias_tiles(rel_bias)
    bias_pages = _bias_pages(rel_bias, past_len, cache_k.shape[2])
    cache_kt = cache_k.transpose(0, 1, 3, 4, 2)
    cache_vt = cache_v.transpose(0, 1, 3, 4, 2)

    xp = x_prompt
    xs = x_sample.reshape(1, n_dec, d)
    outs = [[] for _ in range(9)]
    for l in range(depth):
        p = {name: arr[l] for name, arr in layer_w.items()}
        mod_s = mod[l, :n_dec].reshape(1, n_dec, 6 * d)
        mod_p = mod[l, n_dec:].reshape(n_prompt, 1, 6 * d)
        xp, k_p, v_p, wkv_p, sh_p = _prompt_layer(xp, mod_p, p, big, l, bias_tiles)
        xs, k_s, v_s, wkv_s, sh_s, vg_s = _sample_layer(
            xs, mod_s, p, big, l, bias_pages, cache_kt, cache_vt, page_table, state_wkv, state_shift[l])
        for lst, val in zip(outs, (k_p, v_p, k_s, v_s, wkv_p, wkv_s, sh_p, sh_s, vg_s)):
            lst.append(val)
    kp, vp, ks, vs, wp, ws, sp, ss, gs = (jnp.stack(o) for o in outs)
    heads = lambda a, rows: a.reshape(depth, rows, -1, N_HEADS_B, HEAD_DIM)
    return (xp, xs.reshape(n_dec, 1, d),
            heads(kp, n_prompt), heads(vp, n_prompt), heads(ks, n_dec), heads(vs, n_dec),
            wp, ws, sp, ss, gs.reshape(depth, n_dec, 1, D_C))
```

```python
import functools
import math

import numpy as np
import jax
import jax.numpy as jnp
from jax import lax
from jax.experimental import pallas as pl
from jax.experimental.pallas import tpu as pltpu

F32 = jnp.float32
BF16 = jnp.bfloat16

D_MODEL = 1024
HEAD_DIM = 64
N_HEADS_A = 6
N_HEADS_B = 6
N_GROUPS_C = 4
D_A = N_HEADS_A * HEAD_DIM
D_B = N_HEADS_B * HEAD_DIM
D_C = N_GROUPS_C * HEAD_DIM
LORA_W = 64
LORA_A = 64
LORA_G = 128
W_SHIFT = 3 * D_A + LORA_W + LORA_A + LORA_G
N_IN = W_SHIFT + 3 * D_B + 2 * D_C
IN_SEGS = (0, W_SHIFT, W_SHIFT + D_B, W_SHIFT + 2 * D_B, W_SHIFT + 3 * D_B, N_IN)
MOBA_BLOCK = 256
MOBA_TOPK = 3
CHUNK_C = 128
N_BUCKETS = 32
REL_MAX_DIST = 4096
D_FF = 4 * D_MODEL
EPS = 1e-6
LNX_EPS = 64e-5
WKV_CHUNK = 64
NEG_BIG = -1e30
LOG2_E = math.log2(math.e)
V7X_VMEM_LIMIT = 56 * 1024 * 1024


def _bucket_saturation_tiles():
    max_exact = N_BUCKETS // 2
    n = np.arange(1, 2 * REL_MAX_DIST, dtype=np.float64)
    large = max_exact + (np.log(n / max_exact) / math.log(REL_MAX_DIST / max_exact)
                         * (N_BUCKETS - max_exact)).astype(np.int64)
    first_sat = int(n[np.argmax(np.minimum(large, N_BUCKETS - 1) == N_BUCKETS - 1)])
    return -(-(first_sat + MOBA_BLOCK) // MOBA_BLOCK) + 1


N_BIAS_TILES = _bucket_saturation_tiles() + 1

_NN = (((1,), (0,)), ((), ()))
_NT = (((1,), (1,)), ((), ()))
_TN = (((0,), (0,)), ((), ()))


def _mm(a, b, dims=_NN):
    return lax.dot_general(a, b, dims, preferred_element_type=F32)


def _split2(x):
    hi = x.astype(BF16)
    lo = (x - hi.astype(F32)).astype(BF16)
    return hi, lo


def _split3(x):
    h1 = x.astype(BF16)
    r1 = x - h1.astype(F32)
    h2 = r1.astype(BF16)
    h3 = (r1 - h2.astype(F32)).astype(BF16)
    return h1, h2, h3


def _dot1(a, b, dims=_NN):
    return _mm(a.astype(BF16), b.astype(BF16), dims)


def _dot3(a, b, dims=_NN):
    ah, al = _split2(a)
    bh, bl = _split2(b)
    return _mm(ah, bh, dims) + (_mm(ah, bl, dims) + _mm(al, bh, dims))


def _dotx(a, e, dims=_NN):
    ah, al = _split2(a)
    return _mm(ah, e, dims) + _mm(al, e, dims)


def _dotx3(a, e, dims=_NN):
    h1, h2, h3 = _split3(a)
    return _mm(h1, e, dims) + (_mm(h2, e, dims) + _mm(h3, e, dims))


def _params(sem):
    return pltpu.CompilerParams(dimension_semantics=sem, vmem_limit_bytes=V7X_VMEM_LIMIT)


def _sigmoid(x):
    return jax.nn.sigmoid(x)


def _head_ones(width):
    i = np.arange(width) // HEAD_DIM
    return jnp.asarray(i[:, None] == i[None, :], dtype=BF16)


def _ada_body(c_ref, w_ref, b_ref, o_ref):
    c = c_ref[...]
    o_ref[0] = _dot3(c * _sigmoid(c), w_ref[0]) + b_ref[0]


def _ada(c_all, w_ada, b_ada):
    depth, d, n = w_ada.shape
    m = c_all.shape[0]
    tn = 1536
    return pl.pallas_call(
        _ada_body,
        grid=(depth, n // tn),
        in_specs=[pl.BlockSpec((m, d), lambda l, j: (0, 0)),
                  pl.BlockSpec((1, d, tn), lambda l, j: (l, 0, j)),
                  pl.BlockSpec((1, 1, tn), lambda l, j: (l, 0, j))],
        out_specs=pl.BlockSpec((1, m, tn), lambda l, j: (l, 0, j)),
        out_shape=jax.ShapeDtypeStruct((depth, m, n), F32),
        compiler_params=_params(("parallel", "parallel")),
        name="ada_mod",
    )(c_all, w_ada, b_ada.reshape(depth, 1, n))


def _rel_bucket(dist):
    n = jnp.maximum(dist, 0)
    max_exact = N_BUCKETS // 2
    nf = jnp.maximum(n, 1).astype(F32)
    large = max_exact + (jnp.log(nf / max_exact) / math.log(REL_MAX_DIST / max_exact)
                         * (N_BUCKETS - max_exact)).astype(jnp.int32)
    large = jnp.minimum(large, N_BUCKETS - 1)
    return jnp.where(n < max_exact, n, large)


def _bias_tiles_body(rb_ref, o_ref):
    h = pl.program_id(0)
    d = pl.program_id(1)
    ti = lax.broadcasted_iota(jnp.int32, (MOBA_BLOCK, MOBA_BLOCK), 0)
    tj = lax.broadcasted_iota(jnp.int32, (MOBA_BLOCK, MOBA_BLOCK), 1)
    dist = d * MOBA_BLOCK + ti - tj
    bucket = _rel_bucket(dist)
    acc = jnp.zeros((MOBA_BLOCK, MOBA_BLOCK), F32)
    for b in range(N_BUCKETS):
        acc = jnp.where(bucket == b, rb_ref[h * N_BUCKETS + b], acc)
    o_ref[0, 0] = jnp.where(dist >= 0, acc * LOG2_E, NEG_BIG)


def _bias_tiles(rel_bias):
    rb = rel_bias.T.reshape(-1)
    return pl.pallas_call(
        _bias_tiles_body,
        grid=(N_HEADS_B, N_BIAS_TILES),
        in_specs=[pl.BlockSpec(memory_space=pltpu.SMEM)],
        out_specs=pl.BlockSpec((1, 1, MOBA_BLOCK, MOBA_BLOCK), lambda h, d: (h, d, 0, 0)),
        out_shape=jax.ShapeDtypeStruct((N_HEADS_B, N_BIAS_TILES, MOBA_BLOCK, MOBA_BLOCK), F32),
        compiler_params=_params(("parallel", "parallel")),
        name="bias_tiles",
    )(rb)


def _bias_pages_body(past_len, rb_ref, o_ref):
    page_size = o_ref.shape[-1]
    pos = pl.program_id(0) * page_size + lax.broadcasted_iota(jnp.int32, o_ref.shape[1:], 2)
    bucket = _rel_bucket(past_len - pos)
    acc = jnp.zeros(o_ref.shape[1:], F32)
    for b in range(N_BUCKETS):
        acc = jnp.where(bucket == b, rb_ref[b], acc)
    o_ref[0] = acc


def _bias_pages(rel_bias, past_len, page_size):
    n_pages = past_len // page_size + 1
    rb = jnp.broadcast_to(rel_bias[:, :, None, None], (N_BUCKETS, N_HEADS_B, 1, page_size))
    return pl.pallas_call(
        functools.partial(_bias_pages_body, past_len),
        grid=(n_pages,),
        in_specs=[pl.BlockSpec((N_BUCKETS, N_HEADS_B, 1, page_size), lambda p: (0, 0, 0, 0))],
        out_specs=pl.BlockSpec((1, N_HEADS_B, 1, page_size), lambda p: (p, 0, 0, 0)),
        out_shape=jax.ShapeDtypeStruct((n_pages, N_HEADS_B, 1, page_size), F32),
        compiler_params=_params(("parallel",)),
        name="bias_pages",
    )(rb)


def _inproj_body(x_ref, mod_ref, gain_ref, w_ref, *out_refs):
    x = x_ref[0]
    d = x.shape[-1]
    y = x * lax.rsqrt(jnp.mean(x * x, axis=-1, keepdims=True) + EPS) * gain_ref[...]
    sh = mod_ref[0, :, 0:d]
    sc = mod_ref[0, :, d:2 * d]
    h = (y * (1.0 + sc) + sh).astype(BF16)
    for ref, a, b in zip(out_refs, IN_SEGS[:-1], IN_SEGS[1:]):
        ref[0] = _mm(h, w_ref[0, :, a:b])


def _inproj(x, mod, gain, w_in_bf, layer, per_row):
    nb, t, d = x.shape
    tm = min(t, 512)
    mod_rows = tm if per_row else 1
    widths = [b - a for a, b in zip(IN_SEGS[:-1], IN_SEGS[1:])]
    return pl.pallas_call(
        _inproj_body,
        grid=(nb, t // tm),
        in_specs=[pl.BlockSpec((1, tm, d), lambda b, i: (b, i, 0)),
                  pl.BlockSpec((1, mod_rows, 6 * d), lambda b, i: (b, i if per_row else 0, 0)),
                  pl.BlockSpec((1, d), lambda b, i: (0, 0)),
                  pl.BlockSpec((1, d, N_IN), lambda b, i: (layer, 0, 0))],
        out_specs=[pl.BlockSpec((1, tm, w), lambda b, i: (b, i, 0)) for w in widths],
        out_shape=[jax.ShapeDtypeStruct((nb, t, w), F32) for w in widths],
        compiler_params=_params(("parallel", "parallel")),
        name="in_proj",
    )(x, mod, gain.reshape(1, d), w_in_bf)


_BNN = (((2,), (1,)), ((0,), (0,)))
_BNT = (((2,), (2,)), ((0,), (0,)))
_BTN = (((1,), (1,)), ((0,), (0,)))


def _split_heads(x):
    return jnp.stack([x[:, h * HEAD_DIM:(h + 1) * HEAD_DIM] for h in range(x.shape[1] // HEAD_DIM)])


def _wkv_local(kt, rt, kh, bh, kb, bb, vv, tri_s, tri_i, eye):
    c = kt.shape[1]
    aa = _dot1(jnp.concatenate([kt, rt], axis=1), jnp.concatenate([kh, bh], axis=1), _BNT)
    a_kk = jnp.where(tri_s, aa[:, :c, :c], 0.0)
    a_kb = jnp.where(tri_s, aa[:, :c, c:], 0.0)
    a_rk = jnp.where(tri_i, aa[:, c:, :c], 0.0)
    a_rb = jnp.where(tri_i, aa[:, c:, c:], 0.0)
    x = -a_kb
    inv = eye + x
    span = 2
    while span < c:
        x = _dot1(x, x, _BNN)
        inv = inv + _dot1(inv, x, _BNN)
        span *= 2
    av = _dot1(jnp.concatenate([a_kk, a_rk], axis=1), vv, _BNN)
    pkw = _dot1(inv, jnp.concatenate([kt, av[:, :c]], axis=2), _BNN)
    arb = _dot1(a_rb, pkw, _BNN)
    mb = _dot1(pkw, bb, _BTN)
    rk = rt - arb[:, :, :HEAD_DIM]
    y0 = av[:, c:] - arb[:, :, HEAD_DIM:]
    sv = _dot1(vv, kb, _BTN) - mb[:, HEAD_DIM:]
    return mb[:, :HEAD_DIM], sv, rk, y0


def _rwkv_prep_body(per_row, z_ref, prev_ref, mu_ref, w0_ref, a0_ref, kk_ref, ka_ref, rk_ref,
                    w2a_ref, g2_ref, e_ref, *rest):
    z = z_ref[0]
    tm = z.shape[0]
    if per_row:
        zp = prev_ref[0]
    else:
        m2_o, sv_o, rkc_o, y0_o, gam_o, g_o, bon_o, sh_o, carry = rest
        @pl.when(pl.program_id(1) == 0)
        def _():
            carry[...] = prev_ref[0]
        row = lax.broadcasted_iota(jnp.int32, z.shape, 0)
        zp = jnp.where(row == 0, carry[...], pltpu.roll(z, 1, axis=0))
        carry[...] = z[tm - 1:tm, :]
        sh_o[0] = z[tm - 1:tm, :]
    zs = z + mu_ref[...] * (zp - z)
    r = zs[:, 0:D_A]
    k = zs[:, D_A:2 * D_A]
    v = zs[:, 2 * D_A:3 * D_A]
    zwa = zs[:, 3 * D_A:3 * D_A + LORA_W + LORA_A]
    zg = zs[:, 3 * D_A + LORA_W + LORA_A:]
    lane = lax.broadcasted_iota(jnp.int32, zwa.shape, 1)
    lwa = _dot3(jnp.where(lane < LORA_W, jnp.tanh(zwa), zwa), w2a_ref[...])
    xw = -(w0_ref[...] + lwa[:, :D_A])
    softplus = jnp.maximum(xw, 0.0) + jnp.log1p(jnp.exp(-jnp.abs(xw)))
    w_log = -softplus - 0.5
    a = _sigmoid(a0_ref[...] + lwa[:, D_A:])
    g = _dot3(_sigmoid(zg), g2_ref[...])
    e = e_ref[...]
    kkr = k * kk_ref[...]
    kn = kkr / jnp.maximum(jnp.sqrt(_dotx(kkr * kkr, e)), 1e-12)
    k2 = k * (1.0 + (a - 1.0) * ka_ref[...])
    ld = -jnp.exp(w_log)
    b = kn * a
    bon = _dotx(r * k2 * rk_ref[...], e) * v
    if per_row:
        for ref, val in zip(rest, (r, ld, k2, v, kn, b, g, bon)):
            ref[0] = val
        return
    g_o[0] = g
    bon_o[0] = bon
    c = WKV_CHUNK
    ri = lax.broadcasted_iota(jnp.int32, (c, c), 0)
    ci = lax.broadcasted_iota(jnp.int32, (c, c), 1)
    tri_s = ri > ci
    tri_i = ri >= ci
    cum = tri_i.astype(BF16)
    eye = (ri == ci).astype(F32)
    nch = tm // c
    chunks = lambda x: x.reshape(nch, c, x.shape[-1])

    def chains(x):
        return jnp.stack([x[:, :, h * HEAD_DIM:(h + 1) * HEAD_DIM]
                          for h in range(N_HEADS_A)]).reshape(N_HEADS_A * nch, c, HEAD_DIM)

    ldc = chunks(ld)
    cums = jnp.broadcast_to(cum, (nch, c, c))
    l1, l2, l3 = _split3(ldc)
    gcum = _mm(cums, l1, _BNN) + (_mm(cums, l2, _BNN) + _mm(cums, l3, _BNN))
    e_in = jnp.exp(gcum)
    e_ng = jnp.exp(-gcum)
    gam = e_in[:, c - 1:c, :]
    kh = chunks(k2) * e_ng
    bh = chunks(b) * e_ng
    outs = _wkv_local(chains(chunks(kn) * jnp.exp(gcum - ldc)), chains(chunks(r) * e_in), chains(kh), chains(bh),
                      chains(kh * gam), chains(bh * gam), chains(chunks(v)), tri_s, tri_i, eye)
    for ref, val in zip((m2_o, sv_o, rkc_o, y0_o), outs):
        ref[0] = val.reshape(N_HEADS_A, nch, c, HEAD_DIM)
    gam_o[0] = gam


def _rwkv_prep(z, prev, p, per_row):
    nb, t, w = z.shape
    tm = min(t, 512)
    row = lambda x: x.reshape(1, -1)
    w2a = jnp.zeros((LORA_W + LORA_A, 2 * D_A), F32)
    w2a = w2a.at[:LORA_W, :D_A].set(p['w2']).at[LORA_W:, D_A:].set(p['a2'])
    const = lambda shape: pl.BlockSpec(shape, lambda b, i: (0,) * len(shape))
    tile = lambda width: pl.BlockSpec((1, tm, width), lambda b, i: (b, i, 0))
    if per_row:
        prev_spec = tile(w)
        out_specs = [tile(D_A)] * 8
        out_shape = [jax.ShapeDtypeStruct((nb, t, D_A), F32)] * 8
        scratch = []
    else:
        cpt = tm // WKV_CHUNK
        prev_spec = pl.BlockSpec((1, 1, w), lambda b, i: (b, 0, 0))
        out_specs = ([pl.BlockSpec((1, N_HEADS_A, cpt, WKV_CHUNK, HEAD_DIM), lambda b, i: (b, 0, i, 0, 0))] * 4
                     + [pl.BlockSpec((1, cpt, 1, D_A), lambda b, i: (b, i, 0, 0)), tile(D_A), tile(D_A),
                        pl.BlockSpec((1, 1, w), lambda b, i: (b, 0, 0))])
        out_shape = ([jax.ShapeDtypeStruct((nb, N_HEADS_A, t // WKV_CHUNK, WKV_CHUNK, HEAD_DIM), F32)] * 4
                     + [jax.ShapeDtypeStruct((nb, t // WKV_CHUNK, 1, D_A), F32),
                        jax.ShapeDtypeStruct((nb, t, D_A), F32), jax.ShapeDtypeStruct((nb, t, D_A), F32),
                        jax.ShapeDtypeStruct((nb, 1, w), F32)])
        scratch = [pltpu.VMEM((1, w), F32)]
    return pl.pallas_call(
        functools.partial(_rwkv_prep_body, per_row),
        grid=(nb, t // tm),
        in_specs=[tile(w), prev_spec, const((1, w)), const((1, D_A)), const((1, D_A)), const((1, D_A)),
                  const((1, D_A)), const((1, D_A)), const((LORA_W + LORA_A, 2 * D_A)),
                  const((LORA_G, D_A)), const((D_A, D_A))],
        out_specs=out_specs,
        out_shape=out_shape,
        scratch_shapes=scratch,
        compiler_params=_params(("parallel", "arbitrary")),
        name="rwkv_prep",
    )(z, prev, row(p['mu_shift']), row(p['w0']), row(p['a0']), row(p['k_k']), row(p['k_a']),
      row(p['r_k']), w2a, p['g2'], _head_ones(D_A))


def _wkv_scan_body(m2_ref, sv_ref, rk_ref, y0_ref, gam_ref, g_ref, bon_ref, lw_ref, lb_ref,
                   o_ref, s_out_ref, s_sc):
    nb, _, cps = m2_ref.shape[:3]
    c = WKV_CHUNK
    step = pl.program_id(0)

    @pl.when(step == 0)
    def _():
        s_sc[...] = jnp.zeros_like(s_sc)

    merge = lambda x: x.reshape((nb * N_HEADS_A,) + x.shape[2:])
    for ic in range(cps):
        rows = slice(ic * c, (ic + 1) * c)
        s0 = s_sc[...]
        y = _dot1(merge(rk_ref[:, :, ic]), s0, _BNT) + merge(y0_ref[:, :, ic])
        gam = jnp.stack([gam_ref[bi, ic][:, h * HEAD_DIM:(h + 1) * HEAD_DIM]
                         for bi in range(nb) for h in range(N_HEADS_A)])
        s_sc[...] = s0 * gam - _dot1(s0, merge(m2_ref[:, :, ic]), _BNN) + merge(sv_ref[:, :, ic])
        mu = jnp.mean(y, axis=-1, keepdims=True)
        var = jnp.mean(jnp.square(y - mu), axis=-1, keepdims=True)
        yn = (y - mu) * lax.rsqrt(var + LNX_EPS)
        for bi in range(nb):
            for h in range(N_HEADS_A):
                sl = slice(h * HEAD_DIM, (h + 1) * HEAD_DIM)
                o_ref[bi, rows, sl] = ((yn[bi * N_HEADS_A + h] * lw_ref[:, sl] + lb_ref[:, sl]
                                        + bon_ref[bi, rows, sl]) * g_ref[bi, rows, sl])

    @pl.when(step == pl.num_programs(0) - 1)
    def _():
        s_out_ref[...] = s_sc[...]


def _wkv_scan(m2, sv, rk, y0, gam, g, bon, lnx_w, lnx_b):
    nb, t, _ = g.shape
    cps = 4
    tb = cps * WKV_CHUNK
    mat = pl.BlockSpec((nb, N_HEADS_A, cps, WKV_CHUNK, HEAD_DIM), lambda i: (0, 0, i, 0, 0))
    tile = pl.BlockSpec((nb, tb, D_A), lambda i: (0, i, 0))
    vec = pl.BlockSpec((1, D_A), lambda i: (0, 0))
    st = (nb * N_HEADS_A, HEAD_DIM, HEAD_DIM)
    return pl.pallas_call(
        _wkv_scan_body,
        grid=(t // tb,),
        in_specs=[mat] * 4 + [pl.BlockSpec((nb, cps, 1, D_A), lambda i: (0, i, 0, 0)), tile, tile, vec, vec],
        out_specs=[tile, pl.BlockSpec(st, lambda i: (0, 0, 0))],
        out_shape=[jax.ShapeDtypeStruct((nb, t, D_A), F32), jax.ShapeDtypeStruct(st, F32)],
        scratch_shapes=[pltpu.VMEM(st, F32)],
        compiler_params=_params(("arbitrary",)),
        name="wkv_scan",
    )(m2, sv, rk, y0, gam, g, bon, lnx_w.reshape(1, D_A), lnx_b.reshape(1, D_A))


def _wkv_step_body(s_ref, r_ref, ld_ref, k_ref, v_ref, kn_ref, b_ref, g_ref, bon_ref, lw_ref, lb_ref,
                   o_ref, s_out_ref):
    s0 = s_ref[0]
    ri = lax.broadcasted_iota(jnp.int32, (HEAD_DIM, HEAD_DIM), 0)
    ci = lax.broadcasted_iota(jnp.int32, (HEAD_DIM, HEAD_DIM), 1)
    eye = (ri == ci).astype(F32)
    u = jnp.sum(s0 * kn_ref[...], axis=-1, keepdims=True)
    v_col = jnp.sum(eye * v_ref[...], axis=-1, keepdims=True)
    s1 = s0 * jnp.exp(ld_ref[...]) - u * b_ref[...] + v_col * k_ref[...]
    s_out_ref[0] = s1
    y_col = jnp.sum(s1 * r_ref[...], axis=-1, keepdims=True)
    y = jnp.sum(eye * y_col, axis=-2, keepdims=True)
    mu = jnp.mean(y, axis=-1, keepdims=True)
    var = jnp.mean(jnp.square(y - mu), axis=-1, keepdims=True)
    yn = (y - mu) * lax.rsqrt(var + LNX_EPS)
    o_ref[...] = (yn * lw_ref[...] + lb_ref[...] + bon_ref[...]) * g_ref[...]


def _wkv_step(state_wkv, layer, r, ld, k, v, kn, b, g, bon, lnx_w, lnx_b):
    n = r.shape[0]
    tb = 8
    heads = lambda x: x.reshape(n, N_HEADS_A, 1, HEAD_DIM)
    row = pl.BlockSpec((tb, N_HEADS_A, 1, HEAD_DIM), lambda i: (i, 0, 0, 0))
    vec = pl.BlockSpec((1, N_HEADS_A, 1, HEAD_DIM), lambda i: (0, 0, 0, 0))
    st_in = pl.BlockSpec((1, tb, N_HEADS_A, HEAD_DIM, HEAD_DIM), lambda i: (layer, i, 0, 0, 0))
    st_out = pl.BlockSpec((1, tb, N_HEADS_A, HEAD_DIM, HEAD_DIM), lambda i: (0, i, 0, 0, 0))
    out, s1 = pl.pallas_call(
        _wkv_step_body,
        grid=(n // tb,),
        in_specs=[st_in] + [row] * 8 + [vec, vec],
        out_specs=[row, st_out],
        out_shape=[jax.ShapeDtypeStruct((n, N_HEADS_A, 1, HEAD_DIM), F32),
                   jax.ShapeDtypeStruct((1, n, N_HEADS_A, HEAD_DIM, HEAD_DIM), F32)],
        compiler_params=_params(("parallel",)),
        name="wkv_step",
    )(state_wkv, *[heads(x) for x in (r, ld, k, v, kn, b, g, bon)],
      lnx_w.reshape(1, N_HEADS_A, 1, HEAD_DIM), lnx_b.reshape(1, N_HEADS_A, 1, HEAD_DIM))
    return out.reshape(n, D_A), s1[0]


def _head_tile(x, h, lane):
    pair = x[:, (h // 2) * 128:(h // 2 + 1) * 128]
    if h % 2:
        pair = pltpu.roll(pair, HEAD_DIM, axis=1)
    return jnp.where(lane < HEAD_DIM, pair, 0.0)


def _qk_norm(x, gain, e):
    return x * lax.rsqrt(_dotx(x * x, e) * (1.0 / HEAD_DIM) + EPS) * gain


MOBA_DEN_LANE = HEAD_DIM
MOBA_MAX_BLOCKS = 128 - HEAD_DIM
MOBA_GROUP = 8
MOBA_QBLOCKS = 2


def _moba_prep_body(q_ref, k_ref, v_ref, qg_ref, kg_ref, e_ref, qa_o, kh_o, vh_o, kn_o, km_sc):
    n = pl.program_id(1)

    @pl.when(n == 0)
    def _():
        km_sc[...] = jnp.zeros_like(km_sc)

    e = e_ref[...]
    qn = _qk_norm(q_ref[0], qg_ref[...], e)
    kn = _qk_norm(k_ref[0], kg_ref[...], e)
    v = v_ref[0]
    kn_o[0] = kn
    tq = qn.shape[0]
    lane = lax.broadcasted_iota(jnp.int32, (tq, 128), 1)
    lane_km = lax.broadcasted_iota(jnp.int32, (MOBA_MAX_BLOCKS, 128), 1)
    blk = lax.broadcasted_iota(jnp.int32, (MOBA_MAX_BLOCKS, tq), 0)
    valid = blk < n
    key_marks = jnp.where(lane == HEAD_DIM + n, 1.0, 0.0)
    val_marks = jnp.where(lane == MOBA_DEN_LANE, 1.0, 0.0)
    km = km_sc[...]
    for h in range(N_HEADS_B):
        qt = _head_tile(qn, h, lane)
        gate = jnp.where(valid, _dot3(_head_tile(km, h, lane_km), qt, _NT), -jnp.inf)
        keep = blk == n
        for _ in range(MOBA_TOPK):
            top = jnp.max(gate, axis=0, keepdims=True)
            first = jnp.min(jnp.where(gate == top, blk, 1 << 20), axis=0, keepdims=True)
            pick = blk == first
            keep = jnp.logical_or(keep, jnp.logical_and(pick, valid))
            gate = jnp.where(pick, -jnp.inf, gate)
        pen = jnp.where(keep, 0.0, NEG_BIG)
        pen = jnp.concatenate([jnp.zeros((HEAD_DIM, tq), F32), pen], axis=0).T
        qa_o[0, h] = jnp.where(lane < HEAD_DIM, qt * (HEAD_DIM ** -0.5 * LOG2_E), pen).astype(BF16)
        kh_o[0, h] = (_head_tile(kn, h, lane) + key_marks).astype(BF16)
        vh_o[0, h] = (_head_tile(v, h, lane) + val_marks).astype(BF16)
    km_sc[pl.ds(n, 1), :] = jnp.mean(kn, axis=0, keepdims=True)


def _moba_prep(q, k, v, q_gain, k_gain):
    nb, t, _ = q.shape
    nblk = t // MOBA_BLOCK
    assert nblk <= MOBA_MAX_BLOCKS and nblk % MOBA_GROUP == 0 and nblk % MOBA_QBLOCKS == 0
    tile = pl.BlockSpec((1, MOBA_BLOCK, D_B), lambda b, i: (b, i, 0))
    vec = pl.BlockSpec((1, D_B), lambda b, i: (0, 0))
    hm = pl.BlockSpec((1, N_HEADS_B, MOBA_BLOCK, 128), lambda b, i: (b, 0, i, 0))
    return pl.pallas_call(
        _moba_prep_body,
        grid=(nb, nblk),
        in_specs=[tile, tile, tile, vec, vec, pl.BlockSpec((D_B, D_B), lambda b, i: (0, 0))],
        out_specs=[hm, hm, hm, tile],
        out_shape=[jax.ShapeDtypeStruct((nb, N_HEADS_B, t, 128), BF16)] * 3
                  + [jax.ShapeDtypeStruct((nb, t, D_B), F32)],
        scratch_shapes=[pltpu.VMEM((MOBA_MAX_BLOCKS, D_B), F32)],
        compiler_params=_params(("parallel", "arbitrary")),
        name="moba_prep",
    )(q, k, v, jnp.tile(q_gain, N_HEADS_B).reshape(1, D_B), jnp.tile(k_gain, N_HEADS_B).reshape(1, D_B),
      _head_ones(D_B))


def _moba_attn_body(q_ref, k_ref, v_ref, tab_ref, o_ref, s_sc, mx_sc, acc_sc):
    first = pl.program_id(2) * MOBA_QBLOCKS
    n_groups = (first + MOBA_QBLOCKS - 1 + MOBA_GROUP) // MOBA_GROUP
    span = MOBA_GROUP * MOBA_BLOCK

    def key_rows(g):
        return pl.ds(pl.multiple_of(g * span, span), span)

    def q_rows(r):
        return slice(r * MOBA_BLOCK, (r + 1) * MOBA_BLOCK)

    mx_sc[...] = jnp.full_like(mx_sc, -jnp.inf)

    def pass1(g, carry):
        kg = k_ref[0, 0, key_rows(g), :]
        for r in range(MOBA_QBLOCKS):
            s = _mm(q_ref[0, 0, q_rows(r), :], kg, _NT)
            m = mx_sc[r]
            for i in range(MOBA_GROUP):
                cols = slice(i * MOBA_BLOCK, (i + 1) * MOBA_BLOCK)
                offset = first + r - (g * MOBA_GROUP + i)
                si = s[:, cols] + tab_ref[0, jnp.clip(offset, 0, N_BIAS_TILES - 1)]
                s_sc[g, q_rows(r), cols] = si
                m = jnp.maximum(m, si)
            mx_sc[r] = m
        return carry

    lax.fori_loop(0, n_groups, pass1, 0)
    m = [jnp.max(mx_sc[r], axis=1, keepdims=True) for r in range(MOBA_QBLOCKS)]
    acc_sc[...] = jnp.zeros_like(acc_sc)

    def pass2(g, carry):
        vg = v_ref[0, 0, key_rows(g), :]
        for r in range(MOBA_QBLOCKS):
            p = jnp.exp2(s_sc[g, q_rows(r), :] - m[r]).astype(BF16)
            acc_sc[q_rows(r), :] += _mm(p, vg)
        return carry

    lax.fori_loop(0, n_groups, pass2, 0)
    acc = acc_sc[...]
    o_ref[0, 0] = acc / acc[:, MOBA_DEN_LANE:MOBA_DEN_LANE + 1]


def _moba_attn(qa, kh, vh, bias_tiles):
    nb, nh, t, _ = qa.shape
    tq = MOBA_QBLOCKS * MOBA_BLOCK
    n_groups_max = t // (MOBA_GROUP * MOBA_BLOCK)
    return pl.pallas_call(
        _moba_attn_body,
        grid=(nh, nb, t // tq),
        in_specs=[pl.BlockSpec((1, 1, tq, 128), lambda h, b, a: (b, h, a, 0)),
                  pl.BlockSpec((1, 1, t, 128), lambda h, b, a: (b, h, 0, 0)),
                  pl.BlockSpec((1, 1, t, 128), lambda h, b, a: (b, h, 0, 0)),
                  pl.BlockSpec((1, N_BIAS_TILES, MOBA_BLOCK, MOBA_BLOCK), lambda h, b, a: (h, 0, 0, 0))],
        out_specs=pl.BlockSpec((1, 1, tq, 128), lambda h, b, a: (b, h, a, 0)),
        out_shape=jax.ShapeDtypeStruct((nb, nh, t, 128), F32),
        scratch_shapes=[pltpu.VMEM((n_groups_max, tq, MOBA_GROUP * MOBA_BLOCK), F32),
                        pltpu.VMEM((MOBA_QBLOCKS, MOBA_BLOCK, MOBA_BLOCK), F32),
                        pltpu.VMEM((tq, 128), F32)],
        compiler_params=_params(("parallel", "parallel", "arbitrary")),
        name="moba_attn",
    )(qa, kh, vh, bias_tiles)


def _qk_norm_rows_body(q_ref, k_ref, qg_ref, kg_ref, e_ref, qn_o, kn_o):
    e = e_ref[...]
    qn_o[...] = _qk_norm(q_ref[...], qg_ref[...], e)
    kn_o[...] = _qk_norm(k_ref[...], kg_ref[...], e)


def _qk_norm_rows(q, k, q_gain, k_gain):
    n = q.shape[0]
    return pl.pallas_call(
        _qk_norm_rows_body,
        out_shape=[jax.ShapeDtypeStruct((n, D_B), F32)] * 2,
        name="qk_norm_rows",
    )(q, k, jnp.tile(q_gain, N_HEADS_B).reshape(1, D_B), jnp.tile(k_gain, N_HEADS_B).reshape(1, D_B),
      _head_ones(D_B))


def _moba_paged_body(n_pages, pt_ref, qb_ref, knb_ref, vnb_ref, tab_ref, *rest):
    k_pages = rest[:n_pages]
    v_pages = rest[n_pages:2 * n_pages]
    o_ref = rest[2 * n_pages]
    page_size = qb_ref.shape[-1]
    per_blk = MOBA_BLOCK // page_size
    n_blk = n_pages // per_blk
    qb = qb_ref[0]
    scale = HEAD_DIM ** -0.5
    raw = [jnp.sum(k_pages[p][0, 0] * qb, axis=1, keepdims=True) for p in range(n_pages)]

    gates = []
    for n in range(n_blk):
        tot = raw[n * per_blk]
        for i in range(1, per_blk):
            tot = tot + raw[n * per_blk + i]
        gates.append(jnp.sum(tot, axis=2, keepdims=True) * (1.0 / MOBA_BLOCK))
    keep = [jnp.zeros(gates[0].shape, jnp.bool_)] * n_blk
    for _ in range(min(MOBA_TOPK, n_blk + 1)):
        top = functools.reduce(jnp.maximum, gates)
        found = jnp.zeros(top.shape, jnp.bool_)
        for n in range(n_blk):
            hit = jnp.logical_and(gates[n] == top, jnp.logical_not(found))
            keep[n] = jnp.logical_or(keep[n], hit)
            found = jnp.logical_or(found, hit)
            gates[n] = jnp.where(hit, -jnp.inf, gates[n])

    logits = [raw[p] * scale + tab_ref[p] + jnp.where(keep[p // per_blk], 0.0, NEG_BIG)
              for p in range(n_pages)]
    own = jnp.sum(knb_ref[0] * qb, axis=1, keepdims=True) * scale + tab_ref[n_pages]
    m = jnp.max(functools.reduce(jnp.maximum, logits + [own]), axis=2, keepdims=True)
    lane = lax.broadcasted_iota(jnp.int32, own.shape, 2)
    p_own = jnp.where(lane == 0, jnp.exp(own - m), 0.0)
    probs = [jnp.exp(s - m) for s in logits]
    inv_l = 1.0 / jnp.sum(functools.reduce(jnp.add, probs + [p_own]), axis=2, keepdims=True)
    acc = (p_own * inv_l) * vnb_ref[0]
    for p in range(n_pages):
        acc = acc + (probs[p] * inv_l) * v_pages[p][0, 0]
    h1, h2, h3 = _split3(acc.reshape(N_HEADS_B * HEAD_DIM, page_size))
    ones = jnp.ones((8, page_size), BF16)
    o_ref[0] = (_mm(ones, h1, _NT) + (_mm(ones, h2, _NT) + _mm(ones, h3, _NT)))[0:1]


def _moba_paged(qn, kn, v, cache_kt, cache_vt, page_table, layer, bias_pages):
    n, n_pages = page_table.shape
    page_size = cache_kt.shape[-1]
    tile = (N_HEADS_B, HEAD_DIM, page_size)
    lanes = lambda x: jnp.broadcast_to(x.reshape(n, N_HEADS_B, HEAD_DIM, 1), (n,) + tile)
    row = pl.BlockSpec((1,) + tile, lambda b, pt: (b, 0, 0, 0))
    page = lambda p: pl.BlockSpec((1, 1) + tile, lambda b, pt: (layer, pt[b, p], 0, 0, 0))
    out = pl.pallas_call(
        functools.partial(_moba_paged_body, n_pages),
        grid_spec=pltpu.PrefetchScalarGridSpec(
            num_scalar_prefetch=1,
            grid=(n,),
            in_specs=[row, row, row, pl.BlockSpec(bias_pages.shape, lambda b, pt: (0, 0, 0, 0))]
                     + [page(p) for p in range(n_pages)] * 2,
            out_specs=pl.BlockSpec((1, 1, D_B), lambda b, pt: (b, 0, 0))),
        out_shape=jax.ShapeDtypeStruct((n, 1, D_B), F32),
        compiler_params=_params(("arbitrary",)),
        name="moba_paged",
    )(page_table, lanes(qn), lanes(kn), lanes(v), bias_pages, *([cache_kt] * n_pages), *([cache_vt] * n_pages))
    return out.reshape(n, D_B)


def _gmlp_front(uv, vn_gain, e):
    ge = 0.5 * uv * (1.0 + lax.erf(uv * math.sqrt(0.5)))
    u = ge[:, :D_C]
    v = ge[:, D_C:]
    vg = v * lax.rsqrt(_dotx(v * v, e) * (1.0 / HEAD_DIM) + EPS) * vn_gain
    return u, vg


def _rms(x, gain):
    return x * lax.rsqrt(jnp.mean(x * x, axis=-1, keepdims=True) + EPS) * gain


def _gmlp_body(uv_ref, vn_ref, ws_ref, bs_ref, on_ref, e_ref, o_ref):
    u, vg = _gmlp_front(uv_ref[0], vn_ref[...], e_ref[...])
    tm = u.shape[0]
    ri = lax.broadcasted_iota(jnp.int32, (CHUNK_C, CHUNK_C), 0)
    ci = lax.broadcasted_iota(jnp.int32, (CHUNK_C, CHUNK_C), 1)
    group = lax.broadcasted_iota(jnp.int32, (CHUNK_C, D_C), 1) // HEAD_DIM
    ws = [jnp.where(ri >= ci, ws_ref[g], 0.0).astype(BF16) for g in range(N_GROUPS_C)]
    for c in range(tm // CHUNK_C):
        rows = slice(c * CHUNK_C, (c + 1) * CHUNK_C)
        vc = vg[rows].astype(BF16)
        mixed = bs_ref[...]
        for g in range(N_GROUPS_C):
            mixed = mixed + jnp.where(group == g, _mm(ws[g], vc), 0.0)
        o_ref[0, rows, :] = _rms(u[rows] * mixed, on_ref[...])


def _gmlp(uv, p):
    nb, t, _ = uv.shape
    tm = 512
    const = lambda shape: pl.BlockSpec(shape, lambda b, i: (0,) * len(shape))
    bs = jnp.repeat(p['b_s'].T, HEAD_DIM, axis=1)
    return pl.pallas_call(
        _gmlp_body,
        grid=(nb, t // tm),
        in_specs=[pl.BlockSpec((1, tm, 2 * D_C), lambda b, i: (b, i, 0)), const((1, D_C)),
                  const((N_GROUPS_C, CHUNK_C, CHUNK_C)), const((CHUNK_C, D_C)), const((1, D_C)),
                  const((D_C, D_C))],
        out_specs=pl.BlockSpec((1, tm, D_C), lambda b, i: (b, i, 0)),
        out_shape=jax.ShapeDtypeStruct((nb, t, D_C), F32),
        compiler_params=_params(("parallel", "parallel")),
        name="gmlp",
    )(uv, p['v_norm'].reshape(1, D_C), p['w_s'], bs, p['out_norm_c'].reshape(1, D_C), _head_ones(D_C))


def _gmlp_first_pos_body(uv_ref, vn_ref, w00_ref, b00_ref, on_ref, e_ref, o_ref, vg_ref):
    u, vg = _gmlp_front(uv_ref[...], vn_ref[...], e_ref[...])
    vg_ref[...] = vg
    o_ref[...] = _rms(u * (vg * w00_ref[...] + b00_ref[...]), on_ref[...])


def _gmlp_first_pos(uv, p):
    n = uv.shape[0]
    w00 = jnp.repeat(p['w_s'][:, 0, 0], HEAD_DIM).reshape(1, D_C)
    b00 = jnp.repeat(p['b_s'][:, 0], HEAD_DIM).reshape(1, D_C)
    return pl.pallas_call(
        _gmlp_first_pos_body,
        out_shape=[jax.ShapeDtypeStruct((n, D_C), F32)] * 2,
        name="gmlp_first_pos",
    )(uv, p['v_norm'].reshape(1, D_C), w00, b00, p['out_norm_c'].reshape(1, D_C), _head_ones(D_C))


def _mix_ffn_body(head_major, x_ref, oa_ref, ob_ref, oc_ref, mod_ref, nb_ref, nf_ref, wo_ref, wu_ref, wd_ref,
                  o_ref, x1_sc, h2_sc, acc_sc):
    j = pl.program_id(2)
    d = D_MODEL

    @pl.when(j == 0)
    def _():
        if head_major:
            lane = lax.broadcasted_iota(jnp.int32, ob_ref.shape[2:], 1)
            ob = jnp.concatenate(
                [jnp.where(lane < HEAD_DIM, ob_ref[0, h], pltpu.roll(ob_ref[0, h + 1], HEAD_DIM, axis=1))
                 for h in range(0, N_HEADS_B, 2)], axis=1)
        else:
            ob = ob_ref[0]
        cat = jnp.concatenate([oa_ref[0], _rms(ob, nb_ref[...]), oc_ref[0]], axis=1)
        x1 = x_ref[0] + mod_ref[0, :, 2 * d:3 * d] * _dot1(cat, wo_ref[0])
        x1_sc[...] = x1
        h2 = _rms(x1, nf_ref[...]) * (1.0 + mod_ref[0, :, 4 * d:5 * d]) + mod_ref[0, :, 3 * d:4 * d]
        h2_sc[...] = h2.astype(BF16)
        acc_sc[...] = jnp.zeros_like(acc_sc)

    up = _mm(h2_sc[...], wu_ref[0])
    acc_sc[...] += _mm(jnp.square(jnp.maximum(up, 0.0)).astype(BF16), wd_ref[0])

    @pl.when(j == pl.num_programs(2) - 1)
    def _():
        o_ref[0] = x1_sc[...] + mod_ref[0, :, 5 * d:6 * d] * acc_sc[...]


def _mix_ffn(x, oa, ob, oc, mod, p, w_out_bf, w_up_bf, w_down_bf, layer, per_row, head_major):
    nb, t, d = x.shape
    tm = min(t, 512)
    tf = 1024
    mod_rows = tm if per_row else 1
    tile = lambda width: pl.BlockSpec((1, tm, width), lambda b, i, j: (b, i, 0))
    const = lambda shape: pl.BlockSpec(shape, lambda b, i, j: (0,) * len(shape))
    if head_major:
        ob_spec = pl.BlockSpec((1, N_HEADS_B, tm, 128), lambda b, i, j: (b, 0, i, 0))
    else:
        ob_spec = tile(D_B)
    return pl.pallas_call(
        functools.partial(_mix_ffn_body, head_major),
        grid=(nb, t // tm, D_FF // tf),
        in_specs=[tile(d), tile(D_A), ob_spec, tile(D_C),
                  pl.BlockSpec((1, mod_rows, 6 * d), lambda b, i, j: (b, i if per_row else 0, 0)),
                  const((1, D_B)), const((1, d)),
                  pl.BlockSpec((1, d, d), lambda b, i, j: (layer, 0, 0)),
                  pl.BlockSpec((1, d, tf), lambda b, i, j: (layer, 0, j)),
                  pl.BlockSpec((1, tf, d), lambda b, i, j: (layer, j, 0))],
        out_specs=tile(d),
        out_shape=jax.ShapeDtypeStruct((nb, t, d), F32),
        scratch_shapes=[pltpu.VMEM((tm, d), F32), pltpu.VMEM((tm, d), BF16), pltpu.VMEM((tm, d), F32)],
        compiler_params=_params(("parallel", "parallel", "arbitrary")),
        name="mix_ffn",
    )(x, oa, ob, oc, mod, p['out_norm_b'].reshape(1, D_B), p['norm_ffn'].reshape(1, d),
      w_out_bf, w_up_bf, w_down_bf)


def _prompt_layer(x, mod, p, big, layer, bias_tiles):
    nb, t, _ = x.shape
    z, q, k, v, uv = _inproj(x, mod, p['norm_mix'], big['w_in'], layer, per_row=False)
    prep = _rwkv_prep(z, jnp.zeros((nb, 1, W_SHIFT), F32), p, per_row=False)
    shift_new = prep[7].reshape(nb, W_SHIFT)
    out_a, wkv_new = _wkv_scan(*prep[:7], p['lnx_w'], p['lnx_b'])
    wkv_new = wkv_new.reshape(nb, N_HEADS_A, HEAD_DIM, HEAD_DIM)
    qa, kh, vh, k_new = _moba_prep(q, k, v, p['q_norm'], p['k_norm'])
    out_b = _moba_attn(qa, kh, vh, bias_tiles)
    out_c = _gmlp(uv, p)
    x = _mix_ffn(x, out_a, out_b, out_c, mod, p, big['w_out'], big['w_up'], big['w_down'], layer,
                 per_row=False, head_major=True)
    return x, k_new, v, wkv_new, shift_new


def _sample_layer(x, mod, p, big, layer, bias_pages, cache_kt, cache_vt, page_table, state_wkv, shift0):
    n = x.shape[1]
    z, q, k, v, uv = _inproj(x, mod, p['norm_mix'], big['w_in'], layer, per_row=True)
    prep = _rwkv_prep(z, shift0.reshape(1, n, W_SHIFT), p, per_row=True)
    out_a, wkv_new = _wkv_step(state_wkv, layer, *[a.reshape(n, D_A) for a in prep], p['lnx_w'], p['lnx_b'])
    qn, kn = _qk_norm_rows(q.reshape(n, D_B), k.reshape(n, D_B), p['q_norm'], p['k_norm'])
    v = v.reshape(n, D_B)
    out_b = _moba_paged(qn, kn, v, cache_kt, cache_vt, page_table, layer, bias_pages)
    out_c, vg = _gmlp_first_pos(uv.reshape(n, 2 * D_C), p)
    x = _mix_ffn(x, out_a.reshape(1, n, D_A), out_b.reshape(1, n, D_B), out_c.reshape(1, n, D_C), mod, p,
                 big['w_out'], big['w_up'], big['w_down'], layer, per_row=True, head_major=False)
    return x, kn, v, wkv_new, z.reshape(n, W_SHIFT), vg


def kernel(x_prompt, x_sample, c_prompt, c_sample, cache_k, cache_v, page_table, state_wkv, state_shift, norm_mix, w_ada, b_ada, w_in, mu_shift, w0, w2, a0, a2, g2, k_k, k_a, r_k, lnx_w, lnx_b, q_norm, k_norm, rel_bias, out_norm_b, v_norm, w_s, b_s, out_norm_c, w_out, norm_ffn, w_up, w_down):
    layer_w = {
        'norm_mix': norm_mix, 'mu_shift': mu_shift, 'w0': w0, 'w2': w2, 'a0': a0, 'a2': a2, 'g2': g2,
        'k_k': k_k, 'k_a': k_a, 'r_k': r_k, 'lnx_w': lnx_w, 'lnx_b': lnx_b, 'q_norm': q_norm,
        'k_norm': k_norm, 'out_norm_b': out_norm_b, 'v_norm': v_norm, 'w_s': w_s, 'b_s': b_s,
        'out_norm_c': out_norm_c, 'norm_ffn': norm_ffn,
    }
    depth = w_in.shape[0]
    n_prompt, seq, d = x_prompt.shape
    n_dec = x_sample.shape[0]
    past_len = page_table.shape[1] * cache_k.shape[2]
    big = {'w_in': w_in.astype(BF16), 'w_out': w_out.astype(BF16),
           'w_up': w_up.astype(BF16), 'w_down': w_down.astype(BF16)}
    mod = _ada(jnp.concatenate([c_sample, c_prompt], axis=0), w_ada, b_ada)
    bias_tiles = _bias_tiles(rel_bias)
    bias_pages = _bias_pages(rel_bias, past_len, cache_k.shape[2])
    cache_kt = cache_k.transpose(0, 1, 3, 4, 2)
    cache_vt = cache_v.transpose(0, 1, 3, 4, 2)

    xp = x_prompt
    xs = x_sample.reshape(1, n_dec, d)
    outs = [[] for _ in range(9)]
    for l in range(depth):
        p = {name: arr[l] for name, arr in layer_w.items()}
        mod_s = mod[l, :n_dec].reshape(1, n_dec, 6 * d)
        mod_p = mod[l, n_dec:].reshape(n_prompt, 1, 6 * d)
        xp, k_p, v_p, wkv_p, sh_p = _prompt_layer(xp, mod_p, p, big, l, bias_tiles)
        xs, k_s, v_s, wkv_s, sh_s, vg_s = _sample_layer(
            xs, mod_s, p, big, l, bias_pages, cache_kt, cache_vt, page_table, state_wkv, state_shift[l])
        for lst, val in zip(outs, (k_p, v_p, k_s, v_s, wkv_p, wkv_s, sh_p, sh_s, vg_s)):
            lst.append(val)
    kp, vp, ks, vs, wp, ws, sp, ss, gs = (jnp.stack(o) for o in outs)
    heads = lambda a, rows: a.reshape(depth, rows, -1, N_HEADS_B, HEAD_DIM)
    return (xp, xs.reshape(n_dec, 1, d),
            heads(kp, n_prompt), heads(vp, n_prompt), heads(ks, n_dec), heads(vs, n_dec),
            wp, ws, sp, ss, gs.reshape(depth, n_dec, 1, D_C))
```

```python
import functools
import math

import numpy as np
import jax
import jax.numpy as jnp
from jax import lax
from jax.experimental import pallas as pl
from jax.experimental.pallas import tpu as pltpu

F32 = jnp.float32
BF16 = jnp.bfloat16

D_MODEL = 1024
HEAD_DIM = 64
N_HEADS_A = 6
N_HEADS_B = 6
N_GROUPS_C = 4
D_A = N_HEADS_A * HEAD_DIM
D_B = N_HEADS_B * HEAD_DIM
D_C = N_GROUPS_C * HEAD_DIM
LORA_W = 64
LORA_A = 64
LORA_G = 128
W_SHIFT = 3 * D_A + LORA_W + LORA_A + LORA_G
N_IN = W_SHIFT + 3 * D_B + 2 * D_C
IN_SEGS = (0, W_SHIFT, W_SHIFT + D_B, W_SHIFT + 2 * D_B, W_SHIFT + 3 * D_B, N_IN)
MOBA_BLOCK = 256
MOBA_TOPK = 3
CHUNK_C = 128
N_BUCKETS = 32
REL_MAX_DIST = 4096
D_FF = 4 * D_MODEL
EPS = 1e-6
LNX_EPS = 64e-5
WKV_CHUNK = 64
NEG_BIG = -1e30
LOG2_E = math.log2(math.e)
V7X_VMEM_LIMIT = 56 * 1024 * 1024


def _bucket_saturation_tiles():
    max_exact = N_BUCKETS // 2
    n = np.arange(1, 2 * REL_MAX_DIST, dtype=np.float64)
    large = max_exact + (np.log(n / max_exact) / math.log(REL_MAX_DIST / max_exact)
                         * (N_BUCKETS - max_exact)).astype(np.int64)
    first_sat = int(n[np.argmax(np.minimum(large, N_BUCKETS - 1) == N_BUCKETS - 1)])
    return -(-(first_sat + MOBA_BLOCK) // MOBA_BLOCK) + 1


N_BIAS_TILES = _bucket_saturation_tiles() + 1

_NN = (((1,), (0,)), ((), ()))
_NT = (((1,), (1,)), ((), ()))
_TN = (((0,), (0,)), ((), ()))


def _mm(a, b, dims=_NN):
    return lax.dot_general(a, b, dims, preferred_element_type=F32)


def _split2(x):
    hi = x.astype(BF16)
    lo = (x - hi.astype(F32)).astype(BF16)
    return hi, lo


def _split3(x):
    h1 = x.astype(BF16)
    r1 = x - h1.astype(F32)
    h2 = r1.astype(BF16)
    h3 = (r1 - h2.astype(F32)).astype(BF16)
    return h1, h2, h3


def _dot1(a, b, dims=_NN):
    return _mm(a.astype(BF16), b.astype(BF16), dims)


def _dot3(a, b, dims=_NN):
    ah, al = _split2(a)
    bh, bl = _split2(b)
    return _mm(ah, bh, dims) + (_mm(ah, bl, dims) + _mm(al, bh, dims))


def _dotx(a, e, dims=_NN):
    ah, al = _split2(a)
    return _mm(ah, e, dims) + _mm(al, e, dims)


def _dotx3(a, e, dims=_NN):
    h1, h2, h3 = _split3(a)
    return _mm(h1, e, dims) + (_mm(h2, e, dims) + _mm(h3, e, dims))


def _params(sem):
    return pltpu.CompilerParams(dimension_semantics=sem, vmem_limit_bytes=V7X_VMEM_LIMIT)


def _sigmoid(x):
    return jax.nn.sigmoid(x)


def _head_ones(width):
    i = np.arange(width) // HEAD_DIM
    return jnp.asarray(i[:, None] == i[None, :], dtype=BF16)


def _ada_body(c_ref, w_ref, b_ref, o_ref):
    c = c_ref[...]
    o_ref[0] = _dot3(c * _sigmoid(c), w_ref[0]) + b_ref[0]


def _ada(c_all, w_ada, b_ada):
    depth, d, n = w_ada.shape
    m = c_all.shape[0]
    tn = 1536
    return pl.pallas_call(
        _ada_body,
        grid=(depth, n // tn),
        in_specs=[pl.BlockSpec((m, d), lambda l, j: (0, 0)),
                  pl.BlockSpec((1, d, tn), lambda l, j: (l, 0, j)),
                  pl.BlockSpec((1, 1, tn), lambda l, j: (l, 0, j))],
        out_specs=pl.BlockSpec((1, m, tn), lambda l, j: (l, 0, j)),
        out_shape=jax.ShapeDtypeStruct((depth, m, n), F32),
        compiler_params=_params(("parallel", "parallel")),
        name="ada_mod",
    )(c_all, w_ada, b_ada.reshape(depth, 1, n))


def _rel_bucket(dist):
    n = jnp.maximum(dist, 0)
    max_exact = N_BUCKETS // 2
    nf = jnp.maximum(n, 1).astype(F32)
    large = max_exact + (jnp.log(nf / max_exact) / math.log(REL_MAX_DIST / max_exact)
                         * (N_BUCKETS - max_exact)).astype(jnp.int32)
    large = jnp.minimum(large, N_BUCKETS - 1)
    return jnp.where(n < max_exact, n, large)


def _bias_tiles_body(rb_ref, o_ref):
    h = pl.program_id(0)
    d = pl.program_id(1)
    ti = lax.broadcasted_iota(jnp.int32, (MOBA_BLOCK, MOBA_BLOCK), 0)
    tj = lax.broadcasted_iota(jnp.int32, (MOBA_BLOCK, MOBA_BLOCK), 1)
    dist = d * MOBA_BLOCK + ti - tj
    bucket = _rel_bucket(dist)
    acc = jnp.zeros((MOBA_BLOCK, MOBA_BLOCK), F32)
    for b in range(N_BUCKETS):
        acc = jnp.where(bucket == b, rb_ref[h * N_BUCKETS + b], acc)
    o_ref[0, 0] = jnp.where(dist >= 0, acc * LOG2_E, NEG_BIG)


def _bias_tiles(rel_bias):
    rb = rel_bias.T.reshape(-1)
    return pl.pallas_call(
        _bias_tiles_body,
        grid=(N_HEADS_B, N_BIAS_TILES),
        in_specs=[pl.BlockSpec(memory_space=pltpu.SMEM)],
        out_specs=pl.BlockSpec((1, 1, MOBA_BLOCK, MOBA_BLOCK), lambda h, d: (h, d, 0, 0)),
        out_shape=jax.ShapeDtypeStruct((N_HEADS_B, N_BIAS_TILES, MOBA_BLOCK, MOBA_BLOCK), F32),
        compiler_params=_params(("parallel", "parallel")),
        name="bias_tiles",
    )(rb)


def _bias_pages_body(past_len, rb_ref, o_ref):
    page_size = o_ref.shape[-1]
    pos = pl.program_id(0) * page_size + lax.broadcasted_iota(jnp.int32, o_ref.shape[1:], 2)
    bucket = _rel_bucket(past_len - pos)
    acc = jnp.zeros(o_ref.shape[1:], F32)
    for b in range(N_BUCKETS):
        acc = jnp.where(bucket == b, rb_ref[b], acc)
    o_ref[0] = acc


def _bias_pages(rel_bias, past_len, page_size):
    n_pages = past_len // page_size + 1
    rb = jnp.broadcast_to(rel_bias[:, :, None, None], (N_BUCKETS, N_HEADS_B, 1, page_size))
    return pl.pallas_call(
        functools.partial(_bias_pages_body, past_len),
        grid=(n_pages,),
        in_specs=[pl.BlockSpec((N_BUCKETS, N_HEADS_B, 1, page_size), lambda p: (0, 0, 0, 0))],
        out_specs=pl.BlockSpec((1, N_HEADS_B, 1, page_size), lambda p: (p, 0, 0, 0)),
        out_shape=jax.ShapeDtypeStruct((n_pages, N_HEADS_B, 1, page_size), F32),
        compiler_params=_params(("parallel",)),
        name="bias_pages",
    )(rb)


def _inproj_body(x_ref, mod_ref, gain_ref, w_ref, *out_refs):
    x = x_ref[0]
    d = x.shape[-1]
    y = x * lax.rsqrt(jnp.mean(x * x, axis=-1, keepdims=True) + EPS) * gain_ref[...]
    sh = mod_ref[0, :, 0:d]
    sc = mod_ref[0, :, d:2 * d]
    h = (y * (1.0 + sc) + sh).astype(BF16)
    for ref, a, b in zip(out_refs, IN_SEGS[:-1], IN_SEGS[1:]):
        ref[0] = _mm(h, w_ref[0, :, a:b])


def _inproj(x, mod, gain, w_in_bf, layer, per_row):
    nb, t, d = x.shape
    tm = min(t, 512)
    mod_rows = tm if per_row else 1
    widths = [b - a for a, b in zip(IN_SEGS[:-1], IN_SEGS[1:])]
    return pl.pallas_call(
        _inproj_body,
        grid=(nb, t // tm),
        in_specs=[pl.BlockSpec((1, tm, d), lambda b, i: (b, i, 0)),
                  pl.BlockSpec((1, mod_rows, 6 * d), lambda b, i: (b, i if per_row else 0, 0)),
                  pl.BlockSpec((1, d), lambda b, i: (0, 0)),
                  pl.BlockSpec((1, d, N_IN), lambda b, i: (layer, 0, 0))],
        out_specs=[pl.BlockSpec((1, tm, w), lambda b, i: (b, i, 0)) for w in widths],
        out_shape=[jax.ShapeDtypeStruct((nb, t, w), F32) for w in widths],
        compiler_params=_params(("parallel", "parallel")),
        name="in_proj",
    )(x, mod, gain.reshape(1, d), w_in_bf)


_BNN = (((2,), (1,)), ((0,), (0,)))
_BNT = (((2,), (2,)), ((0,), (0,)))
_BTN = (((1,), (1,)), ((0,), (0,)))


def _split_heads(x):
    return jnp.stack([x[:, h * HEAD_DIM:(h + 1) * HEAD_DIM] for h in range(x.shape[1] // HEAD_DIM)])


def _wkv_local(kt, rt, kh, bh, kb, bb, vv, tri_s, tri_i, eye):
    c = kt.shape[1]
    aa = _dot1(jnp.concatenate([kt, rt], axis=1), jnp.concatenate([kh, bh], axis=1), _BNT)
    a_kk = jnp.where(tri_s, aa[:, :c, :c], 0.0)
    a_kb = jnp.where(tri_s, aa[:, :c, c:], 0.0)
    a_rk = jnp.where(tri_i, aa[:, c:, :c], 0.0)
    a_rb = jnp.where(tri_i, aa[:, c:, c:], 0.0)
    x = -a_kb
    inv = eye + x
    span = 2
    while span < c:
        x = _dot1(x, x, _BNN)
        inv = inv + _dot1(inv, x, _BNN)
        span *= 2
    av = _dot1(jnp.concatenate([a_kk, a_rk], axis=1), vv, _BNN)
    pkw = _dot1(inv, jnp.concatenate([kt, av[:, :c]], axis=2), _BNN)
    arb = _dot1(a_rb, pkw, _BNN)
    mb = _dot1(pkw, bb, _BTN)
    rk = rt - arb[:, :, :HEAD_DIM]
    y0 = av[:, c:] - arb[:, :, HEAD_DIM:]
    sv = _dot1(vv, kb, _BTN) - mb[:, HEAD_DIM:]
    return mb[:, :HEAD_DIM], sv, rk, y0


def _rwkv_prep_body(per_row, z_ref, prev_ref, mu_ref, w0_ref, a0_ref, kk_ref, ka_ref, rk_ref,
                    w2a_ref, g2_ref, e_ref, *rest):
    z = z_ref[0]
    tm = z.shape[0]
    if per_row:
        zp = prev_ref[0]
    else:
        m2_o, sv_o, rkc_o, y0_o, gam_o, g_o, bon_o, sh_o, carry = rest
        @pl.when(pl.program_id(1) == 0)
        def _():
            carry[...] = prev_ref[0]
        row = lax.broadcasted_iota(jnp.int32, z.shape, 0)
        zp = jnp.where(row == 0, carry[...], pltpu.roll(z, 1, axis=0))
        carry[...] = z[tm - 1:tm, :]
        sh_o[0] = z[tm - 1:tm, :]
    zs = z + mu_ref[...] * (zp - z)
    r = zs[:, 0:D_A]
    k = zs[:, D_A:2 * D_A]
    v = zs[:, 2 * D_A:3 * D_A]
    zwa = zs[:, 3 * D_A:3 * D_A + LORA_W + LORA_A]
    zg = zs[:, 3 * D_A + LORA_W + LORA_A:]
    lane = lax.broadcasted_iota(jnp.int32, zwa.shape, 1)
    lwa = _dot3(jnp.where(lane < LORA_W, jnp.tanh(zwa), zwa), w2a_ref[...])
    xw = -(w0_ref[...] + lwa[:, :D_A])
    softplus = jnp.maximum(xw, 0.0) + jnp.log1p(jnp.exp(-jnp.abs(xw)))
    w_log = -softplus - 0.5
    a = _sigmoid(a0_ref[...] + lwa[:, D_A:])
    g = _dot3(_sigmoid(zg), g2_ref[...])
    e = e_ref[...]
    kkr = k * kk_ref[...]
    kn = kkr / jnp.maximum(jnp.sqrt(_dotx(kkr * kkr, e)), 1e-12)
    k2 = k * (1.0 + (a - 1.0) * ka_ref[...])
    ld = -jnp.exp(w_log)
    b = kn * a
    bon = _dotx(r * k2 * rk_ref[...], e) * v
    if per_row:
        for ref, val in zip(rest, (r, ld, k2, v, kn, b, g, bon)):
            ref[0] = val
        return
    g_o[0] = g
    bon_o[0] = bon
    c = WKV_CHUNK
    ri = lax.broadcasted_iota(jnp.int32, (c, c), 0)
    ci = lax.broadcasted_iota(jnp.int32, (c, c), 1)
    tri_s = ri > ci
    tri_i = ri >= ci
    cum = tri_i.astype(BF16)
    eye = (ri == ci).astype(F32)
    nch = tm // c
    chunks = lambda x: x.reshape(nch, c, x.shape[-1])

    def chains(x):
        return jnp.stack([x[:, :, h * HEAD_DIM:(h + 1) * HEAD_DIM]
                          for h in range(N_HEADS_A)]).reshape(N_HEADS_A * nch, c, HEAD_DIM)

    ldc = chunks(ld)
    cums = jnp.broadcast_to(cum, (nch, c, c))
    l1, l2, l3 = _split3(ldc)
    gcum = _mm(cums, l1, _BNN) + (_mm(cums, l2, _BNN) + _mm(cums, l3, _BNN))
    e_in = jnp.exp(gcum)
    e_ng = jnp.exp(-gcum)
    gam = e_in[:, c - 1:c, :]
    kh = chunks(k2) * e_ng
    bh = chunks(b) * e_ng
    outs = _wkv_local(chains(chunks(kn) * jnp.exp(gcum - ldc)), chains(chunks(r) * e_in), chains(kh), chains(bh),
                      chains(kh * gam), chains(bh * gam), chains(chunks(v)), tri_s, tri_i, eye)
    for ref, val in zip((m2_o, sv_o, rkc_o, y0_o), outs):
        ref[0] = val.reshape(N_HEADS_A, nch, c, HEAD_DIM)
    gam_o[0] = gam


def _rwkv_prep(z, prev, p, per_row):
    nb, t, w = z.shape
    tm = min(t, 512)
    row = lambda x: x.reshape(1, -1)
    w2a = jnp.zeros((LORA_W + LORA_A, 2 * D_A), F32)
    w2a = w2a.at[:LORA_W, :D_A].set(p['w2']).at[LORA_W:, D_A:].set(p['a2'])
    const = lambda shape: pl.BlockSpec(shape, lambda b, i: (0,) * len(shape))
    tile = lambda width: pl.BlockSpec((1, tm, width), lambda b, i: (b, i, 0))
    if per_row:
        prev_spec = tile(w)
        out_specs = [tile(D_A)] * 8
        out_shape = [jax.ShapeDtypeStruct((nb, t, D_A), F32)] * 8
        scratch = []
    else:
        cpt = tm // WKV_CHUNK
        prev_spec = pl.BlockSpec((1, 1, w), lambda b, i: (b, 0, 0))
        out_specs = ([pl.BlockSpec((1, N_HEADS_A, cpt, WKV_CHUNK, HEAD_DIM), lambda b, i: (b, 0, i, 0, 0))] * 4
                     + [pl.BlockSpec((1, cpt, 1, D_A), lambda b, i: (b, i, 0, 0)), tile(D_A), tile(D_A),
                        pl.BlockSpec((1, 1, w), lambda b, i: (b, 0, 0))])
        out_shape = ([jax.ShapeDtypeStruct((nb, N_HEADS_A, t // WKV_CHUNK, WKV_CHUNK, HEAD_DIM), F32)] * 4
                     + [jax.ShapeDtypeStruct((nb, t // WKV_CHUNK, 1, D_A), F32),
                        jax.ShapeDtypeStruct((nb, t, D_A), F32), jax.ShapeDtypeStruct((nb, t, D_A), F32),
                        jax.ShapeDtypeStruct((nb, 1, w), F32)])
        scratch = [pltpu.VMEM((1, w), F32)]
    return pl.pallas_call(
        functools.partial(_rwkv_prep_body, per_row),
        grid=(nb, t // tm),
        in_specs=[tile(w), prev_spec, const((1, w)), const((1, D_A)), const((1, D_A)), const((1, D_A)),
                  const((1, D_A)), const((1, D_A)), const((LORA_W + LORA_A, 2 * D_A)),
                  const((LORA_G, D_A)), const((D_A, D_A))],
        out_specs=out_specs,
        out_shape=out_shape,
        scratch_shapes=scratch,
        compiler_params=_params(("parallel", "arbitrary")),
        name="rwkv_prep",
    )(z, prev, row(p['mu_shift']), row(p['w0']), row(p['a0']), row(p['k_k']), row(p['k_a']),
      row(p['r_k']), w2a, p['g2'], _head_ones(D_A))


def _wkv_scan_body(m2_ref, sv_ref, rk_ref, y0_ref, gam_ref, g_ref, bon_ref, lw_ref, lb_ref,
                   o_ref, s_out_ref, s_sc):
    nb, _, cps = m2_ref.shape[:3]
    c = WKV_CHUNK
    step = pl.program_id(0)

    @pl.when(step == 0)
    def _():
        s_sc[...] = jnp.zeros_like(s_sc)

    merge = lambda x: x.reshape((nb * N_HEADS_A,) + x.shape[2:])
    avg = jnp.full((nb * N_HEADS_A, HEAD_DIM, HEAD_DIM), 1.0 / HEAD_DIM, BF16)
    for ic in range(cps):
        rows = slice(ic * c, (ic + 1) * c)
        s0 = s_sc[...]
        y = _dot1(merge(rk_ref[:, :, ic]), s0, _BNT) + merge(y0_ref[:, :, ic])
        gam = jnp.stack([gam_ref[bi, ic][:, h * HEAD_DIM:(h + 1) * HEAD_DIM]
                         for bi in range(nb) for h in range(N_HEADS_A)])
        s_sc[...] = s0 * gam - _dot1(s0, merge(m2_ref[:, :, ic]), _BNN) + merge(sv_ref[:, :, ic])
        mu = _dotx(y, avg, _BNN)
        yc = y - mu
        yn = yc * lax.rsqrt(_dotx(yc * yc, avg, _BNN) + LNX_EPS)
        for bi in range(nb):
            for h in range(N_HEADS_A):
                sl = slice(h * HEAD_DIM, (h + 1) * HEAD_DIM)
                o_ref[bi, rows, sl] = ((yn[bi * N_HEADS_A + h] * lw_ref[:, sl] + lb_ref[:, sl]
                                        + bon_ref[bi, rows, sl]) * g_ref[bi, rows, sl])

    @pl.when(step == pl.num_programs(0) - 1)
    def _():
        s_out_ref[...] = s_sc[...]


def _wkv_scan(m2, sv, rk, y0, gam, g, bon, lnx_w, lnx_b):
    nb, t, _ = g.shape
    cps = 4
    tb = cps * WKV_CHUNK
    mat = pl.BlockSpec((nb, N_HEADS_A, cps, WKV_CHUNK, HEAD_DIM), lambda i: (0, 0, i, 0, 0))
    tile = pl.BlockSpec((nb, tb, D_A), lambda i: (0, i, 0))
    vec = pl.BlockSpec((1, D_A), lambda i: (0, 0))
    st = (nb * N_HEADS_A, HEAD_DIM, HEAD_DIM)
    return pl.pallas_call(
        _wkv_scan_body,
        grid=(t // tb,),
        in_specs=[mat] * 4 + [pl.BlockSpec((nb, cps, 1, D_A), lambda i: (0, i, 0, 0)), tile, tile, vec, vec],
        out_specs=[tile, pl.BlockSpec(st, lambda i: (0, 0, 0))],
        out_shape=[jax.ShapeDtypeStruct((nb, t, D_A), F32), jax.ShapeDtypeStruct(st, F32)],
        scratch_shapes=[pltpu.VMEM(st, F32)],
        compiler_params=_params(("arbitrary",)),
        name="wkv_scan",
    )(m2, sv, rk, y0, gam, g, bon, lnx_w.reshape(1, D_A), lnx_b.reshape(1, D_A))


def _wkv_step_body(s_ref, r_ref, ld_ref, k_ref, v_ref, kn_ref, b_ref, g_ref, bon_ref, lw_ref, lb_ref,
                   o_ref, s_out_ref):
    s0 = s_ref[0]
    ri = lax.broadcasted_iota(jnp.int32, (HEAD_DIM, HEAD_DIM), 0)
    ci = lax.broadcasted_iota(jnp.int32, (HEAD_DIM, HEAD_DIM), 1)
    eye = (ri == ci).astype(F32)
    u = jnp.sum(s0 * kn_ref[...], axis=-1, keepdims=True)
    v_col = jnp.sum(eye * v_ref[...], axis=-1, keepdims=True)
    s1 = s0 * jnp.exp(ld_ref[...]) - u * b_ref[...] + v_col * k_ref[...]
    s_out_ref[0] = s1
    y_col = jnp.sum(s1 * r_ref[...], axis=-1, keepdims=True)
    y = jnp.sum(eye * y_col, axis=-2, keepdims=True)
    mu = jnp.mean(y, axis=-1, keepdims=True)
    var = jnp.mean(jnp.square(y - mu), axis=-1, keepdims=True)
    yn = (y - mu) * lax.rsqrt(var + LNX_EPS)
    o_ref[...] = (yn * lw_ref[...] + lb_ref[...] + bon_ref[...]) * g_ref[...]


def _wkv_step(state_wkv, layer, r, ld, k, v, kn, b, g, bon, lnx_w, lnx_b):
    n = r.shape[0]
    tb = 8
    heads = lambda x: x.reshape(n, N_HEADS_A, 1, HEAD_DIM)
    row = pl.BlockSpec((tb, N_HEADS_A, 1, HEAD_DIM), lambda i: (i, 0, 0, 0))
    vec = pl.BlockSpec((1, N_HEADS_A, 1, HEAD_DIM), lambda i: (0, 0, 0, 0))
    st_in = pl.BlockSpec((1, tb, N_HEADS_A, HEAD_DIM, HEAD_DIM), lambda i: (layer, i, 0, 0, 0))
    st_out = pl.BlockSpec((1, tb, N_HEADS_A, HEAD_DIM, HEAD_DIM), lambda i: (0, i, 0, 0, 0))
    out, s1 = pl.pallas_call(
        _wkv_step_body,
        grid=(n // tb,),
        in_specs=[st_in] + [row] * 8 + [vec, vec],
        out_specs=[row, st_out],
        out_shape=[jax.ShapeDtypeStruct((n, N_HEADS_A, 1, HEAD_DIM), F32),
                   jax.ShapeDtypeStruct((1, n, N_HEADS_A, HEAD_DIM, HEAD_DIM), F32)],
        compiler_params=_params(("parallel",)),
        name="wkv_step",
    )(state_wkv, *[heads(x) for x in (r, ld, k, v, kn, b, g, bon)],
      lnx_w.reshape(1, N_HEADS_A, 1, HEAD_DIM), lnx_b.reshape(1, N_HEADS_A, 1, HEAD_DIM))
    return out.reshape(n, D_A), s1[0]


def _head_tile(x, h, lane):
    pair = x[:, (h // 2) * 128:(h // 2 + 1) * 128]
    if h % 2:
        pair = pltpu.roll(pair, HEAD_DIM, axis=1)
    return jnp.where(lane < HEAD_DIM, pair, 0.0)


def _qk_norm(x, gain, e):
    return x * lax.rsqrt(_dotx(x * x, e) * (1.0 / HEAD_DIM) + EPS) * gain


MOBA_DEN_LANE = HEAD_DIM
MOBA_MAX_BLOCKS = 128 - HEAD_DIM
MOBA_GROUP = 8
MOBA_QBLOCKS = 2


def _moba_prep_body(q_ref, k_ref, v_ref, qg_ref, kg_ref, e_ref, qa_o, kh_o, vh_o, kn_o, km_sc):
    n = pl.program_id(1)

    @pl.when(n == 0)
    def _():
        km_sc[...] = jnp.zeros_like(km_sc)

    e = e_ref[...]
    qn = _qk_norm(q_ref[0], qg_ref[...], e)
    kn = _qk_norm(k_ref[0], kg_ref[...], e)
    v = v_ref[0]
    kn_o[0] = kn
    tq = qn.shape[0]
    lane = lax.broadcasted_iota(jnp.int32, (tq, 128), 1)
    lane_km = lax.broadcasted_iota(jnp.int32, (MOBA_MAX_BLOCKS, 128), 1)
    blk = lax.broadcasted_iota(jnp.int32, (MOBA_MAX_BLOCKS, tq), 0)
    valid = blk < n
    key_marks = jnp.where(lane == HEAD_DIM + n, 1.0, 0.0)
    val_marks = jnp.where(lane == MOBA_DEN_LANE, 1.0, 0.0)
    km = km_sc[...]
    for h in range(N_HEADS_B):
        qt = _head_tile(qn, h, lane)
        gate = jnp.where(valid, _dot3(_head_tile(km, h, lane_km), qt, _NT), -jnp.inf)
        keep = blk == n
        for _ in range(MOBA_TOPK):
            top = jnp.max(gate, axis=0, keepdims=True)
            first = jnp.min(jnp.where(gate == top, blk, 1 << 20), axis=0, keepdims=True)
            pick = blk == first
            keep = jnp.logical_or(keep, jnp.logical_and(pick, valid))
            gate = jnp.where(pick, -jnp.inf, gate)
        pen = jnp.where(keep, 0.0, NEG_BIG)
        pen = jnp.concatenate([jnp.zeros((HEAD_DIM, tq), F32), pen], axis=0).T
        qa_o[0, h] = jnp.where(lane < HEAD_DIM, qt * (HEAD_DIM ** -0.5 * LOG2_E), pen).astype(BF16)
        kh_o[0, h] = (_head_tile(kn, h, lane) + key_marks).astype(BF16)
        vh_o[0, h] = (_head_tile(v, h, lane) + val_marks).astype(BF16)
    km_sc[pl.ds(n, 1), :] = jnp.mean(kn, axis=0, keepdims=True)


def _moba_prep(q, k, v, q_gain, k_gain):
    nb, t, _ = q.shape
    nblk = t // MOBA_BLOCK
    assert nblk <= MOBA_MAX_BLOCKS and nblk % MOBA_GROUP == 0 and nblk % MOBA_QBLOCKS == 0
    tile = pl.BlockSpec((1, MOBA_BLOCK, D_B), lambda b, i: (b, i, 0))
    vec = pl.BlockSpec((1, D_B), lambda b, i: (0, 0))
    hm = pl.BlockSpec((1, N_HEADS_B, MOBA_BLOCK, 128), lambda b, i: (b, 0, i, 0))
    return pl.pallas_call(
        _moba_prep_body,
        grid=(nb, nblk),
        in_specs=[tile, tile, tile, vec, vec, pl.BlockSpec((D_B, D_B), lambda b, i: (0, 0))],
        out_specs=[hm, hm, hm, tile],
        out_shape=[jax.ShapeDtypeStruct((nb, N_HEADS_B, t, 128), BF16)] * 3
                  + [jax.ShapeDtypeStruct((nb, t, D_B), F32)],
        scratch_shapes=[pltpu.VMEM((MOBA_MAX_BLOCKS, D_B), F32)],
        compiler_params=_params(("parallel", "arbitrary")),
        name="moba_prep",
    )(q, k, v, jnp.tile(q_gain, N_HEADS_B).reshape(1, D_B), jnp.tile(k_gain, N_HEADS_B).reshape(1, D_B),
      _head_ones(D_B))


def _moba_attn_body(q_ref, k_ref, v_ref, tab_ref, o_ref, s_sc, mx_sc, acc_sc):
    first = pl.program_id(2) * MOBA_QBLOCKS
    n_groups = (first + MOBA_QBLOCKS - 1 + MOBA_GROUP) // MOBA_GROUP
    span = MOBA_GROUP * MOBA_BLOCK

    def key_rows(g):
        return pl.ds(pl.multiple_of(g * span, span), span)

    def q_rows(r):
        return slice(r * MOBA_BLOCK, (r + 1) * MOBA_BLOCK)

    mx_sc[...] = jnp.full_like(mx_sc, -jnp.inf)

    def pass1(g, carry):
        kg = k_ref[0, 0, key_rows(g), :]
        for r in range(MOBA_QBLOCKS):
            s = _mm(q_ref[0, 0, q_rows(r), :], kg, _NT)
            m = mx_sc[r]
            for i in range(MOBA_GROUP):
                cols = slice(i * MOBA_BLOCK, (i + 1) * MOBA_BLOCK)
                offset = first + r - (g * MOBA_GROUP + i)
                si = s[:, cols] + tab_ref[0, jnp.clip(offset, 0, N_BIAS_TILES - 1)]
                s_sc[g, q_rows(r), cols] = si
                m = jnp.maximum(m, si)
            mx_sc[r] = m
        return carry

    lax.fori_loop(0, n_groups, pass1, 0)
    m = [jnp.max(mx_sc[r], axis=1, keepdims=True) for r in range(MOBA_QBLOCKS)]
    acc_sc[...] = jnp.zeros_like(acc_sc)

    def pass2(g, carry):
        vg = v_ref[0, 0, key_rows(g), :]
        for r in range(MOBA_QBLOCKS):
            p = jnp.exp2(s_sc[g, q_rows(r), :] - m[r]).astype(BF16)
            acc_sc[q_rows(r), :] += _mm(p, vg)
        return carry

    lax.fori_loop(0, n_groups, pass2, 0)
    acc = acc_sc[...]
    o_ref[0, 0] = acc / acc[:, MOBA_DEN_LANE:MOBA_DEN_LANE + 1]


def _moba_attn(qa, kh, vh, bias_tiles):
    nb, nh, t, _ = qa.shape
    tq = MOBA_QBLOCKS * MOBA_BLOCK
    n_groups_max = t // (MOBA_GROUP * MOBA_BLOCK)
    return pl.pallas_call(
        _moba_attn_body,
        grid=(nh, nb, t // tq),
        in_specs=[pl.BlockSpec((1, 1, tq, 128), lambda h, b, a: (b, h, a, 0)),
                  pl.BlockSpec((1, 1, t, 128), lambda h, b, a: (b, h, 0, 0)),
                  pl.BlockSpec((1, 1, t, 128), lambda h, b, a: (b, h, 0, 0)),
                  pl.BlockSpec((1, N_BIAS_TILES, MOBA_BLOCK, MOBA_BLOCK), lambda h, b, a: (h, 0, 0, 0))],
        out_specs=pl.BlockSpec((1, 1, tq, 128), lambda h, b, a: (b, h, a, 0)),
        out_shape=jax.ShapeDtypeStruct((nb, nh, t, 128), F32),
        scratch_shapes=[pltpu.VMEM((n_groups_max, tq, MOBA_GROUP * MOBA_BLOCK), F32),
                        pltpu.VMEM((MOBA_QBLOCKS, MOBA_BLOCK, MOBA_BLOCK), F32),
                        pltpu.VMEM((tq, 128), F32)],
        compiler_params=_params(("parallel", "parallel", "arbitrary")),
        name="moba_attn",
    )(qa, kh, vh, bias_tiles)


def _qk_norm_rows_body(q_ref, k_ref, qg_ref, kg_ref, e_ref, qn_o, kn_o):
    e = e_ref[...]
    qn_o[...] = _qk_norm(q_ref[...], qg_ref[...], e)
    kn_o[...] = _qk_norm(k_ref[...], kg_ref[...], e)


def _qk_norm_rows(q, k, q_gain, k_gain):
    n = q.shape[0]
    return pl.pallas_call(
        _qk_norm_rows_body,
        out_shape=[jax.ShapeDtypeStruct((n, D_B), F32)] * 2,
        name="qk_norm_rows",
    )(q, k, jnp.tile(q_gain, N_HEADS_B).reshape(1, D_B), jnp.tile(k_gain, N_HEADS_B).reshape(1, D_B),
      _head_ones(D_B))


PAGED_SEQS = 1


def _moba_paged_body(n_pages, pt_ref, qb_ref, knb_ref, vnb_ref, tab_ref, *rest):
    o_ref = rest[2 * PAGED_SEQS * n_pages]
    for s in range(PAGED_SEQS):
        k_pages = rest[s * n_pages:(s + 1) * n_pages]
        v_pages = rest[(PAGED_SEQS + s) * n_pages:(PAGED_SEQS + s + 1) * n_pages]
        o_ref[s] = _moba_paged_one(qb_ref[s], knb_ref[s], vnb_ref[s], tab_ref, k_pages, v_pages)


def _moba_paged_one(qb, knb, vnb, tab_ref, k_pages, v_pages):
    n_pages = len(k_pages)
    page_size = qb.shape[-1]
    per_blk = MOBA_BLOCK // page_size
    n_blk = n_pages // per_blk
    scale = HEAD_DIM ** -0.5
    raw = [jnp.sum(k_pages[p][0, 0] * qb, axis=1, keepdims=True) for p in range(n_pages)]

    gates = []
    for n in range(n_blk):
        tot = raw[n * per_blk]
        for i in range(1, per_blk):
            tot = tot + raw[n * per_blk + i]
        gates.append(jnp.sum(tot, axis=2, keepdims=True) * (1.0 / MOBA_BLOCK))
    keep = [jnp.zeros(gates[0].shape, jnp.bool_)] * n_blk
    for _ in range(min(MOBA_TOPK, n_blk + 1)):
        top = functools.reduce(jnp.maximum, gates)
        found = jnp.zeros(top.shape, jnp.bool_)
        for n in range(n_blk):
            hit = jnp.logical_and(gates[n] == top, jnp.logical_not(found))
            keep[n] = jnp.logical_or(keep[n], hit)
            found = jnp.logical_or(found, hit)
            gates[n] = jnp.where(hit, -jnp.inf, gates[n])

    logits = [raw[p] * scale + tab_ref[p] + jnp.where(keep[p // per_blk], 0.0, NEG_BIG)
              for p in range(n_pages)]
    own = jnp.sum(knb * qb, axis=1, keepdims=True) * scale + tab_ref[n_pages]
    m = jnp.max(functools.reduce(jnp.maximum, logits + [own]), axis=2, keepdims=True)
    lane = lax.broadcasted_iota(jnp.int32, own.shape, 2)
    p_own = jnp.where(lane == 0, jnp.exp(own - m), 0.0)
    probs = [jnp.exp(s - m) for s in logits]
    inv_l = 1.0 / jnp.sum(functools.reduce(jnp.add, probs + [p_own]), axis=2, keepdims=True)
    acc = (p_own * inv_l) * vnb
    for p in range(n_pages):
        acc = acc + (probs[p] * inv_l) * v_pages[p][0, 0]
    h1, h2, h3 = _split3(acc.reshape(N_HEADS_B * HEAD_DIM, page_size))
    ones = jnp.ones((8, page_size), BF16)
    return (_mm(ones, h1, _NT) + (_mm(ones, h2, _NT) + _mm(ones, h3, _NT)))[0:1]


def _moba_paged(qn, kn, v, cache_kt, cache_vt, page_table, layer, bias_pages):
    n, n_pages = page_table.shape
    page_size = cache_kt.shape[-1]
    tile = (N_HEADS_B, HEAD_DIM, page_size)
    lanes = lambda x: jnp.broadcast_to(x.reshape(n, N_HEADS_B, HEAD_DIM, 1), (n,) + tile)
    assert n % PAGED_SEQS == 0
    row = pl.BlockSpec((PAGED_SEQS,) + tile, lambda b, pt: (b, 0, 0, 0))
    page = lambda s, p: pl.BlockSpec((1, 1) + tile, lambda b, pt: (layer, pt[b * PAGED_SEQS + s, p], 0, 0, 0))
    pages = [page(s, p) for s in range(PAGED_SEQS) for p in range(n_pages)]
    out = pl.pallas_call(
        functools.partial(_moba_paged_body, n_pages),
        grid_spec=pltpu.PrefetchScalarGridSpec(
            num_scalar_prefetch=1,
            grid=(n // PAGED_SEQS,),
            in_specs=[row, row, row, pl.BlockSpec(bias_pages.shape, lambda b, pt: (0, 0, 0, 0))] + pages * 2,
            out_specs=pl.BlockSpec((PAGED_SEQS, 1, D_B), lambda b, pt: (b, 0, 0))),
        out_shape=jax.ShapeDtypeStruct((n, 1, D_B), F32),
        compiler_params=_params(("arbitrary",)),
        name="moba_paged",
    )(page_table, lanes(qn), lanes(kn), lanes(v), bias_pages,
      *([cache_kt] * (PAGED_SEQS * n_pages)), *([cache_vt] * (PAGED_SEQS * n_pages)))
    return out.reshape(n, D_B)


def _gmlp_front(uv, vn_gain, e):
    ge = 0.5 * uv * (1.0 + lax.erf(uv * math.sqrt(0.5)))
    u = ge[:, :D_C]
    v = ge[:, D_C:]
    vg = v * lax.rsqrt(_dotx(v * v, e) * (1.0 / HEAD_DIM) + EPS) * vn_gain
    return u, vg


def _rms(x, gain):
    return x * lax.rsqrt(jnp.mean(x * x, axis=-1, keepdims=True) + EPS) * gain


def _gmlp_body(uv_ref, vn_ref, ws_ref, bs_ref, on_ref, e_ref, o_ref):
    u, vg = _gmlp_front(uv_ref[0], vn_ref[...], e_ref[...])
    tm = u.shape[0]
    ri = lax.broadcasted_iota(jnp.int32, (CHUNK_C, CHUNK_C), 0)
    ci = lax.broadcasted_iota(jnp.int32, (CHUNK_C, CHUNK_C), 1)
    group = lax.broadcasted_iota(jnp.int32, (CHUNK_C, D_C), 1) // HEAD_DIM
    ws = [jnp.where(ri >= ci, ws_ref[g], 0.0).astype(BF16) for g in range(N_GROUPS_C)]
    for c in range(tm // CHUNK_C):
        rows = slice(c * CHUNK_C, (c + 1) * CHUNK_C)
        vc = vg[rows].astype(BF16)
        mixed = bs_ref[...]
        for g in range(N_GROUPS_C):
            mixed = mixed + jnp.where(group == g, _mm(ws[g], vc), 0.0)
        o_ref[0, rows, :] = _rms(u[rows] * mixed, on_ref[...])


def _gmlp(uv, p):
    nb, t, _ = uv.shape
    tm = 512
    const = lambda shape: pl.BlockSpec(shape, lambda b, i: (0,) * len(shape))
    bs = jnp.repeat(p['b_s'].T, HEAD_DIM, axis=1)
    return pl.pallas_call(
        _gmlp_body,
        grid=(nb, t // tm),
        in_specs=[pl.BlockSpec((1, tm, 2 * D_C), lambda b, i: (b, i, 0)), const((1, D_C)),
                  const((N_GROUPS_C, CHUNK_C, CHUNK_C)), const((CHUNK_C, D_C)), const((1, D_C)),
                  const((D_C, D_C))],
        out_specs=pl.BlockSpec((1, tm, D_C), lambda b, i: (b, i, 0)),
        out_shape=jax.ShapeDtypeStruct((nb, t, D_C), F32),
        compiler_params=_params(("parallel", "parallel")),
        name="gmlp",
    )(uv, p['v_norm'].reshape(1, D_C), p['w_s'], bs, p['out_norm_c'].reshape(1, D_C), _head_ones(D_C))


def _gmlp_first_pos_body(uv_ref, vn_ref, w00_ref, b00_ref, on_ref, e_ref, o_ref, vg_ref):
    u, vg = _gmlp_front(uv_ref[...], vn_ref[...], e_ref[...])
    vg_ref[...] = vg
    o_ref[...] = _rms(u * (vg * w00_ref[...] + b00_ref[...]), on_ref[...])


def _gmlp_first_pos(uv, p):
    n = uv.shape[0]
    w00 = jnp.repeat(p['w_s'][:, 0, 0], HEAD_DIM).reshape(1, D_C)
    b00 = jnp.repeat(p['b_s'][:, 0], HEAD_DIM).reshape(1, D_C)
    return pl.pallas_call(
        _gmlp_first_pos_body,
        out_shape=[jax.ShapeDtypeStruct((n, D_C), F32)] * 2,
        name="gmlp_first_pos",
    )(uv, p['v_norm'].reshape(1, D_C), w00, b00, p['out_norm_c'].reshape(1, D_C), _head_ones(D_C))


def _mix_ffn_body(head_major, x_ref, oa_ref, ob_ref, oc_ref, mod_ref, nb_ref, nf_ref, wo_ref, wu_ref, wd_ref,
                  o_ref, x1_sc, h2_sc, acc_sc):
    j = pl.program_id(2)
    d = D_MODEL

    @pl.when(j == 0)
    def _():
        if head_major:
            lane = lax.broadcasted_iota(jnp.int32, ob_ref.shape[2:], 1)
            ob = jnp.concatenate(
                [jnp.where(lane < HEAD_DIM, ob_ref[0, h], pltpu.roll(ob_ref[0, h + 1], HEAD_DIM, axis=1))
                 for h in range(0, N_HEADS_B, 2)], axis=1)
        else:
            ob = ob_ref[0]
        cat = jnp.concatenate([oa_ref[0], _rms(ob, nb_ref[...]), oc_ref[0]], axis=1)
        x1 = x_ref[0] + mod_ref[0, :, 2 * d:3 * d] * _dot1(cat, wo_ref[0])
        x1_sc[...] = x1
        h2 = _rms(x1, nf_ref[...]) * (1.0 + mod_ref[0, :, 4 * d:5 * d]) + mod_ref[0, :, 3 * d:4 * d]
        h2_sc[...] = h2.astype(BF16)
        acc_sc[...] = jnp.zeros_like(acc_sc)

    up = _mm(h2_sc[...], wu_ref[0])
    acc_sc[...] += _mm(jnp.square(jnp.maximum(up, 0.0)).astype(BF16), wd_ref[0])

    @pl.when(j == pl.num_programs(2) - 1)
    def _():
        o_ref[0] = x1_sc[...] + mod_ref[0, :, 5 * d:6 * d] * acc_sc[...]


def _mix_ffn(x, oa, ob, oc, mod, p, w_out_bf, w_up_bf, w_down_bf, layer, per_row, head_major):
    nb, t, d = x.shape
    tm = min(t, 512)
    tf = 1024
    mod_rows = tm if per_row else 1
    tile = lambda width: pl.BlockSpec((1, tm, width), lambda b, i, j: (b, i, 0))
    const = lambda shape: pl.BlockSpec(shape, lambda b, i, j: (0,) * len(shape))
    if head_major:
        ob_spec = pl.BlockSpec((1, N_HEADS_B, tm, 128), lambda b, i, j: (b, 0, i, 0))
    else:
        ob_spec = tile(D_B)
    return pl.pallas_call(
        functools.partial(_mix_ffn_body, head_major),
        grid=(nb, t // tm, D_FF // tf),
        in_specs=[tile(d), tile(D_A), ob_spec, tile(D_C),
                  pl.BlockSpec((1, mod_rows, 6 * d), lambda b, i, j: (b, i if per_row else 0, 0)),
                  const((1, D_B)), const((1, d)),
                  pl.BlockSpec((1, d, d), lambda b, i, j: (layer, 0, 0)),
                  pl.BlockSpec((1, d, tf), lambda b, i, j: (layer, 0, j)),
                  pl.BlockSpec((1, tf, d), lambda b, i, j: (layer, j, 0))],
        out_specs=tile(d),
        out_shape=jax.ShapeDtypeStruct((nb, t, d), F32),
        scratch_shapes=[pltpu.VMEM((tm, d), F32), pltpu.VMEM((tm, d), BF16), pltpu.VMEM((tm, d), F32)],
        compiler_params=_params(("parallel", "parallel", "arbitrary")),
        name="mix_ffn",
    )(x, oa, ob, oc, mod, p['out_norm_b'].reshape(1, D_B), p['norm_ffn'].reshape(1, d),
      w_out_bf, w_up_bf, w_down_bf)


def _prompt_layer(x, mod, p, big, layer, bias_tiles):
    nb, t, _ = x.shape
    z, q, k, v, uv = _inproj(x, mod, p['norm_mix'], big['w_in'], layer, per_row=False)
    prep = _rwkv_prep(z, jnp.zeros((nb, 1, W_SHIFT), F32), p, per_row=False)
    shift_new = prep[7].reshape(nb, W_SHIFT)
    out_a, wkv_new = _wkv_scan(*prep[:7], p['lnx_w'], p['lnx_b'])
    wkv_new = wkv_new.reshape(nb, N_HEADS_A, HEAD_DIM, HEAD_DIM)
    qa, kh, vh, k_new = _moba_prep(q, k, v, p['q_norm'], p['k_norm'])
    out_b = _moba_attn(qa, kh, vh, bias_tiles)
    out_c = _gmlp(uv, p)
    x = _mix_ffn(x, out_a, out_b, out_c, mod, p, big['w_out'], big['w_up'], big['w_down'], layer,
                 per_row=False, head_major=True)
    return x, k_new, v, wkv_new, shift_new


def _sample_layer(x, mod, p, big, layer, bias_pages, cache_kt, cache_vt, page_table, state_wkv, shift0):
    n = x.shape[1]
    z, q, k, v, uv = _inproj(x, mod, p['norm_mix'], big['w_in'], layer, per_row=True)
    prep = _rwkv_prep(z, shift0.reshape(1, n, W_SHIFT), p, per_row=True)
    out_a, wkv_new = _wkv_step(state_wkv, layer, *[a.reshape(n, D_A) for a in prep], p['lnx_w'], p['lnx_b'])
    qn, kn = _qk_norm_rows(q.reshape(n, D_B), k.reshape(n, D_B), p['q_norm'], p['k_norm'])
    v = v.reshape(n, D_B)
    out_b = _moba_paged(qn, kn, v, cache_kt, cache_vt, page_table, layer, bias_pages)
    out_c, vg = _gmlp_first_pos(uv.reshape(n, 2 * D_C), p)
    x = _mix_ffn(x, out_a.reshape(1, n, D_A), out_b.reshape(1, n, D_B), out_c.reshape(1, n, D_C), mod, p,
                 big['w_out'], big['w_up'], big['w_down'], layer, per_row=True, head_major=False)
    return x, kn, v, wkv_new, z.reshape(n, W_SHIFT), vg


def kernel(x_prompt, x_sample, c_prompt, c_sample, cache_k, cache_v, page_table, state_wkv, state_shift, norm_mix, w_ada, b_ada, w_in, mu_shift, w0, w2, a0, a2, g2, k_k, k_a, r_k, lnx_w, lnx_b, q_norm, k_norm, rel_bias, out_norm_b, v_norm, w_s, b_s, out_norm_c, w_out, norm_ffn, w_up, w_down):
    layer_w = {
        'norm_mix': norm_mix, 'mu_shift': mu_shift, 'w0': w0, 'w2': w2, 'a0': a0, 'a2': a2, 'g2': g2,
        'k_k': k_k, 'k_a': k_a, 'r_k': r_k, 'lnx_w': lnx_w, 'lnx_b': lnx_b, 'q_norm': q_norm,
        'k_norm': k_norm, 'out_norm_b': out_norm_b, 'v_norm': v_norm, 'w_s': w_s, 'b_s': b_s,
        'out_norm_c': out_norm_c, 'norm_ffn': norm_ffn,
    }
    depth = w_in.shape[0]
    n_prompt, seq, d = x_prompt.shape
    n_dec = x_sample.shape[0]
    past_len = page_table.shape[1] * cache_k.shape[2]
    big = {'w_in': w_in.astype(BF16), 'w_out': w_out.astype(BF16),
           'w_up': w_up.astype(BF16), 'w_down': w_down.astype(BF16)}
    mod = _ada(jnp.concatenate([c_sample, c_prompt], axis=0), w_ada, b_ada)
    bias_tiles = _bias_tiles(rel_bias)
    bias_pages = _bias_pages(rel_bias, past_len, cache_k.shape[2])
    cache_kt = cache_k.transpose(0, 1, 3, 4, 2)
    cache_vt = cache_v.transpose(0, 1, 3, 4, 2)

    xp = x_prompt
    xs = x_sample.reshape(1, n_dec, d)
    outs = [[] for _ in range(9)]
    for l in range(depth):
        p = {name: arr[l] for name, arr in layer_w.items()}
        mod_s = mod[l, :n_dec].reshape(1, n_dec, 6 * d)
        mod_p = mod[l, n_dec:].reshape(n_prompt, 1, 6 * d)
        xp, k_p, v_p, wkv_p, sh_p = _prompt_layer(xp, mod_p, p, big, l, bias_tiles)
        xs, k_s, v_s, wkv_s, sh_s, vg_s = _sample_layer(
            xs, mod_s, p, big, l, bias_pages, cache_kt, cache_vt, page_table, state_wkv, state_shift[l])
        for lst, val in zip(outs, (k_p, v_p, k_s, v_s, wkv_p, wkv_s, sh_p, sh_s, vg_s)):
            lst.append(val)
    kp, vp, ks, vs, wp, ws, sp, ss, gs = (jnp.stack(o) for o in outs)
    heads = lambda a, rows: a.reshape(depth, rows, -1, N_HEADS_B, HEAD_DIM)
    return (xp, xs.reshape(n_dec, 1, d),
            heads(kp, n_prompt), heads(vp, n_prompt), heads(ks, n_dec), heads(vs, n_dec),
            wp, ws, sp, ss, gs.reshape(depth, n_dec, 1, D_C))
```

```python
import functools
import math

import numpy as np
import jax
import jax.numpy as jnp
from jax import lax
from jax.experimental import pallas as pl
from jax.experimental.pallas import tpu as pltpu

F32 = jnp.float32
BF16 = jnp.bfloat16

D_MODEL = 1024
HEAD_DIM = 64
N_HEADS_A = 6
N_HEADS_B = 6
N_GROUPS_C = 4
D_A = N_HEADS_A * HEAD_DIM
D_B = N_HEADS_B * HEAD_DIM
D_C = N_GROUPS_C * HEAD_DIM
LORA_W = 64
LORA_A = 64
LORA_G = 128
W_SHIFT = 3 * D_A + LORA_W + LORA_A + LORA_G
N_IN = W_SHIFT + 3 * D_B + 2 * D_C
IN_SEGS = (0, W_SHIFT, W_SHIFT + D_B, W_SHIFT + 2 * D_B, W_SHIFT + 3 * D_B, N_IN)
MOBA_BLOCK = 256
MOBA_TOPK = 3
CHUNK_C = 128
N_BUCKETS = 32
REL_MAX_DIST = 4096
D_FF = 4 * D_MODEL
EPS = 1e-6
LNX_EPS = 64e-5
WKV_CHUNK = 64
NEG_BIG = -1e30
LOG2_E = math.log2(math.e)
V7X_VMEM_LIMIT = 56 * 1024 * 1024


def _bucket_saturation_tiles():
    max_exact = N_BUCKETS // 2
    n = np.arange(1, 2 * REL_MAX_DIST, dtype=np.float64)
    large = max_exact + (np.log(n / max_exact) / math.log(REL_MAX_DIST / max_exact)
                         * (N_BUCKETS - max_exact)).astype(np.int64)
    first_sat = int(n[np.argmax(np.minimum(large, N_BUCKETS - 1) == N_BUCKETS - 1)])
    return -(-(first_sat + MOBA_BLOCK) // MOBA_BLOCK) + 1


N_BIAS_TILES = _bucket_saturation_tiles() + 1

_NN = (((1,), (0,)), ((), ()))
_NT = (((1,), (1,)), ((), ()))
_TN = (((0,), (0,)), ((), ()))


def _mm(a, b, dims=_NN):
    return lax.dot_general(a, b, dims, preferred_element_type=F32)


def _split2(x):
    hi = x.astype(BF16)
    lo = (x - hi.astype(F32)).astype(BF16)
    return hi, lo


def _split3(x):
    h1 = x.astype(BF16)
    r1 = x - h1.astype(F32)
    h2 = r1.astype(BF16)
    h3 = (r1 - h2.astype(F32)).astype(BF16)
    return h1, h2, h3


def _dot1(a, b, dims=_NN):
    return _mm(a.astype(BF16), b.astype(BF16), dims)


def _dot3(a, b, dims=_NN):
    ah, al = _split2(a)
    bh, bl = _split2(b)
    return _mm(ah, bh, dims) + (_mm(ah, bl, dims) + _mm(al, bh, dims))


def _dotx(a, e, dims=_NN):
    ah, al = _split2(a)
    return _mm(ah, e, dims) + _mm(al, e, dims)


def _dotx3(a, e, dims=_NN):
    h1, h2, h3 = _split3(a)
    return _mm(h1, e, dims) + (_mm(h2, e, dims) + _mm(h3, e, dims))


def _params(sem):
    return pltpu.CompilerParams(dimension_semantics=sem, vmem_limit_bytes=V7X_VMEM_LIMIT)


def _sigmoid(x):
    return jax.nn.sigmoid(x)


def _head_ones(width):
    i = np.arange(width) // HEAD_DIM
    return jnp.asarray(i[:, None] == i[None, :], dtype=BF16)


def _ada_body(c_ref, w_ref, b_ref, o_ref):
    c = c_ref[...]
    o_ref[0] = _dot3(c * _sigmoid(c), w_ref[0]) + b_ref[0]


def _ada(c_all, w_ada, b_ada):
    depth, d, n = w_ada.shape
    m = c_all.shape[0]
    tn = 1536
    return pl.pallas_call(
        _ada_body,
        grid=(depth, n // tn),
        in_specs=[pl.BlockSpec((m, d), lambda l, j: (0, 0)),
                  pl.BlockSpec((1, d, tn), lambda l, j: (l, 0, j)),
                  pl.BlockSpec((1, 1, tn), lambda l, j: (l, 0, j))],
        out_specs=pl.BlockSpec((1, m, tn), lambda l, j: (l, 0, j)),
        out_shape=jax.ShapeDtypeStruct((depth, m, n), F32),
        compiler_params=_params(("parallel", "parallel")),
        name="ada_mod",
    )(c_all, w_ada, b_ada.reshape(depth, 1, n))


def _rel_bucket(dist):
    n = jnp.maximum(dist, 0)
    max_exact = N_BUCKETS // 2
    nf = jnp.maximum(n, 1).astype(F32)
    large = max_exact + (jnp.log(nf / max_exact) / math.log(REL_MAX_DIST / max_exact)
                         * (N_BUCKETS - max_exact)).astype(jnp.int32)
    large = jnp.minimum(large, N_BUCKETS - 1)
    return jnp.where(n < max_exact, n, large)


def _bias_tiles_body(rb_ref, o_ref):
    h = pl.program_id(0)
    d = pl.program_id(1)
    ti = lax.broadcasted_iota(jnp.int32, (MOBA_BLOCK, MOBA_BLOCK), 0)
    tj = lax.broadcasted_iota(jnp.int32, (MOBA_BLOCK, MOBA_BLOCK), 1)
    dist = d * MOBA_BLOCK + ti - tj
    bucket = _rel_bucket(dist)
    acc = jnp.zeros((MOBA_BLOCK, MOBA_BLOCK), F32)
    for b in range(N_BUCKETS):
        acc = jnp.where(bucket == b, rb_ref[h * N_BUCKETS + b], acc)
    o_ref[0, 0] = jnp.where(dist >= 0, acc * LOG2_E, NEG_BIG)


def _bias_tiles(rel_bias):
    rb = rel_bias.T.reshape(-1)
    return pl.pallas_call(
        _bias_tiles_body,
        grid=(N_HEADS_B, N_BIAS_TILES),
        in_specs=[pl.BlockSpec(memory_space=pltpu.SMEM)],
        out_specs=pl.BlockSpec((1, 1, MOBA_BLOCK, MOBA_BLOCK), lambda h, d: (h, d, 0, 0)),
        out_shape=jax.ShapeDtypeStruct((N_HEADS_B, N_BIAS_TILES, MOBA_BLOCK, MOBA_BLOCK), F32),
        compiler_params=_params(("parallel", "parallel")),
        name="bias_tiles",
    )(rb)


def _bias_pages_body(past_len, rb_ref, o_ref):
    page_size = o_ref.shape[-1]
    pos = pl.program_id(0) * page_size + lax.broadcasted_iota(jnp.int32, o_ref.shape[1:], 2)
    bucket = _rel_bucket(past_len - pos)
    acc = jnp.zeros(o_ref.shape[1:], F32)
    for b in range(N_BUCKETS):
        acc = jnp.where(bucket == b, rb_ref[b], acc)
    o_ref[0] = acc


def _bias_pages(rel_bias, past_len, page_size):
    n_pages = past_len // page_size + 1
    rb = jnp.broadcast_to(rel_bias[:, :, None, None], (N_BUCKETS, N_HEADS_B, 1, page_size))
    return pl.pallas_call(
        functools.partial(_bias_pages_body, past_len),
        grid=(n_pages,),
        in_specs=[pl.BlockSpec((N_BUCKETS, N_HEADS_B, 1, page_size), lambda p: (0, 0, 0, 0))],
        out_specs=pl.BlockSpec((1, N_HEADS_B, 1, page_size), lambda p: (p, 0, 0, 0)),
        out_shape=jax.ShapeDtypeStruct((n_pages, N_HEADS_B, 1, page_size), F32),
        compiler_params=_params(("parallel",)),
        name="bias_pages",
    )(rb)


def _inproj_body(x_ref, mod_ref, gain_ref, w_ref, *out_refs):
    x = x_ref[0]
    d = x.shape[-1]
    y = x * lax.rsqrt(jnp.mean(x * x, axis=-1, keepdims=True) + EPS) * gain_ref[...]
    sh = mod_ref[0, :, 0:d]
    sc = mod_ref[0, :, d:2 * d]
    h = (y * (1.0 + sc) + sh).astype(BF16)
    for ref, a, b in zip(out_refs, IN_SEGS[:-1], IN_SEGS[1:]):
        ref[0] = _mm(h, w_ref[0, :, a:b])


def _inproj(x, mod, gain, w_in_bf, layer, per_row):
    nb, t, d = x.shape
    tm = min(t, 512)
    mod_rows = tm if per_row else 1
    widths = [b - a for a, b in zip(IN_SEGS[:-1], IN_SEGS[1:])]
    return pl.pallas_call(
        _inproj_body,
        grid=(nb, t // tm),
        in_specs=[pl.BlockSpec((1, tm, d), lambda b, i: (b, i, 0)),
                  pl.BlockSpec((1, mod_rows, 6 * d), lambda b, i: (b, i if per_row else 0, 0)),
                  pl.BlockSpec((1, d), lambda b, i: (0, 0)),
                  pl.BlockSpec((1, d, N_IN), lambda b, i: (layer, 0, 0))],
        out_specs=[pl.BlockSpec((1, tm, w), lambda b, i: (b, i, 0)) for w in widths],
        out_shape=[jax.ShapeDtypeStruct((nb, t, w), F32) for w in widths],
        compiler_params=_params(("parallel", "parallel")),
        name="in_proj",
    )(x, mod, gain.reshape(1, d), w_in_bf)


_BNN = (((2,), (1,)), ((0,), (0,)))
_BNT = (((2,), (2,)), ((0,), (0,)))
_BTN = (((1,), (1,)), ((0,), (0,)))


def _split_heads(x):
    return jnp.stack([x[:, h * HEAD_DIM:(h + 1) * HEAD_DIM] for h in range(x.shape[1] // HEAD_DIM)])


def _wkv_local(kt, rt, kh, bh, kb, bb, vv, tri_s, tri_i, eye):
    c = kt.shape[1]
    aa = _dot1(jnp.concatenate([kt, rt], axis=1), jnp.concatenate([kh, bh], axis=1), _BNT)
    a_kk = jnp.where(tri_s, aa[:, :c, :c], 0.0)
    a_kb = jnp.where(tri_s, aa[:, :c, c:], 0.0)
    a_rk = jnp.where(tri_i, aa[:, c:, :c], 0.0)
    a_rb = jnp.where(tri_i, aa[:, c:, c:], 0.0)
    x = -a_kb
    inv = eye + x
    span = 2
    while span < c:
        x = _dot1(x, x, _BNN)
        inv = inv + _dot1(inv, x, _BNN)
        span *= 2
    av = _dot1(jnp.concatenate([a_kk, a_rk], axis=1), vv, _BNN)
    pkw = _dot1(inv, jnp.concatenate([kt, av[:, :c]], axis=2), _BNN)
    arb = _dot1(a_rb, pkw, _BNN)
    mb = _dot1(pkw, bb, _BTN)
    rk = rt - arb[:, :, :HEAD_DIM]
    y0 = av[:, c:] - arb[:, :, HEAD_DIM:]
    sv = _dot1(vv, kb, _BTN) - mb[:, HEAD_DIM:]
    return mb[:, :HEAD_DIM], sv, rk, y0


def _rwkv_prep_body(per_row, z_ref, prev_ref, mu_ref, w0_ref, a0_ref, kk_ref, ka_ref, rk_ref,
                    w2a_ref, g2_ref, e_ref, *rest):
    z = z_ref[0]
    tm = z.shape[0]
    if per_row:
        zp = prev_ref[0]
    else:
        m2_o, sv_o, rkc_o, y0_o, gam_o, g_o, bon_o, sh_o, carry = rest
        @pl.when(pl.program_id(1) == 0)
        def _():
            carry[...] = prev_ref[0]
        row = lax.broadcasted_iota(jnp.int32, z.shape, 0)
        zp = jnp.where(row == 0, carry[...], pltpu.roll(z, 1, axis=0))
        carry[...] = z[tm - 1:tm, :]
        sh_o[0] = z[tm - 1:tm, :]
    zs = z + mu_ref[...] * (zp - z)
    r = zs[:, 0:D_A]
    k = zs[:, D_A:2 * D_A]
    v = zs[:, 2 * D_A:3 * D_A]
    zwa = zs[:, 3 * D_A:3 * D_A + LORA_W + LORA_A]
    zg = zs[:, 3 * D_A + LORA_W + LORA_A:]
    lane = lax.broadcasted_iota(jnp.int32, zwa.shape, 1)
    lwa = _dot3(jnp.where(lane < LORA_W, jnp.tanh(zwa), zwa), w2a_ref[...])
    xw = -(w0_ref[...] + lwa[:, :D_A])
    softplus = jnp.maximum(xw, 0.0) + jnp.log1p(jnp.exp(-jnp.abs(xw)))
    w_log = -softplus - 0.5
    a = _sigmoid(a0_ref[...] + lwa[:, D_A:])
    g = _dot3(_sigmoid(zg), g2_ref[...])
    e = e_ref[...]
    kkr = k * kk_ref[...]
    kn = kkr / jnp.maximum(jnp.sqrt(_dotx(kkr * kkr, e)), 1e-12)
    k2 = k * (1.0 + (a - 1.0) * ka_ref[...])
    ld = -jnp.exp(w_log)
    b = kn * a
    bon = _dotx(r * k2 * rk_ref[...], e) * v
    if per_row:
        for ref, val in zip(rest, (r, ld, k2, v, kn, b, g, bon)):
            ref[0] = val
        return
    g_o[0] = g
    bon_o[0] = bon
    c = WKV_CHUNK
    ri = lax.broadcasted_iota(jnp.int32, (c, c), 0)
    ci = lax.broadcasted_iota(jnp.int32, (c, c), 1)
    tri_s = ri > ci
    tri_i = ri >= ci
    cum = tri_i.astype(BF16)
    eye = (ri == ci).astype(F32)
    nch = tm // c
    chunks = lambda x: x.reshape(nch, c, x.shape[-1])

    def chains(x):
        return jnp.stack([x[:, :, h * HEAD_DIM:(h + 1) * HEAD_DIM]
                          for h in range(N_HEADS_A)]).reshape(N_HEADS_A * nch, c, HEAD_DIM)

    ldc = chunks(ld)
    cums = jnp.broadcast_to(cum, (nch, c, c))
    l1, l2, l3 = _split3(ldc)
    gcum = _mm(cums, l1, _BNN) + (_mm(cums, l2, _BNN) + _mm(cums, l3, _BNN))
    e_in = jnp.exp(gcum)
    e_ng = jnp.exp(-gcum)
    gam = e_in[:, c - 1:c, :]
    kh = chunks(k2) * e_ng
    bh = chunks(b) * e_ng
    outs = _wkv_local(chains(chunks(kn) * jnp.exp(gcum - ldc)), chains(chunks(r) * e_in), chains(kh), chains(bh),
                      chains(kh * gam), chains(bh * gam), chains(chunks(v)), tri_s, tri_i, eye)
    for ref, val in zip((m2_o, sv_o, rkc_o, y0_o), outs):
        ref[0] = val.reshape(N_HEADS_A, nch, c, HEAD_DIM)
    gam_o[0] = gam


def _rwkv_prep(z, prev, p, per_row):
    nb, t, w = z.shape
    tm = min(t, 512)
    row = lambda x: x.reshape(1, -1)
    w2a = jnp.zeros((LORA_W + LORA_A, 2 * D_A), F32)
    w2a = w2a.at[:LORA_W, :D_A].set(p['w2']).at[LORA_W:, D_A:].set(p['a2'])
    const = lambda shape: pl.BlockSpec(shape, lambda b, i: (0,) * len(shape))
    tile = lambda width: pl.BlockSpec((1, tm, width), lambda b, i: (b, i, 0))
    if per_row:
        prev_spec = tile(w)
        out_specs = [tile(D_A)] * 8
        out_shape = [jax.ShapeDtypeStruct((nb, t, D_A), F32)] * 8
        scratch = []
    else:
        cpt = tm // WKV_CHUNK
        prev_spec = pl.BlockSpec((1, 1, w), lambda b, i: (b, 0, 0))
        out_specs = ([pl.BlockSpec((1, N_HEADS_A, cpt, WKV_CHUNK, HEAD_DIM), lambda b, i: (b, 0, i, 0, 0))] * 4
                     + [pl.BlockSpec((1, cpt, 1, D_A), lambda b, i: (b, i, 0, 0)), tile(D_A), tile(D_A),
                        pl.BlockSpec((1, 1, w), lambda b, i: (b, 0, 0))])
        out_shape = ([jax.ShapeDtypeStruct((nb, N_HEADS_A, t // WKV_CHUNK, WKV_CHUNK, HEAD_DIM), F32)] * 4
                     + [jax.ShapeDtypeStruct((nb, t // WKV_CHUNK, 1, D_A), F32),
                        jax.ShapeDtypeStruct((nb, t, D_A), F32), jax.ShapeDtypeStruct((nb, t, D_A), F32),
                        jax.ShapeDtypeStruct((nb, 1, w), F32)])
        scratch = [pltpu.VMEM((1, w), F32)]
    return pl.pallas_call(
        functools.partial(_rwkv_prep_body, per_row),
        grid=(nb, t // tm),
        in_specs=[tile(w), prev_spec, const((1, w)), const((1, D_A)), const((1, D_A)), const((1, D_A)),
                  const((1, D_A)), const((1, D_A)), const((LORA_W + LORA_A, 2 * D_A)),
                  const((LORA_G, D_A)), const((D_A, D_A))],
        out_specs=out_specs,
        out_shape=out_shape,
        scratch_shapes=scratch,
        compiler_params=_params(("parallel", "arbitrary")),
        name="rwkv_prep",
    )(z, prev, row(p['mu_shift']), row(p['w0']), row(p['a0']), row(p['k_k']), row(p['k_a']),
      row(p['r_k']), w2a, p['g2'], _head_ones(D_A))


def _wkv_scan_body(m2_ref, sv_ref, rk_ref, y0_ref, gam_ref, g_ref, bon_ref, lw_ref, lb_ref,
                   o_ref, s_out_ref, s_sc):
    nb, _, cps = m2_ref.shape[:3]
    c = WKV_CHUNK
    step = pl.program_id(0)

    @pl.when(step == 0)
    def _():
        s_sc[...] = jnp.zeros_like(s_sc)

    merge = lambda x: x.reshape((nb * N_HEADS_A,) + x.shape[2:])
    avg = jnp.full((nb * N_HEADS_A, HEAD_DIM, HEAD_DIM), 1.0 / HEAD_DIM, BF16)
    for ic in range(cps):
        rows = slice(ic * c, (ic + 1) * c)
        s0 = s_sc[...]
        y = _dot1(merge(rk_ref[:, :, ic]), s0, _BNT) + merge(y0_ref[:, :, ic])
        gam = jnp.stack([gam_ref[bi, ic][:, h * HEAD_DIM:(h + 1) * HEAD_DIM]
                         for bi in range(nb) for h in range(N_HEADS_A)])
        s_sc[...] = s0 * gam - _dot1(s0, merge(m2_ref[:, :, ic]), _BNN) + merge(sv_ref[:, :, ic])
        mu = _dotx(y, avg, _BNN)
        yc = y - mu
        yn = yc * lax.rsqrt(_dotx(yc * yc, avg, _BNN) + LNX_EPS)
        for bi in range(nb):
            for h in range(N_HEADS_A):
                sl = slice(h * HEAD_DIM, (h + 1) * HEAD_DIM)
                o_ref[bi, rows, sl] = ((yn[bi * N_HEADS_A + h] * lw_ref[:, sl] + lb_ref[:, sl]
                                        + bon_ref[bi, rows, sl]) * g_ref[bi, rows, sl])

    @pl.when(step == pl.num_programs(0) - 1)
    def _():
        s_out_ref[...] = s_sc[...]


def _wkv_scan(m2, sv, rk, y0, gam, g, bon, lnx_w, lnx_b):
    nb, t, _ = g.shape
    cps = 4
    tb = cps * WKV_CHUNK
    mat = pl.BlockSpec((nb, N_HEADS_A, cps, WKV_CHUNK, HEAD_DIM), lambda i: (0, 0, i, 0, 0))
    tile = pl.BlockSpec((nb, tb, D_A), lambda i: (0, i, 0))
    vec = pl.BlockSpec((1, D_A), lambda i: (0, 0))
    st = (nb * N_HEADS_A, HEAD_DIM, HEAD_DIM)
    return pl.pallas_call(
        _wkv_scan_body,
        grid=(t // tb,),
        in_specs=[mat] * 4 + [pl.BlockSpec((nb, cps, 1, D_A), lambda i: (0, i, 0, 0)), tile, tile, vec, vec],
        out_specs=[tile, pl.BlockSpec(st, lambda i: (0, 0, 0))],
        out_shape=[jax.ShapeDtypeStruct((nb, t, D_A), F32), jax.ShapeDtypeStruct(st, F32)],
        scratch_shapes=[pltpu.VMEM(st, F32)],
        compiler_params=_params(("arbitrary",)),
        name="wkv_scan",
    )(m2, sv, rk, y0, gam, g, bon, lnx_w.reshape(1, D_A), lnx_b.reshape(1, D_A))


def _wkv_step_body(s_ref, r_ref, ld_ref, k_ref, v_ref, kn_ref, b_ref, g_ref, bon_ref, lw_ref, lb_ref,
                   o_ref, s_out_ref):
    s0 = s_ref[0]
    ri = lax.broadcasted_iota(jnp.int32, (HEAD_DIM, HEAD_DIM), 0)
    ci = lax.broadcasted_iota(jnp.int32, (HEAD_DIM, HEAD_DIM), 1)
    eye = (ri == ci).astype(F32)
    u = jnp.sum(s0 * kn_ref[...], axis=-1, keepdims=True)
    v_col = jnp.sum(eye * v_ref[...], axis=-1, keepdims=True)
    s1 = s0 * jnp.exp(ld_ref[...]) - u * b_ref[...] + v_col * k_ref[...]
    s_out_ref[0] = s1
    y_col = jnp.sum(s1 * r_ref[...], axis=-1, keepdims=True)
    y = jnp.sum(eye * y_col, axis=-2, keepdims=True)
    mu = jnp.mean(y, axis=-1, keepdims=True)
    var = jnp.mean(jnp.square(y - mu), axis=-1, keepdims=True)
    yn = (y - mu) * lax.rsqrt(var + LNX_EPS)
    o_ref[...] = (yn * lw_ref[...] + lb_ref[...] + bon_ref[...]) * g_ref[...]


def _wkv_step(state_wkv, layer, r, ld, k, v, kn, b, g, bon, lnx_w, lnx_b):
    n = r.shape[0]
    tb = 8
    heads = lambda x: x.reshape(n, N_HEADS_A, 1, HEAD_DIM)
    row = pl.BlockSpec((tb, N_HEADS_A, 1, HEAD_DIM), lambda i: (i, 0, 0, 0))
    vec = pl.BlockSpec((1, N_HEADS_A, 1, HEAD_DIM), lambda i: (0, 0, 0, 0))
    st_in = pl.BlockSpec((1, tb, N_HEADS_A, HEAD_DIM, HEAD_DIM), lambda i: (layer, i, 0, 0, 0))
    st_out = pl.BlockSpec((1, tb, N_HEADS_A, HEAD_DIM, HEAD_DIM), lambda i: (0, i, 0, 0, 0))
    out, s1 = pl.pallas_call(
        _wkv_step_body,
        grid=(n // tb,),
        in_specs=[st_in] + [row] * 8 + [vec, vec],
        out_specs=[row, st_out],
        out_shape=[jax.ShapeDtypeStruct((n, N_HEADS_A, 1, HEAD_DIM), F32),
                   jax.ShapeDtypeStruct((1, n, N_HEADS_A, HEAD_DIM, HEAD_DIM), F32)],
        compiler_params=_params(("parallel",)),
        name="wkv_step",
    )(state_wkv, *[heads(x) for x in (r, ld, k, v, kn, b, g, bon)],
      lnx_w.reshape(1, N_HEADS_A, 1, HEAD_DIM), lnx_b.reshape(1, N_HEADS_A, 1, HEAD_DIM))
    return out.reshape(n, D_A), s1[0]


def _head_tile(x, h, lane):
    pair = x[:, (h // 2) * 128:(h // 2 + 1) * 128]
    if h % 2:
        pair = pltpu.roll(pair, HEAD_DIM, axis=1)
    return jnp.where(lane < HEAD_DIM, pair, 0.0)


def _qk_norm(x, gain, e):
    return x * lax.rsqrt(_dotx(x * x, e) * (1.0 / HEAD_DIM) + EPS) * gain


MOBA_DEN_LANE = HEAD_DIM
MOBA_MAX_BLOCKS = 128 - HEAD_DIM
MOBA_GROUP = 8
MOBA_QBLOCKS = 2


def _moba_prep_body(q_ref, k_ref, v_ref, qg_ref, kg_ref, e_ref, qa_o, kh_o, vh_o, kn_o, km_sc):
    n = pl.program_id(1)

    @pl.when(n == 0)
    def _():
        km_sc[...] = jnp.zeros_like(km_sc)

    e = e_ref[...]
    qn = _qk_norm(q_ref[0], qg_ref[...], e)
    kn = _qk_norm(k_ref[0], kg_ref[...], e)
    v = v_ref[0]
    kn_o[0] = kn
    tq = qn.shape[0]
    lane = lax.broadcasted_iota(jnp.int32, (tq, 128), 1)
    lane_km = lax.broadcasted_iota(jnp.int32, (MOBA_MAX_BLOCKS, 128), 1)
    blk = lax.broadcasted_iota(jnp.int32, (MOBA_MAX_BLOCKS, tq), 0)
    valid = blk < n
    key_marks = jnp.where(lane == HEAD_DIM + n, 1.0, 0.0)
    val_marks = jnp.where(lane == MOBA_DEN_LANE, 1.0, 0.0)
    km = km_sc[...]
    for h in range(N_HEADS_B):
        qt = _head_tile(qn, h, lane)
        gate = jnp.where(valid, _dot3(_head_tile(km, h, lane_km), qt, _NT), -jnp.inf)
        keep = blk == n
        for _ in range(MOBA_TOPK):
            top = jnp.max(gate, axis=0, keepdims=True)
            first = jnp.min(jnp.where(gate == top, blk, 1 << 20), axis=0, keepdims=True)
            pick = blk == first
            keep = jnp.logical_or(keep, jnp.logical_and(pick, valid))
            gate = jnp.where(pick, -jnp.inf, gate)
        pen = jnp.where(keep, 0.0, NEG_BIG)
        pen = jnp.concatenate([jnp.zeros((HEAD_DIM, tq), F32), pen], axis=0).T
        qa_o[0, h] = jnp.where(lane < HEAD_DIM, qt * (HEAD_DIM ** -0.5 * LOG2_E), pen).astype(BF16)
        kh_o[0, h] = (_head_tile(kn, h, lane) + key_marks).astype(BF16)
        vh_o[0, h] = (_head_tile(v, h, lane) + val_marks).astype(BF16)
    km_sc[pl.ds(n, 1), :] = jnp.mean(kn, axis=0, keepdims=True)


def _moba_prep(q, k, v, q_gain, k_gain):
    nb, t, _ = q.shape
    nblk = t // MOBA_BLOCK
    assert nblk <= MOBA_MAX_BLOCKS and nblk % MOBA_GROUP == 0 and nblk % MOBA_QBLOCKS == 0
    tile = pl.BlockSpec((1, MOBA_BLOCK, D_B), lambda b, i: (b, i, 0))
    vec = pl.BlockSpec((1, D_B), lambda b, i: (0, 0))
    hm = pl.BlockSpec((1, N_HEADS_B, MOBA_BLOCK, 128), lambda b, i: (b, 0, i, 0))
    return pl.pallas_call(
        _moba_prep_body,
        grid=(nb, nblk),
        in_specs=[tile, tile, tile, vec, vec, pl.BlockSpec((D_B, D_B), lambda b, i: (0, 0))],
        out_specs=[hm, hm, hm, tile],
        out_shape=[jax.ShapeDtypeStruct((nb, N_HEADS_B, t, 128), BF16)] * 3
                  + [jax.ShapeDtypeStruct((nb, t, D_B), F32)],
        scratch_shapes=[pltpu.VMEM((MOBA_MAX_BLOCKS, D_B), F32)],
        compiler_params=_params(("parallel", "arbitrary")),
        name="moba_prep",
    )(q, k, v, jnp.tile(q_gain, N_HEADS_B).reshape(1, D_B), jnp.tile(k_gain, N_HEADS_B).reshape(1, D_B),
      _head_ones(D_B))


def _moba_attn_body(q_ref, k_ref, v_ref, tab_ref, o_ref, s_sc, mx_sc, acc_sc):
    first = pl.program_id(2) * MOBA_QBLOCKS
    n_groups = (first + MOBA_QBLOCKS - 1 + MOBA_GROUP) // MOBA_GROUP
    span = MOBA_GROUP * MOBA_BLOCK

    def key_rows(g):
        return pl.ds(pl.multiple_of(g * span, span), span)

    def q_rows(r):
        return slice(r * MOBA_BLOCK, (r + 1) * MOBA_BLOCK)

    mx_sc[...] = jnp.full_like(mx_sc, -jnp.inf)

    def pass1(g, carry):
        kg = k_ref[0, 0, key_rows(g), :]
        for r in range(MOBA_QBLOCKS):
            s = _mm(q_ref[0, 0, q_rows(r), :], kg, _NT)
            m = mx_sc[r]
            for i in range(MOBA_GROUP):
                cols = slice(i * MOBA_BLOCK, (i + 1) * MOBA_BLOCK)
                offset = first + r - (g * MOBA_GROUP + i)
                si = s[:, cols] + tab_ref[0, jnp.clip(offset, 0, N_BIAS_TILES - 1)]
                s_sc[g, q_rows(r), cols] = si
                m = jnp.maximum(m, si)
            mx_sc[r] = m
        return carry

    lax.fori_loop(0, n_groups, pass1, 0)
    m = [jnp.max(mx_sc[r], axis=1, keepdims=True) for r in range(MOBA_QBLOCKS)]
    acc_sc[...] = jnp.zeros_like(acc_sc)

    def pass2(g, carry):
        vg = v_ref[0, 0, key_rows(g), :]
        for r in range(MOBA_QBLOCKS):
            p = jnp.exp2(s_sc[g, q_rows(r), :] - m[r]).astype(BF16)
            acc_sc[q_rows(r), :] += _mm(p, vg)
        return carry

    lax.fori_loop(0, n_groups, pass2, 0)
    acc = acc_sc[...]
    o_ref[0, 0] = acc / acc[:, MOBA_DEN_LANE:MOBA_DEN_LANE + 1]


def _moba_attn(qa, kh, vh, bias_tiles):
    nb, nh, t, _ = qa.shape
    tq = MOBA_QBLOCKS * MOBA_BLOCK
    n_groups_max = t // (MOBA_GROUP * MOBA_BLOCK)
    return pl.pallas_call(
        _moba_attn_body,
        grid=(nh, nb, t // tq),
        in_specs=[pl.BlockSpec((1, 1, tq, 128), lambda h, b, a: (b, h, a, 0)),
                  pl.BlockSpec((1, 1, t, 128), lambda h, b, a: (b, h, 0, 0)),
                  pl.BlockSpec((1, 1, t, 128), lambda h, b, a: (b, h, 0, 0)),
                  pl.BlockSpec((1, N_BIAS_TILES, MOBA_BLOCK, MOBA_BLOCK), lambda h, b, a: (h, 0, 0, 0))],
        out_specs=pl.BlockSpec((1, 1, tq, 128), lambda h, b, a: (b, h, a, 0)),
        out_shape=jax.ShapeDtypeStruct((nb, nh, t, 128), F32),
        scratch_shapes=[pltpu.VMEM((n_groups_max, tq, MOBA_GROUP * MOBA_BLOCK), F32),
                        pltpu.VMEM((MOBA_QBLOCKS, MOBA_BLOCK, MOBA_BLOCK), F32),
                        pltpu.VMEM((tq, 128), F32)],
        compiler_params=_params(("parallel", "parallel", "arbitrary")),
        name="moba_attn",
    )(qa, kh, vh, bias_tiles)


def _qk_norm_rows_body(q_ref, k_ref, qg_ref, kg_ref, e_ref, qn_o, kn_o):
    e = e_ref[...]
    qn_o[...] = _qk_norm(q_ref[...], qg_ref[...], e)
    kn_o[...] = _qk_norm(k_ref[...], kg_ref[...], e)


def _qk_norm_rows(q, k, q_gain, k_gain):
    n = q.shape[0]
    return pl.pallas_call(
        _qk_norm_rows_body,
        out_shape=[jax.ShapeDtypeStruct((n, D_B), F32)] * 2,
        name="qk_norm_rows",
    )(q, k, jnp.tile(q_gain, N_HEADS_B).reshape(1, D_B), jnp.tile(k_gain, N_HEADS_B).reshape(1, D_B),
      _head_ones(D_B))


PAGED_SEQS = 1


def _moba_paged_body(n_pages, pt_ref, qb_ref, knb_ref, vnb_ref, tab_ref, *rest):
    o_ref = rest[2 * PAGED_SEQS * n_pages]
    for s in range(PAGED_SEQS):
        k_pages = rest[s * n_pages:(s + 1) * n_pages]
        v_pages = rest[(PAGED_SEQS + s) * n_pages:(PAGED_SEQS + s + 1) * n_pages]
        o_ref[s] = _moba_paged_one(qb_ref[s], knb_ref[s], vnb_ref[s], tab_ref, k_pages, v_pages)


def _moba_paged_one(qb, knb, vnb, tab_ref, k_pages, v_pages):
    n_pages = len(k_pages)
    page_size = qb.shape[-1]
    per_blk = MOBA_BLOCK // page_size
    n_blk = n_pages // per_blk
    scale = HEAD_DIM ** -0.5
    raw = [jnp.sum(k_pages[p][0, 0] * qb, axis=1, keepdims=True) for p in range(n_pages)]

    gates = []
    for n in range(n_blk):
        tot = raw[n * per_blk]
        for i in range(1, per_blk):
            tot = tot + raw[n * per_blk + i]
        gates.append(jnp.sum(tot, axis=2, keepdims=True) * (1.0 / MOBA_BLOCK))
    keep = [jnp.zeros(gates[0].shape, jnp.bool_)] * n_blk
    for _ in range(min(MOBA_TOPK, n_blk + 1)):
        top = functools.reduce(jnp.maximum, gates)
        found = jnp.zeros(top.shape, jnp.bool_)
        for n in range(n_blk):
            hit = jnp.logical_and(gates[n] == top, jnp.logical_not(found))
            keep[n] = jnp.logical_or(keep[n], hit)
            found = jnp.logical_or(found, hit)
            gates[n] = jnp.where(hit, -jnp.inf, gates[n])

    logits = [raw[p] * scale + tab_ref[p] + jnp.where(keep[p // per_blk], 0.0, NEG_BIG)
              for p in range(n_pages)]
    own = jnp.sum(knb * qb, axis=1, keepdims=True) * scale + tab_ref[n_pages]
    m = jnp.max(functools.reduce(jnp.maximum, logits + [own]), axis=2, keepdims=True)
    lane = lax.broadcasted_iota(jnp.int32, own.shape, 2)
    p_own = jnp.where(lane == 0, jnp.exp(own - m), 0.0)
    probs = [jnp.exp(s - m) for s in logits]
    inv_l = 1.0 / jnp.sum(functools.reduce(jnp.add, probs + [p_own]), axis=2, keepdims=True)
    acc = (p_own * inv_l) * vnb
    for p in range(n_pages):
        acc = acc + (probs[p] * inv_l) * v_pages[p][0, 0]
    h1, h2, h3 = _split3(acc.reshape(N_HEADS_B * HEAD_DIM, page_size))
    ones = jnp.ones((8, page_size), BF16)
    return (_mm(ones, h1, _NT) + (_mm(ones, h2, _NT) + _mm(ones, h3, _NT)))[0:1]


def _moba_paged(qn, kn, v, cache_kt, cache_vt, page_table, layer, bias_pages):
    n, n_pages = page_table.shape
    page_size = cache_kt.shape[-1]
    tile = (N_HEADS_B, HEAD_DIM, page_size)
    lanes = lambda x: jnp.broadcast_to(x.reshape(n, N_HEADS_B, HEAD_DIM, 1), (n,) + tile)
    assert n % PAGED_SEQS == 0
    row = pl.BlockSpec((PAGED_SEQS,) + tile, lambda b, pt: (b, 0, 0, 0))
    page = lambda s, p: pl.BlockSpec((1, 1) + tile, lambda b, pt: (layer, pt[b * PAGED_SEQS + s, p], 0, 0, 0))
    pages = [page(s, p) for s in range(PAGED_SEQS) for p in range(n_pages)]
    out = pl.pallas_call(
        functools.partial(_moba_paged_body, n_pages),
        grid_spec=pltpu.PrefetchScalarGridSpec(
            num_scalar_prefetch=1,
            grid=(n // PAGED_SEQS,),
            in_specs=[row, row, row, pl.BlockSpec(bias_pages.shape, lambda b, pt: (0, 0, 0, 0))] + pages * 2,
            out_specs=pl.BlockSpec((PAGED_SEQS, 1, D_B), lambda b, pt: (b, 0, 0))),
        out_shape=jax.ShapeDtypeStruct((n, 1, D_B), F32),
        compiler_params=_params(("arbitrary",)),
        name="moba_paged",
    )(page_table, lanes(qn), lanes(kn), lanes(v), bias_pages,
      *([cache_kt] * (PAGED_SEQS * n_pages)), *([cache_vt] * (PAGED_SEQS * n_pages)))
    return out.reshape(n, D_B)


def _gmlp_front(uv, vn_gain, e):
    ge = 0.5 * uv * (1.0 + lax.erf(uv * math.sqrt(0.5)))
    u = ge[:, :D_C]
    v = ge[:, D_C:]
    vg = v * lax.rsqrt(_dotx(v * v, e) * (1.0 / HEAD_DIM) + EPS) * vn_gain
    return u, vg


def _rms(x, gain):
    return x * lax.rsqrt(jnp.mean(x * x, axis=-1, keepdims=True) + EPS) * gain


def _gmlp_body(uv_ref, vn_ref, ws_ref, bs_ref, on_ref, e_ref, o_ref):
    u, vg = _gmlp_front(uv_ref[0], vn_ref[...], e_ref[...])
    tm = u.shape[0]
    ri = lax.broadcasted_iota(jnp.int32, (CHUNK_C, CHUNK_C), 0)
    ci = lax.broadcasted_iota(jnp.int32, (CHUNK_C, CHUNK_C), 1)
    group = lax.broadcasted_iota(jnp.int32, (CHUNK_C, D_C), 1) // HEAD_DIM
    ws = [jnp.where(ri >= ci, ws_ref[g], 0.0).astype(BF16) for g in range(N_GROUPS_C)]
    for c in range(tm // CHUNK_C):
        rows = slice(c * CHUNK_C, (c + 1) * CHUNK_C)
        vc = vg[rows].astype(BF16)
        mixed = bs_ref[...]
        for g in range(N_GROUPS_C):
            mixed = mixed + jnp.where(group == g, _mm(ws[g], vc), 0.0)
        o_ref[0, rows, :] = _rms(u[rows] * mixed, on_ref[...])


def _gmlp(uv, p):
    nb, t, _ = uv.shape
    tm = 512
    const = lambda shape: pl.BlockSpec(shape, lambda b, i: (0,) * len(shape))
    bs = jnp.repeat(p['b_s'].T, HEAD_DIM, axis=1)
    return pl.pallas_call(
        _gmlp_body,
        grid=(nb, t // tm),
        in_specs=[pl.BlockSpec((1, tm, 2 * D_C), lambda b, i: (b, i, 0)), const((1, D_C)),
                  const((N_GROUPS_C, CHUNK_C, CHUNK_C)), const((CHUNK_C, D_C)), const((1, D_C)),
                  const((D_C, D_C))],
        out_specs=pl.BlockSpec((1, tm, D_C), lambda b, i: (b, i, 0)),
        out_shape=jax.ShapeDtypeStruct((nb, t, D_C), F32),
        compiler_params=_params(("parallel", "parallel")),
        name="gmlp",
    )(uv, p['v_norm'].reshape(1, D_C), p['w_s'], bs, p['out_norm_c'].reshape(1, D_C), _head_ones(D_C))


def _gmlp_first_pos_body(uv_ref, vn_ref, w00_ref, b00_ref, on_ref, e_ref, o_ref, vg_ref):
    u, vg = _gmlp_front(uv_ref[...], vn_ref[...], e_ref[...])
    vg_ref[...] = vg
    o_ref[...] = _rms(u * (vg * w00_ref[...] + b00_ref[...]), on_ref[...])


def _gmlp_first_pos(uv, p):
    n = uv.shape[0]
    w00 = jnp.repeat(p['w_s'][:, 0, 0], HEAD_DIM).reshape(1, D_C)
    b00 = jnp.repeat(p['b_s'][:, 0], HEAD_DIM).reshape(1, D_C)
    return pl.pallas_call(
        _gmlp_first_pos_body,
        out_shape=[jax.ShapeDtypeStruct((n, D_C), F32)] * 2,
        name="gmlp_first_pos",
    )(uv, p['v_norm'].reshape(1, D_C), w00, b00, p['out_norm_c'].reshape(1, D_C), _head_ones(D_C))


def _mix_ffn_body(head_major, x_ref, oa_ref, ob_ref, oc_ref, mod_ref, nb_ref, nf_ref, wo_ref, wu_ref, wd_ref,
                  o_ref, x1_sc, h2_sc, acc_sc):
    j = pl.program_id(2)
    d = D_MODEL

    @pl.when(j == 0)
    def _():
        if head_major:
            lane = lax.broadcasted_iota(jnp.int32, ob_ref.shape[2:], 1)
            ob = jnp.concatenate(
                [jnp.where(lane < HEAD_DIM, ob_ref[0, h], pltpu.roll(ob_ref[0, h + 1], HEAD_DIM, axis=1))
                 for h in range(0, N_HEADS_B, 2)], axis=1)
        else:
            ob = ob_ref[0]
        cat = jnp.concatenate([oa_ref[0], _rms(ob, nb_ref[...]), oc_ref[0]], axis=1)
        x1 = x_ref[0] + mod_ref[0, :, 2 * d:3 * d] * _dot1(cat, wo_ref[0])
        x1_sc[...] = x1
        h2 = _rms(x1, nf_ref[...]) * (1.0 + mod_ref[0, :, 4 * d:5 * d]) + mod_ref[0, :, 3 * d:4 * d]
        h2_sc[...] = h2.astype(BF16)
        acc_sc[...] = jnp.zeros_like(acc_sc)

    up = _mm(h2_sc[...], wu_ref[0])
    acc_sc[...] += _mm(jnp.square(jnp.maximum(up, 0.0)).astype(BF16), wd_ref[0])

    @pl.when(j == pl.num_programs(2) - 1)
    def _():
        o_ref[0] = x1_sc[...] + mod_ref[0, :, 5 * d:6 * d] * acc_sc[...]


def _mix_ffn(x, oa, ob, oc, mod, p, w_out_bf, w_up_bf, w_down_bf, layer, per_row, head_major):
    nb, t, d = x.shape
    tm = min(t, 512)
    tf = 2048
    mod_rows = tm if per_row else 1
    tile = lambda width: pl.BlockSpec((1, tm, width), lambda b, i, j: (b, i, 0))
    const = lambda shape: pl.BlockSpec(shape, lambda b, i, j: (0,) * len(shape))
    if head_major:
        ob_spec = pl.BlockSpec((1, N_HEADS_B, tm, 128), lambda b, i, j: (b, 0, i, 0))
    else:
        ob_spec = tile(D_B)
    return pl.pallas_call(
        functools.partial(_mix_ffn_body, head_major),
        grid=(nb, t // tm, D_FF // tf),
        in_specs=[tile(d), tile(D_A), ob_spec, tile(D_C),
                  pl.BlockSpec((1, mod_rows, 6 * d), lambda b, i, j: (b, i if per_row else 0, 0)),
                  const((1, D_B)), const((1, d)),
                  pl.BlockSpec((1, d, d), lambda b, i, j: (layer, 0, 0)),
                  pl.BlockSpec((1, d, tf), lambda b, i, j: (layer, 0, j)),
                  pl.BlockSpec((1, tf, d), lambda b, i, j: (layer, j, 0))],
        out_specs=tile(d),
        out_shape=jax.ShapeDtypeStruct((nb, t, d), F32),
        scratch_shapes=[pltpu.VMEM((tm, d), F32), pltpu.VMEM((tm, d), BF16), pltpu.VMEM((tm, d), F32)],
        compiler_params=_params(("parallel", "parallel", "arbitrary")),
        name="mix_ffn",
    )(x, oa, ob, oc, mod, p['out_norm_b'].reshape(1, D_B), p['norm_ffn'].reshape(1, d),
      w_out_bf, w_up_bf, w_down_bf)


def _prompt_layer(x, mod, p, big, layer, bias_tiles):
    nb, t, _ = x.shape
    z, q, k, v, uv = _inproj(x, mod, p['norm_mix'], big['w_in'], layer, per_row=False)
    prep = _rwkv_prep(z, jnp.zeros((nb, 1, W_SHIFT), F32), p, per_row=False)
    shift_new = prep[7].reshape(nb, W_SHIFT)
    out_a, wkv_new = _wkv_scan(*prep[:7], p['lnx_w'], p['lnx_b'])
    wkv_new = wkv_new.reshape(nb, N_HEADS_A, HEAD_DIM, HEAD_DIM)
    qa, kh, vh, k_new = _moba_prep(q, k, v, p['q_norm'], p['k_norm'])
    out_b = _moba_attn(qa, kh, vh, bias_tiles)
    out_c = _gmlp(uv, p)
    x = _mix_ffn(x, out_a, out_b, out_c, mod, p, big['w_out'], big['w_up'], big['w_down'], layer,
                 per_row=False, head_major=True)
    return x, k_new, v, wkv_new, shift_new


def _sample_layer(x, mod, p, big, layer, bias_pages, cache_kt, cache_vt, page_table, state_wkv, shift0):
    n = x.shape[1]
    z, q, k, v, uv = _inproj(x, mod, p['norm_mix'], big['w_in'], layer, per_row=True)
    prep = _rwkv_prep(z, shift0.reshape(1, n, W_SHIFT), p, per_row=True)
    out_a, wkv_new = _wkv_step(state_wkv, layer, *[a.reshape(n, D_A) for a in prep], p['lnx_w'], p['lnx_b'])
    qn, kn = _qk_norm_rows(q.reshape(n, D_B), k.reshape(n, D_B), p['q_norm'], p['k_norm'])
    v = v.reshape(n, D_B)
    out_b = _moba_paged(qn, kn, v, cache_kt, cache_vt, page_table, layer, bias_pages)
    out_c, vg = _gmlp_first_pos(uv.reshape(n, 2 * D_C), p)
    x = _mix_ffn(x, out_a.reshape(1, n, D_A), out_b.reshape(1, n, D_B), out_c.reshape(1, n, D_C), mod, p,
                 big['w_out'], big['w_up'], big['w_down'], layer, per_row=True, head_major=False)
    return x, kn, v, wkv_new, z.reshape(n, W_SHIFT), vg


def kernel(x_prompt, x_sample, c_prompt, c_sample, cache_k, cache_v, page_table, state_wkv, state_shift, norm_mix, w_ada, b_ada, w_in, mu_shift, w0, w2, a0, a2, g2, k_k, k_a, r_k, lnx_w, lnx_b, q_norm, k_norm, rel_bias, out_norm_b, v_norm, w_s, b_s, out_norm_c, w_out, norm_ffn, w_up, w_down):
    layer_w = {
        'norm_mix': norm_mix, 'mu_shift': mu_shift, 'w0': w0, 'w2': w2, 'a0': a0, 'a2': a2, 'g2': g2,
        'k_k': k_k, 'k_a': k_a, 'r_k': r_k, 'lnx_w': lnx_w, 'lnx_b': lnx_b, 'q_norm': q_norm,
        'k_norm': k_norm, 'out_norm_b': out_norm_b, 'v_norm': v_norm, 'w_s': w_s, 'b_s': b_s,
        'out_norm_c': out_norm_c, 'norm_ffn': norm_ffn,
    }
    depth = w_in.shape[0]
    n_prompt, seq, d = x_prompt.shape
    n_dec = x_sample.shape[0]
    past_len = page_table.shape[1] * cache_k.shape[2]
    big = {'w_in': w_in.astype(BF16), 'w_out': w_out.astype(BF16),
           'w_up': w_up.astype(BF16), 'w_down': w_down.astype(BF16)}
    mod = _ada(jnp.concatenate([c_sample, c_prompt], axis=0), w_ada, b_ada)
    bias_tiles = _bias_tiles(rel_bias)
    bias_pages = _bias_pages(rel_bias, past_len, cache_k.shape[2])
    cache_kt = cache_k.transpose(0, 1, 3, 4, 2)
    cache_vt = cache_v.transpose(0, 1, 3, 4, 2)

    xp = x_prompt
    xs = x_sample.reshape(1, n_dec, d)
    outs = [[] for _ in range(9)]
    for l in range(depth):
        p = {name: arr[l] for name, arr in layer_w.items()}
        mod_s = mod[l, :n_dec].reshape(1, n_dec, 6 * d)
        mod_p = mod[l, n_dec:].reshape(n_prompt, 1, 6 * d)
        xp, k_p, v_p, wkv_p, sh_p = _prompt_layer(xp, mod_p, p, big, l, bias_tiles)
        xs, k_s, v_s, wkv_s, sh_s, vg_s = _sample_layer(
            xs, mod_s, p, big, l, bias_pages, cache_kt, cache_vt, page_table, state_wkv, state_shift[l])
        for lst, val in zip(outs, (k_p, v_p, k_s, v_s, wkv_p, wkv_s, sh_p, sh_s, vg_s)):
            lst.append(val)
    kp, vp, ks, vs, wp, ws, sp, ss, gs = (jnp.stack(o) for o in outs)
    heads = lambda a, rows: a.reshape(depth, rows, -1, N_HEADS_B, HEAD_DIM)
    return (xp, xs.reshape(n_dec, 1, d),
            heads(kp, n_prompt), heads(vp, n_prompt), heads(ks, n_dec), heads(vs, n_dec),
            wp, ws, sp, ss, gs.reshape(depth, n_dec, 1, D_C))
```

```python
import functools
import math

import numpy as np
import jax
import jax.numpy as jnp
from jax import lax
from jax.experimental import pallas as pl
from jax.experimental.pallas import tpu as pltpu

F32 = jnp.float32
BF16 = jnp.bfloat16

D_MODEL = 1024
HEAD_DIM = 64
N_HEADS_A = 6
N_HEADS_B = 6
N_GROUPS_C = 4
D_A = N_HEADS_A * HEAD_DIM
D_B = N_HEADS_B * HEAD_DIM
D_C = N_GROUPS_C * HEAD_DIM
LORA_W = 64
LORA_A = 64
LORA_G = 128
W_SHIFT = 3 * D_A + LORA_W + LORA_A + LORA_G
N_IN = W_SHIFT + 3 * D_B + 2 * D_C
IN_SEGS = (0, W_SHIFT, W_SHIFT + D_B, W_SHIFT + 2 * D_B, W_SHIFT + 3 * D_B, N_IN)
MOBA_BLOCK = 256
MOBA_TOPK = 3
CHUNK_C = 128
N_BUCKETS = 32
REL_MAX_DIST = 4096
D_FF = 4 * D_MODEL
EPS = 1e-6
LNX_EPS = 64e-5
WKV_CHUNK = 64
NEG_BIG = -1e30
LOG2_E = math.log2(math.e)
V7X_VMEM_LIMIT = 56 * 1024 * 1024


def _bucket_saturation_tiles():
    max_exact = N_BUCKETS // 2
    n = np.arange(1, 2 * REL_MAX_DIST, dtype=np.float64)
    large = max_exact + (np.log(n / max_exact) / math.log(REL_MAX_DIST / max_exact)
                         * (N_BUCKETS - max_exact)).astype(np.int64)
    first_sat = int(n[np.argmax(np.minimum(large, N_BUCKETS - 1) == N_BUCKETS - 1)])
    return -(-(first_sat + MOBA_BLOCK) // MOBA_BLOCK) + 1


N_BIAS_TILES = _bucket_saturation_tiles() + 1

_NN = (((1,), (0,)), ((), ()))
_NT = (((1,), (1,)), ((), ()))
_TN = (((0,), (0,)), ((), ()))


def _mm(a, b, dims=_NN):
    return lax.dot_general(a, b, dims, preferred_element_type=F32)


def _split2(x):
    hi = x.astype(BF16)
    lo = (x - hi.astype(F32)).astype(BF16)
    return hi, lo


def _split3(x):
    h1 = x.astype(BF16)
    r1 = x - h1.astype(F32)
    h2 = r1.astype(BF16)
    h3 = (r1 - h2.astype(F32)).astype(BF16)
    return h1, h2, h3


def _dot1(a, b, dims=_NN):
    return _mm(a.astype(BF16), b.astype(BF16), dims)


def _dot3(a, b, dims=_NN):
    ah, al = _split2(a)
    bh, bl = _split2(b)
    return _mm(ah, bh, dims) + (_mm(ah, bl, dims) + _mm(al, bh, dims))


def _dotx(a, e, dims=_NN):
    ah, al = _split2(a)
    return _mm(ah, e, dims) + _mm(al, e, dims)


def _dotx3(a, e, dims=_NN):
    h1, h2, h3 = _split3(a)
    return _mm(h1, e, dims) + (_mm(h2, e, dims) + _mm(h3, e, dims))


def _params(sem):
    return pltpu.CompilerParams(dimension_semantics=sem, vmem_limit_bytes=V7X_VMEM_LIMIT)


def _sigmoid(x):
    return jax.nn.sigmoid(x)


def _head_ones(width):
    i = np.arange(width) // HEAD_DIM
    return jnp.asarray(i[:, None] == i[None, :], dtype=BF16)


def _ada_body(c_ref, w_ref, b_ref, o_ref):
    c = c_ref[...]
    o_ref[0] = _dot3(c * _sigmoid(c), w_ref[0]) + b_ref[0]


def _ada(c_all, w_ada, b_ada):
    depth, d, n = w_ada.shape
    m = c_all.shape[0]
    tn = 1536
    return pl.pallas_call(
        _ada_body,
        grid=(depth, n // tn),
        in_specs=[pl.BlockSpec((m, d), lambda l, j: (0, 0)),
                  pl.BlockSpec((1, d, tn), lambda l, j: (l, 0, j)),
                  pl.BlockSpec((1, 1, tn), lambda l, j: (l, 0, j))],
        out_specs=pl.BlockSpec((1, m, tn), lambda l, j: (l, 0, j)),
        out_shape=jax.ShapeDtypeStruct((depth, m, n), F32),
        compiler_params=_params(("parallel", "parallel")),
        name="ada_mod",
    )(c_all, w_ada, b_ada.reshape(depth, 1, n))


def _rel_bucket(dist):
    n = jnp.maximum(dist, 0)
    max_exact = N_BUCKETS // 2
    nf = jnp.maximum(n, 1).astype(F32)
    large = max_exact + (jnp.log(nf / max_exact) / math.log(REL_MAX_DIST / max_exact)
                         * (N_BUCKETS - max_exact)).astype(jnp.int32)
    large = jnp.minimum(large, N_BUCKETS - 1)
    return jnp.where(n < max_exact, n, large)


def _bias_tiles_body(rb_ref, o_ref):
    h = pl.program_id(0)
    d = pl.program_id(1)
    ti = lax.broadcasted_iota(jnp.int32, (MOBA_BLOCK, MOBA_BLOCK), 0)
    tj = lax.broadcasted_iota(jnp.int32, (MOBA_BLOCK, MOBA_BLOCK), 1)
    dist = d * MOBA_BLOCK + ti - tj
    bucket = _rel_bucket(dist)
    acc = jnp.zeros((MOBA_BLOCK, MOBA_BLOCK), F32)
    for b in range(N_BUCKETS):
        acc = jnp.where(bucket == b, rb_ref[h * N_BUCKETS + b], acc)
    o_ref[0, 0] = jnp.where(dist >= 0, acc * LOG2_E, NEG_BIG)


def _bias_tiles(rel_bias):
    rb = rel_bias.T.reshape(-1)
    return pl.pallas_call(
        _bias_tiles_body,
        grid=(N_HEADS_B, N_BIAS_TILES),
        in_specs=[pl.BlockSpec(memory_space=pltpu.SMEM)],
        out_specs=pl.BlockSpec((1, 1, MOBA_BLOCK, MOBA_BLOCK), lambda h, d: (h, d, 0, 0)),
        out_shape=jax.ShapeDtypeStruct((N_HEADS_B, N_BIAS_TILES, MOBA_BLOCK, MOBA_BLOCK), F32),
        compiler_params=_params(("parallel", "parallel")),
        name="bias_tiles",
    )(rb)


def _bias_pages_body(past_len, rb_ref, o_ref):
    page_size = o_ref.shape[-1]
    pos = pl.program_id(0) * page_size + lax.broadcasted_iota(jnp.int32, o_ref.shape[1:], 2)
    bucket = _rel_bucket(past_len - pos)
    acc = jnp.zeros(o_ref.shape[1:], F32)
    for b in range(N_BUCKETS):
        acc = jnp.where(bucket == b, rb_ref[b], acc)
    o_ref[0] = acc


def _bias_pages(rel_bias, past_len, page_size):
    n_pages = past_len // page_size + 1
    rb = jnp.broadcast_to(rel_bias[:, :, None, None], (N_BUCKETS, N_HEADS_B, 1, page_size))
    return pl.pallas_call(
        functools.partial(_bias_pages_body, past_len),
        grid=(n_pages,),
        in_specs=[pl.BlockSpec((N_BUCKETS, N_HEADS_B, 1, page_size), lambda p: (0, 0, 0, 0))],
        out_specs=pl.BlockSpec((1, N_HEADS_B, 1, page_size), lambda p: (p, 0, 0, 0)),
        out_shape=jax.ShapeDtypeStruct((n_pages, N_HEADS_B, 1, page_size), F32),
        compiler_params=_params(("parallel",)),
        name="bias_pages",
    )(rb)


def _inproj_body(n_gmlp, x_ref, mod_ref, gain_ref, w_ref, *rest):
    gmlp_refs, out_refs = rest[:n_gmlp], rest[n_gmlp:]
    x = x_ref[0]
    d = x.shape[-1]
    y = x * lax.rsqrt(jnp.mean(x * x, axis=-1, keepdims=True) + EPS) * gain_ref[...]
    sh = mod_ref[0, :, 0:d]
    sc = mod_ref[0, :, d:2 * d]
    h = (y * (1.0 + sc) + sh).astype(BF16)
    for i, (ref, a, b) in enumerate(zip(out_refs, IN_SEGS[:-1], IN_SEGS[1:])):
        seg = _mm(h, w_ref[0, :, a:b])
        if gmlp_refs and i == len(out_refs) - 1:
            _gmlp_rows(seg, *gmlp_refs, ref)
        else:
            ref[0] = seg


def _inproj(x, mod, gain, w_in_bf, layer, per_row, gmlp_p=None):
    nb, t, d = x.shape
    tm = min(t, 512)
    mod_rows = tm if per_row else 1
    widths = [b - a for a, b in zip(IN_SEGS[:-1], IN_SEGS[1:])]
    extra, extra_specs = [], []
    if gmlp_p is not None:
        assert tm % CHUNK_C == 0
        widths[-1] = D_C
        const = lambda shape: pl.BlockSpec(shape, lambda b, i: (0,) * len(shape))
        extra = [gmlp_p['v_norm'].reshape(1, D_C), gmlp_p['w_s'], jnp.repeat(gmlp_p['b_s'].T, HEAD_DIM, axis=1),
                 gmlp_p['out_norm_c'].reshape(1, D_C), _head_ones(D_C)]
        extra_specs = [const(a.shape) for a in extra]
    return pl.pallas_call(
        functools.partial(_inproj_body, len(extra)),
        grid=(nb, t // tm),
        in_specs=[pl.BlockSpec((1, tm, d), lambda b, i: (b, i, 0)),
                  pl.BlockSpec((1, mod_rows, 6 * d), lambda b, i: (b, i if per_row else 0, 0)),
                  pl.BlockSpec((1, d), lambda b, i: (0, 0)),
                  pl.BlockSpec((1, d, N_IN), lambda b, i: (layer, 0, 0))] + extra_specs,
        out_specs=[pl.BlockSpec((1, tm, w), lambda b, i: (b, i, 0)) for w in widths],
        out_shape=[jax.ShapeDtypeStruct((nb, t, w), F32) for w in widths],
        compiler_params=_params(("parallel", "parallel")),
        name="in_proj",
    )(x, mod, gain.reshape(1, d), w_in_bf, *extra)


_BNN = (((2,), (1,)), ((0,), (0,)))
_BNT = (((2,), (2,)), ((0,), (0,)))
_BTN = (((1,), (1,)), ((0,), (0,)))


def _split_heads(x):
    return jnp.stack([x[:, h * HEAD_DIM:(h + 1) * HEAD_DIM] for h in range(x.shape[1] // HEAD_DIM)])


def _wkv_local(kt, rt, kh, bh, kb, bb, vv, tri_s, tri_i, eye):
    c = kt.shape[1]
    aa = _dot1(jnp.concatenate([kt, rt], axis=1), jnp.concatenate([kh, bh], axis=1), _BNT)
    a_kk = jnp.where(tri_s, aa[:, :c, :c], 0.0)
    a_kb = jnp.where(tri_s, aa[:, :c, c:], 0.0)
    a_rk = jnp.where(tri_i, aa[:, c:, :c], 0.0)
    a_rb = jnp.where(tri_i, aa[:, c:, c:], 0.0)
    x = -a_kb
    inv = eye + x
    span = 2
    while span < c:
        x = _dot1(x, x, _BNN)
        inv = inv + _dot1(inv, x, _BNN)
        span *= 2
    av = _dot1(jnp.concatenate([a_kk, a_rk], axis=1), vv, _BNN)
    pkw = _dot1(inv, jnp.concatenate([kt, av[:, :c]], axis=2), _BNN)
    arb = _dot1(a_rb, pkw, _BNN)
    mb = _dot1(pkw, bb, _BTN)
    rk = rt - arb[:, :, :HEAD_DIM]
    y0 = av[:, c:] - arb[:, :, HEAD_DIM:]
    sv = _dot1(vv, kb, _BTN) - mb[:, HEAD_DIM:]
    return mb[:, :HEAD_DIM], sv, rk, y0


def _rwkv_prep_body(per_row, z_ref, prev_ref, mu_ref, w0_ref, a0_ref, kk_ref, ka_ref, rk_ref,
                    w2a_ref, g2_ref, e_ref, *rest):
    z = z_ref[0]
    tm = z.shape[0]
    if per_row:
        zp = prev_ref[0]
    else:
        m2_o, sv_o, rkc_o, y0_o, gam_o, g_o, bon_o, sh_o, carry = rest
        @pl.when(pl.program_id(1) == 0)
        def _():
            carry[...] = prev_ref[0]
        row = lax.broadcasted_iota(jnp.int32, z.shape, 0)
        zp = jnp.where(row == 0, carry[...], pltpu.roll(z, 1, axis=0))
        carry[...] = z[tm - 1:tm, :]
        sh_o[0] = z[tm - 1:tm, :]
    zs = z + mu_ref[...] * (zp - z)
    r = zs[:, 0:D_A]
    k = zs[:, D_A:2 * D_A]
    v = zs[:, 2 * D_A:3 * D_A]
    zwa = zs[:, 3 * D_A:3 * D_A + LORA_W + LORA_A]
    zg = zs[:, 3 * D_A + LORA_W + LORA_A:]
    lane = lax.broadcasted_iota(jnp.int32, zwa.shape, 1)
    lwa = _dot3(jnp.where(lane < LORA_W, jnp.tanh(zwa), zwa), w2a_ref[...])
    xw = -(w0_ref[...] + lwa[:, :D_A])
    softplus = jnp.maximum(xw, 0.0) + jnp.log1p(jnp.exp(-jnp.abs(xw)))
    w_log = -softplus - 0.5
    a = _sigmoid(a0_ref[...] + lwa[:, D_A:])
    g = _dot3(_sigmoid(zg), g2_ref[...])
    e = e_ref[...]
    kkr = k * kk_ref[...]
    kn = kkr / jnp.maximum(jnp.sqrt(_dotx(kkr * kkr, e)), 1e-12)
    k2 = k * (1.0 + (a - 1.0) * ka_ref[...])
    ld = -jnp.exp(w_log)
    b = kn * a
    bon = _dotx(r * k2 * rk_ref[...], e) * v
    if per_row:
        for ref, val in zip(rest, (r, ld, k2, v, kn, b, g, bon)):
            ref[0] = val
        return
    g_o[0] = g
    bon_o[0] = bon
    c = WKV_CHUNK
    ri = lax.broadcasted_iota(jnp.int32, (c, c), 0)
    ci = lax.broadcasted_iota(jnp.int32, (c, c), 1)
    tri_s = ri > ci
    tri_i = ri >= ci
    cum = tri_i.astype(BF16)
    eye = (ri == ci).astype(F32)
    nch = tm // c
    chunks = lambda x: x.reshape(nch, c, x.shape[-1])

    def chains(x):
        return jnp.stack([x[:, :, h * HEAD_DIM:(h + 1) * HEAD_DIM]
                          for h in range(N_HEADS_A)]).reshape(N_HEADS_A * nch, c, HEAD_DIM)

    ldc = chunks(ld)
    cums = jnp.broadcast_to(cum, (nch, c, c))
    l1, l2, l3 = _split3(ldc)
    gcum = _mm(cums, l1, _BNN) + (_mm(cums, l2, _BNN) + _mm(cums, l3, _BNN))
    e_in = jnp.exp(gcum)
    e_ng = jnp.exp(-gcum)
    gam = e_in[:, c - 1:c, :]
    kh = chunks(k2) * e_ng
    bh = chunks(b) * e_ng
    outs = _wkv_local(chains(chunks(kn) * jnp.exp(gcum - ldc)), chains(chunks(r) * e_in), chains(kh), chains(bh),
                      chains(kh * gam), chains(bh * gam), chains(chunks(v)), tri_s, tri_i, eye)
    for ref, val in zip((m2_o, sv_o, rkc_o, y0_o), outs):
        ref[0] = val.reshape(N_HEADS_A, nch, c, HEAD_DIM)
    gam_o[0] = gam


def _rwkv_prep(z, prev, p, per_row):
    nb, t, w = z.shape
    tm = min(t, 512)
    row = lambda x: x.reshape(1, -1)
    w2a = jnp.zeros((LORA_W + LORA_A, 2 * D_A), F32)
    w2a = w2a.at[:LORA_W, :D_A].set(p['w2']).at[LORA_W:, D_A:].set(p['a2'])
    const = lambda shape: pl.BlockSpec(shape, lambda b, i: (0,) * len(shape))
    tile = lambda width: pl.BlockSpec((1, tm, width), lambda b, i: (b, i, 0))
    if per_row:
        prev_spec = tile(w)
        out_specs = [tile(D_A)] * 8
        out_shape = [jax.ShapeDtypeStruct((nb, t, D_A), F32)] * 8
        scratch = []
    else:
        cpt = tm // WKV_CHUNK
        prev_spec = pl.BlockSpec((1, 1, w), lambda b, i: (b, 0, 0))
        out_specs = ([pl.BlockSpec((1, N_HEADS_A, cpt, WKV_CHUNK, HEAD_DIM), lambda b, i: (b, 0, i, 0, 0))] * 4
                     + [pl.BlockSpec((1, cpt, 1, D_A), lambda b, i: (b, i, 0, 0)), tile(D_A), tile(D_A),
                        pl.BlockSpec((1, 1, w), lambda b, i: (b, 0, 0))])
        out_shape = ([jax.ShapeDtypeStruct((nb, N_HEADS_A, t // WKV_CHUNK, WKV_CHUNK, HEAD_DIM), F32)] * 4
                     + [jax.ShapeDtypeStruct((nb, t // WKV_CHUNK, 1, D_A), F32),
                        jax.ShapeDtypeStruct((nb, t, D_A), F32), jax.ShapeDtypeStruct((nb, t, D_A), F32),
                        jax.ShapeDtypeStruct((nb, 1, w), F32)])
        scratch = [pltpu.VMEM((1, w), F32)]
    return pl.pallas_call(
        functools.partial(_rwkv_prep_body, per_row),
        grid=(nb, t // tm),
        in_specs=[tile(w), prev_spec, const((1, w)), const((1, D_A)), const((1, D_A)), const((1, D_A)),
                  const((1, D_A)), const((1, D_A)), const((LORA_W + LORA_A, 2 * D_A)),
                  const((LORA_G, D_A)), const((D_A, D_A))],
        out_specs=out_specs,
        out_shape=out_shape,
        scratch_shapes=scratch,
        compiler_params=_params(("parallel", "arbitrary")),
        name="rwkv_prep",
    )(z, prev, row(p['mu_shift']), row(p['w0']), row(p['a0']), row(p['k_k']), row(p['k_a']),
      row(p['r_k']), w2a, p['g2'], _head_ones(D_A))


def _wkv_scan_body(m2_ref, sv_ref, rk_ref, y0_ref, gam_ref, g_ref, bon_ref, lw_ref, lb_ref,
                   o_ref, s_out_ref, s_sc):
    nb, _, cps = m2_ref.shape[:3]
    c = WKV_CHUNK
    step = pl.program_id(0)

    @pl.when(step == 0)
    def _():
        s_sc[...] = jnp.zeros_like(s_sc)

    merge = lambda x: x.reshape((nb * N_HEADS_A,) + x.shape[2:])
    avg = jnp.full((nb * N_HEADS_A, HEAD_DIM, HEAD_DIM), 1.0 / HEAD_DIM, BF16)
    for ic in range(cps):
        rows = slice(ic * c, (ic + 1) * c)
        s0 = s_sc[...]
        y = _dot1(merge(rk_ref[:, :, ic]), s0, _BNT) + merge(y0_ref[:, :, ic])
        gam = jnp.stack([gam_ref[bi, ic][:, h * HEAD_DIM:(h + 1) * HEAD_DIM]
                         for bi in range(nb) for h in range(N_HEADS_A)])
        s_sc[...] = s0 * gam - _dot1(s0, merge(m2_ref[:, :, ic]), _BNN) + merge(sv_ref[:, :, ic])
        mu = _dotx(y, avg, _BNN)
        yc = y - mu
        yn = yc * lax.rsqrt(_dotx(yc * yc, avg, _BNN) + LNX_EPS)
        for bi in range(nb):
            for h in range(N_HEADS_A):
                sl = slice(h * HEAD_DIM, (h + 1) * HEAD_DIM)
                o_ref[bi, rows, sl] = ((yn[bi * N_HEADS_A + h] * lw_ref[:, sl] + lb_ref[:, sl]
                                        + bon_ref[bi, rows, sl]) * g_ref[bi, rows, sl])

    @pl.when(step == pl.num_programs(0) - 1)
    def _():
        s_out_ref[...] = s_sc[...]


def _wkv_scan(m2, sv, rk, y0, gam, g, bon, lnx_w, lnx_b):
    nb, t, _ = g.shape
    cps = 4
    tb = cps * WKV_CHUNK
    mat = pl.BlockSpec((nb, N_HEADS_A, cps, WKV_CHUNK, HEAD_DIM), lambda i: (0, 0, i, 0, 0))
    tile = pl.BlockSpec((nb, tb, D_A), lambda i: (0, i, 0))
    vec = pl.BlockSpec((1, D_A), lambda i: (0, 0))
    st = (nb * N_HEADS_A, HEAD_DIM, HEAD_DIM)
    return pl.pallas_call(
        _wkv_scan_body,
        grid=(t // tb,),
        in_specs=[mat] * 4 + [pl.BlockSpec((nb, cps, 1, D_A), lambda i: (0, i, 0, 0)), tile, tile, vec, vec],
        out_specs=[tile, pl.BlockSpec(st, lambda i: (0, 0, 0))],
        out_shape=[jax.ShapeDtypeStruct((nb, t, D_A), F32), jax.ShapeDtypeStruct(st, F32)],
        scratch_shapes=[pltpu.VMEM(st, F32)],
        compiler_params=_params(("arbitrary",)),
        name="wkv_scan",
    )(m2, sv, rk, y0, gam, g, bon, lnx_w.reshape(1, D_A), lnx_b.reshape(1, D_A))


def _wkv_step_body(s_ref, r_ref, ld_ref, k_ref, v_ref, kn_ref, b_ref, g_ref, bon_ref, lw_ref, lb_ref,
                   o_ref, s_out_ref):
    s0 = s_ref[0]
    ri = lax.broadcasted_iota(jnp.int32, (HEAD_DIM, HEAD_DIM), 0)
    ci = lax.broadcasted_iota(jnp.int32, (HEAD_DIM, HEAD_DIM), 1)
    eye = (ri == ci).astype(F32)
    u = jnp.sum(s0 * kn_ref[...], axis=-1, keepdims=True)
    v_col = jnp.sum(eye * v_ref[...], axis=-1, keepdims=True)
    s1 = s0 * jnp.exp(ld_ref[...]) - u * b_ref[...] + v_col * k_ref[...]
    s_out_ref[0] = s1
    y_col = jnp.sum(s1 * r_ref[...], axis=-1, keepdims=True)
    y = jnp.sum(eye * y_col, axis=-2, keepdims=True)
    mu = jnp.mean(y, axis=-1, keepdims=True)
    var = jnp.mean(jnp.square(y - mu), axis=-1, keepdims=True)
    yn = (y - mu) * lax.rsqrt(var + LNX_EPS)
    o_ref[...] = (yn * lw_ref[...] + lb_ref[...] + bon_ref[...]) * g_ref[...]


def _wkv_step(state_wkv, layer, r, ld, k, v, kn, b, g, bon, lnx_w, lnx_b):
    n = r.shape[0]
    tb = 8
    heads = lambda x: x.reshape(n, N_HEADS_A, 1, HEAD_DIM)
    row = pl.BlockSpec((tb, N_HEADS_A, 1, HEAD_DIM), lambda i: (i, 0, 0, 0))
    vec = pl.BlockSpec((1, N_HEADS_A, 1, HEAD_DIM), lambda i: (0, 0, 0, 0))
    st_in = pl.BlockSpec((1, tb, N_HEADS_A, HEAD_DIM, HEAD_DIM), lambda i: (layer, i, 0, 0, 0))
    st_out = pl.BlockSpec((1, tb, N_HEADS_A, HEAD_DIM, HEAD_DIM), lambda i: (0, i, 0, 0, 0))
    out, s1 = pl.pallas_call(
        _wkv_step_body,
        grid=(n // tb,),
        in_specs=[st_in] + [row] * 8 + [vec, vec],
        out_specs=[row, st_out],
        out_shape=[jax.ShapeDtypeStruct((n, N_HEADS_A, 1, HEAD_DIM), F32),
                   jax.ShapeDtypeStruct((1, n, N_HEADS_A, HEAD_DIM, HEAD_DIM), F32)],
        compiler_params=_params(("parallel",)),
        name="wkv_step",
    )(state_wkv, *[heads(x) for x in (r, ld, k, v, kn, b, g, bon)],
      lnx_w.reshape(1, N_HEADS_A, 1, HEAD_DIM), lnx_b.reshape(1, N_HEADS_A, 1, HEAD_DIM))
    return out.reshape(n, D_A), s1[0]


def _head_tile(x, h, lane):
    pair = x[:, (h // 2) * 128:(h // 2 + 1) * 128]
    if h % 2:
        pair = pltpu.roll(pair, HEAD_DIM, axis=1)
    return jnp.where(lane < HEAD_DIM, pair, 0.0)


def _qk_norm(x, gain, e):
    return x * lax.rsqrt(_dotx(x * x, e) * (1.0 / HEAD_DIM) + EPS) * gain


MOBA_DEN_LANE = HEAD_DIM
MOBA_MAX_BLOCKS = 128 - HEAD_DIM
MOBA_GROUP = 8
MOBA_QBLOCKS = 2


def _moba_prep_body(q_ref, k_ref, v_ref, qg_ref, kg_ref, e_ref, qa_o, kh_o, vh_o, kn_o, km_sc):
    n = pl.program_id(1)

    @pl.when(n == 0)
    def _():
        km_sc[...] = jnp.zeros_like(km_sc)

    e = e_ref[...]
    qn = _qk_norm(q_ref[0], qg_ref[...], e)
    kn = _qk_norm(k_ref[0], kg_ref[...], e)
    v = v_ref[0]
    kn_o[0] = kn
    tq = qn.shape[0]
    lane = lax.broadcasted_iota(jnp.int32, (tq, 128), 1)
    lane_km = lax.broadcasted_iota(jnp.int32, (MOBA_MAX_BLOCKS, 128), 1)
    blk = lax.broadcasted_iota(jnp.int32, (MOBA_MAX_BLOCKS, tq), 0)
    valid = blk < n
    key_marks = jnp.where(lane == HEAD_DIM + n, 1.0, 0.0)
    val_marks = jnp.where(lane == MOBA_DEN_LANE, 1.0, 0.0)
    km = km_sc[...]
    for h in range(N_HEADS_B):
        qt = _head_tile(qn, h, lane)
        gate = jnp.where(valid, _dot3(_head_tile(km, h, lane_km), qt, _NT), -jnp.inf)
        keep = blk == n
        for _ in range(MOBA_TOPK):
            top = jnp.max(gate, axis=0, keepdims=True)
            first = jnp.min(jnp.where(gate == top, blk, 1 << 20), axis=0, keepdims=True)
            pick = blk == first
            keep = jnp.logical_or(keep, jnp.logical_and(pick, valid))
            gate = jnp.where(pick, -jnp.inf, gate)
        pen = jnp.where(keep, 0.0, NEG_BIG)
        pen = jnp.concatenate([jnp.zeros((HEAD_DIM, tq), F32), pen], axis=0).T
        qa_o[0, h] = jnp.where(lane < HEAD_DIM, qt * (HEAD_DIM ** -0.5 * LOG2_E), pen).astype(BF16)
        kh_o[0, h] = (_head_tile(kn, h, lane) + key_marks).astype(BF16)
        vh_o[0, h] = (_head_tile(v, h, lane) + val_marks).astype(BF16)
    km_sc[pl.ds(n, 1), :] = jnp.mean(kn, axis=0, keepdims=True)


def _moba_prep(q, k, v, q_gain, k_gain):
    nb, t, _ = q.shape
    nblk = t // MOBA_BLOCK
    assert nblk <= MOBA_MAX_BLOCKS and nblk % MOBA_GROUP == 0 and nblk % MOBA_QBLOCKS == 0
    tile = pl.BlockSpec((1, MOBA_BLOCK, D_B), lambda b, i: (b, i, 0))
    vec = pl.BlockSpec((1, D_B), lambda b, i: (0, 0))
    hm = pl.BlockSpec((1, N_HEADS_B, MOBA_BLOCK, 128), lambda b, i: (b, 0, i, 0))
    return pl.pallas_call(
        _moba_prep_body,
        grid=(nb, nblk),
        in_specs=[tile, tile, tile, vec, vec, pl.BlockSpec((D_B, D_B), lambda b, i: (0, 0))],
        out_specs=[hm, hm, hm, tile],
        out_shape=[jax.ShapeDtypeStruct((nb, N_HEADS_B, t, 128), BF16)] * 3
                  + [jax.ShapeDtypeStruct((nb, t, D_B), F32)],
        scratch_shapes=[pltpu.VMEM((MOBA_MAX_BLOCKS, D_B), F32)],
        compiler_params=_params(("parallel", "arbitrary")),
        name="moba_prep",
    )(q, k, v, jnp.tile(q_gain, N_HEADS_B).reshape(1, D_B), jnp.tile(k_gain, N_HEADS_B).reshape(1, D_B),
      _head_ones(D_B))


def _moba_attn_body(q_ref, k_ref, v_ref, tab_ref, o_ref, s_sc, mx_sc, acc_sc):
    first = pl.program_id(2) * MOBA_QBLOCKS
    n_groups = (first + MOBA_QBLOCKS - 1 + MOBA_GROUP) // MOBA_GROUP
    span = MOBA_GROUP * MOBA_BLOCK

    def key_rows(g):
        return pl.ds(pl.multiple_of(g * span, span), span)

    def q_rows(r):
        return slice(r * MOBA_BLOCK, (r + 1) * MOBA_BLOCK)

    mx_sc[...] = jnp.full_like(mx_sc, -jnp.inf)

    def pass1(g, carry):
        kg = k_ref[0, 0, key_rows(g), :]
        for r in range(MOBA_QBLOCKS):
            s = _mm(q_ref[0, 0, q_rows(r), :], kg, _NT)
            m = mx_sc[r]
            for i in range(MOBA_GROUP):
                cols = slice(i * MOBA_BLOCK, (i + 1) * MOBA_BLOCK)
                offset = first + r - (g * MOBA_GROUP + i)
                si = s[:, cols] + tab_ref[0, jnp.clip(offset, 0, N_BIAS_TILES - 1)]
                s_sc[g, q_rows(r), cols] = si
                m = jnp.maximum(m, si)
            mx_sc[r] = m
        return carry

    lax.fori_loop(0, n_groups, pass1, 0)
    m = [jnp.max(mx_sc[r], axis=1, keepdims=True) for r in range(MOBA_QBLOCKS)]
    acc_sc[...] = jnp.zeros_like(acc_sc)

    def pass2(g, carry):
        vg = v_ref[0, 0, key_rows(g), :]
        for r in range(MOBA_QBLOCKS):
            p = jnp.exp2(s_sc[g, q_rows(r), :] - m[r]).astype(BF16)
            acc_sc[q_rows(r), :] += _mm(p, vg)
        return carry

    lax.fori_loop(0, n_groups, pass2, 0)
    acc = acc_sc[...]
    o_ref[0, 0] = acc / acc[:, MOBA_DEN_LANE:MOBA_DEN_LANE + 1]


def _moba_attn(qa, kh, vh, bias_tiles):
    nb, nh, t, _ = qa.shape
    tq = MOBA_QBLOCKS * MOBA_BLOCK
    n_groups_max = t // (MOBA_GROUP * MOBA_BLOCK)
    return pl.pallas_call(
        _moba_attn_body,
        grid=(nh, nb, t // tq),
        in_specs=[pl.BlockSpec((1, 1, tq, 128), lambda h, b, a: (b, h, a, 0)),
                  pl.BlockSpec((1, 1, t, 128), lambda h, b, a: (b, h, 0, 0)),
                  pl.BlockSpec((1, 1, t, 128), lambda h, b, a: (b, h, 0, 0)),
                  pl.BlockSpec((1, N_BIAS_TILES, MOBA_BLOCK, MOBA_BLOCK), lambda h, b, a: (h, 0, 0, 0))],
        out_specs=pl.BlockSpec((1, 1, tq, 128), lambda h, b, a: (b, h, a, 0)),
        out_shape=jax.ShapeDtypeStruct((nb, nh, t, 128), F32),
        scratch_shapes=[pltpu.VMEM((n_groups_max, tq, MOBA_GROUP * MOBA_BLOCK), F32),
                        pltpu.VMEM((MOBA_QBLOCKS, MOBA_BLOCK, MOBA_BLOCK), F32),
                        pltpu.VMEM((tq, 128), F32)],
        compiler_params=_params(("parallel", "parallel", "arbitrary")),
        name="moba_attn",
    )(qa, kh, vh, bias_tiles)


def _qk_norm_rows_body(q_ref, k_ref, qg_ref, kg_ref, e_ref, qn_o, kn_o):
    e = e_ref[...]
    qn_o[...] = _qk_norm(q_ref[...], qg_ref[...], e)
    kn_o[...] = _qk_norm(k_ref[...], kg_ref[...], e)


def _qk_norm_rows(q, k, q_gain, k_gain):
    n = q.shape[0]
    return pl.pallas_call(
        _qk_norm_rows_body,
        out_shape=[jax.ShapeDtypeStruct((n, D_B), F32)] * 2,
        name="qk_norm_rows",
    )(q, k, jnp.tile(q_gain, N_HEADS_B).reshape(1, D_B), jnp.tile(k_gain, N_HEADS_B).reshape(1, D_B),
      _head_ones(D_B))


PAGED_SEQS = 1


def _moba_paged_body(n_pages, pt_ref, qb_ref, knb_ref, vnb_ref, tab_ref, *rest):
    o_ref = rest[2 * PAGED_SEQS * n_pages]
    for s in range(PAGED_SEQS):
        k_pages = rest[s * n_pages:(s + 1) * n_pages]
        v_pages = rest[(PAGED_SEQS + s) * n_pages:(PAGED_SEQS + s + 1) * n_pages]
        o_ref[s] = _moba_paged_one(qb_ref[s], knb_ref[s], vnb_ref[s], tab_ref, k_pages, v_pages)


def _moba_paged_one(qb, knb, vnb, tab_ref, k_pages, v_pages):
    n_pages = len(k_pages)
    page_size = qb.shape[-1]
    per_blk = MOBA_BLOCK // page_size
    n_blk = n_pages // per_blk
    scale = HEAD_DIM ** -0.5
    raw = [jnp.sum(k_pages[p][0, 0] * qb, axis=1, keepdims=True) for p in range(n_pages)]

    gates = []
    for n in range(n_blk):
        tot = raw[n * per_blk]
        for i in range(1, per_blk):
            tot = tot + raw[n * per_blk + i]
        gates.append(jnp.sum(tot, axis=2, keepdims=True) * (1.0 / MOBA_BLOCK))
    keep = [jnp.zeros(gates[0].shape, jnp.bool_)] * n_blk
    for _ in range(min(MOBA_TOPK, n_blk + 1)):
        top = functools.reduce(jnp.maximum, gates)
        found = jnp.zeros(top.shape, jnp.bool_)
        for n in range(n_blk):
            hit = jnp.logical_and(gates[n] == top, jnp.logical_not(found))
            keep[n] = jnp.logical_or(keep[n], hit)
            found = jnp.logical_or(found, hit)
            gates[n] = jnp.where(hit, -jnp.inf, gates[n])

    logits = [raw[p] * scale + tab_ref[p] + jnp.where(keep[p // per_blk], 0.0, NEG_BIG)
              for p in range(n_pages)]
    own = jnp.sum(knb * qb, axis=1, keepdims=True) * scale + tab_ref[n_pages]
    m = jnp.max(functools.reduce(jnp.maximum, logits + [own]), axis=2, keepdims=True)
    lane = lax.broadcasted_iota(jnp.int32, own.shape, 2)
    p_own = jnp.where(lane == 0, jnp.exp(own - m), 0.0)
    probs = [jnp.exp(s - m) for s in logits]
    inv_l = 1.0 / jnp.sum(functools.reduce(jnp.add, probs + [p_own]), axis=2, keepdims=True)
    acc = (p_own * inv_l) * vnb
    for p in range(n_pages):
        acc = acc + (probs[p] * inv_l) * v_pages[p][0, 0]
    h1, h2, h3 = _split3(acc.reshape(N_HEADS_B * HEAD_DIM, page_size))
    ones = jnp.ones((8, page_size), BF16)
    return (_mm(ones, h1, _NT) + (_mm(ones, h2, _NT) + _mm(ones, h3, _NT)))[0:1]


def _moba_paged(qn, kn, v, cache_kt, cache_vt, page_table, layer, bias_pages):
    n, n_pages = page_table.shape
    page_size = cache_kt.shape[-1]
    tile = (N_HEADS_B, HEAD_DIM, page_size)
    lanes = lambda x: jnp.broadcast_to(x.reshape(n, N_HEADS_B, HEAD_DIM, 1), (n,) + tile)
    assert n % PAGED_SEQS == 0
    row = pl.BlockSpec((PAGED_SEQS,) + tile, lambda b, pt: (b, 0, 0, 0))
    page = lambda s, p: pl.BlockSpec((1, 1) + tile, lambda b, pt: (layer, pt[b * PAGED_SEQS + s, p], 0, 0, 0))
    pages = [page(s, p) for s in range(PAGED_SEQS) for p in range(n_pages)]
    out = pl.pallas_call(
        functools.partial(_moba_paged_body, n_pages),
        grid_spec=pltpu.PrefetchScalarGridSpec(
            num_scalar_prefetch=1,
            grid=(n // PAGED_SEQS,),
            in_specs=[row, row, row, pl.BlockSpec(bias_pages.shape, lambda b, pt: (0, 0, 0, 0))] + pages * 2,
            out_specs=pl.BlockSpec((PAGED_SEQS, 1, D_B), lambda b, pt: (b, 0, 0))),
        out_shape=jax.ShapeDtypeStruct((n, 1, D_B), F32),
        compiler_params=_params(("arbitrary",)),
        name="moba_paged",
    )(page_table, lanes(qn), lanes(kn), lanes(v), bias_pages,
      *([cache_kt] * (PAGED_SEQS * n_pages)), *([cache_vt] * (PAGED_SEQS * n_pages)))
    return out.reshape(n, D_B)


def _gmlp_front(uv, vn_gain, e):
    ge = 0.5 * uv * (1.0 + lax.erf(uv * math.sqrt(0.5)))
    u = ge[:, :D_C]
    v = ge[:, D_C:]
    vg = v * lax.rsqrt(_dotx(v * v, e) * (1.0 / HEAD_DIM) + EPS) * vn_gain
    return u, vg


def _rms(x, gain):
    return x * lax.rsqrt(jnp.mean(x * x, axis=-1, keepdims=True) + EPS) * gain


def _gmlp_rows(uv, vn_ref, ws_ref, bs_ref, on_ref, e_ref, o_ref):
    u, vg = _gmlp_front(uv, vn_ref[...], e_ref[...])
    tm = u.shape[0]
    ri = lax.broadcasted_iota(jnp.int32, (CHUNK_C, CHUNK_C), 0)
    ci = lax.broadcasted_iota(jnp.int32, (CHUNK_C, CHUNK_C), 1)
    group = lax.broadcasted_iota(jnp.int32, (CHUNK_C, D_C), 1) // HEAD_DIM
    ws = [jnp.where(ri >= ci, ws_ref[g], 0.0).astype(BF16) for g in range(N_GROUPS_C)]
    for c in range(tm // CHUNK_C):
        rows = slice(c * CHUNK_C, (c + 1) * CHUNK_C)
        vc = vg[rows].astype(BF16)
        mixed = bs_ref[...]
        for g in range(N_GROUPS_C):
            mixed = mixed + jnp.where(group == g, _mm(ws[g], vc), 0.0)
        o_ref[0, rows, :] = _rms(u[rows] * mixed, on_ref[...])


def _gmlp_first_pos_body(uv_ref, vn_ref, w00_ref, b00_ref, on_ref, e_ref, o_ref, vg_ref):
    u, vg = _gmlp_front(uv_ref[...], vn_ref[...], e_ref[...])
    vg_ref[...] = vg
    o_ref[...] = _rms(u * (vg * w00_ref[...] + b00_ref[...]), on_ref[...])


def _gmlp_first_pos(uv, p):
    n = uv.shape[0]
    w00 = jnp.repeat(p['w_s'][:, 0, 0], HEAD_DIM).reshape(1, D_C)
    b00 = jnp.repeat(p['b_s'][:, 0], HEAD_DIM).reshape(1, D_C)
    return pl.pallas_call(
        _gmlp_first_pos_body,
        out_shape=[jax.ShapeDtypeStruct((n, D_C), F32)] * 2,
        name="gmlp_first_pos",
    )(uv, p['v_norm'].reshape(1, D_C), w00, b00, p['out_norm_c'].reshape(1, D_C), _head_ones(D_C))


def _mix_ffn_body(head_major, x_ref, oa_ref, ob_ref, oc_ref, mod_ref, nb_ref, nf_ref, wo_ref, wu_ref, wd_ref,
                  o_ref, x1_sc, h2_sc, acc_sc):
    j = pl.program_id(2)
    d = D_MODEL

    @pl.when(j == 0)
    def _():
        if head_major:
            lane = lax.broadcasted_iota(jnp.int32, ob_ref.shape[2:], 1)
            ob = jnp.concatenate(
                [jnp.where(lane < HEAD_DIM, ob_ref[0, h], pltpu.roll(ob_ref[0, h + 1], HEAD_DIM, axis=1))
                 for h in range(0, N_HEADS_B, 2)], axis=1)
        else:
            ob = ob_ref[0]
        cat = jnp.concatenate([oa_ref[0], _rms(ob, nb_ref[...]), oc_ref[0]], axis=1)
        x1 = x_ref[0] + mod_ref[0, :, 2 * d:3 * d] * _dot1(cat, wo_ref[0])
        x1_sc[...] = x1
        h2 = _rms(x1, nf_ref[...]) * (1.0 + mod_ref[0, :, 4 * d:5 * d]) + mod_ref[0, :, 3 * d:4 * d]
        h2_sc[...] = h2.astype(BF16)
        acc_sc[...] = jnp.zeros_like(acc_sc)

    up = _mm(h2_sc[...], wu_ref[0])
    acc_sc[...] += _mm(jnp.square(jnp.maximum(up, 0.0)).astype(BF16), wd_ref[0])

    @pl.when(j == pl.num_programs(2) - 1)
    def _():
        o_ref[0] = x1_sc[...] + mod_ref[0, :, 5 * d:6 * d] * acc_sc[...]


def _mix_ffn(x, oa, ob, oc, mod, p, w_out_bf, w_up_bf, w_down_bf, layer, per_row, head_major):
    nb, t, d = x.shape
    tm = min(t, 512)
    tf = 2048
    mod_rows = tm if per_row else 1
    tile = lambda width: pl.BlockSpec((1, tm, width), lambda b, i, j: (b, i, 0))
    const = lambda shape: pl.BlockSpec(shape, lambda b, i, j: (0,) * len(shape))
    if head_major:
        ob_spec = pl.BlockSpec((1, N_HEADS_B, tm, 128), lambda b, i, j: (b, 0, i, 0))
    else:
        ob_spec = tile(D_B)
    return pl.pallas_call(
        functools.partial(_mix_ffn_body, head_major),
        grid=(nb, t // tm, D_FF // tf),
        in_specs=[tile(d), tile(D_A), ob_spec, tile(D_C),
                  pl.BlockSpec((1, mod_rows, 6 * d), lambda b, i, j: (b, i if per_row else 0, 0)),
                  const((1, D_B)), const((1, d)),
                  pl.BlockSpec((1, d, d), lambda b, i, j: (layer, 0, 0)),
                  pl.BlockSpec((1, d, tf), lambda b, i, j: (layer, 0, j)),
                  pl.BlockSpec((1, tf, d), lambda b, i, j: (layer, j, 0))],
        out_specs=tile(d),
        out_shape=jax.ShapeDtypeStruct((nb, t, d), F32),
        scratch_shapes=[pltpu.VMEM((tm, d), F32), pltpu.VMEM((tm, d), BF16), pltpu.VMEM((tm, d), F32)],
        compiler_params=_params(("parallel", "parallel", "arbitrary")),
        name="mix_ffn",
    )(x, oa, ob, oc, mod, p['out_norm_b'].reshape(1, D_B), p['norm_ffn'].reshape(1, d),
      w_out_bf, w_up_bf, w_down_bf)


def _prompt_layer(x, mod, p, big, layer, bias_tiles):
    nb, t, _ = x.shape
    z, q, k, v, out_c = _inproj(x, mod, p['norm_mix'], big['w_in'], layer, per_row=False, gmlp_p=p)
    prep = _rwkv_prep(z, jnp.zeros((nb, 1, W_SHIFT), F32), p, per_row=False)
    shift_new = prep[7].reshape(nb, W_SHIFT)
    out_a, wkv_new = _wkv_scan(*prep[:7], p['lnx_w'], p['lnx_b'])
    wkv_new = wkv_new.reshape(nb, N_HEADS_A, HEAD_DIM, HEAD_DIM)
    qa, kh, vh, k_new = _moba_prep(q, k, v, p['q_norm'], p['k_norm'])
    out_b = _moba_attn(qa, kh, vh, bias_tiles)
    x = _mix_ffn(x, out_a, out_b, out_c, mod, p, big['w_out'], big['w_up'], big['w_down'], layer,
                 per_row=False, head_major=True)
    return x, k_new, v, wkv_new, shift_new


def _sample_layer(x, mod, p, big, layer, bias_pages, cache_kt, cache_vt, page_table, state_wkv, shift0):
    n = x.shape[1]
    z, q, k, v, uv = _inproj(x, mod, p['norm_mix'], big['w_in'], layer, per_row=True)
    prep = _rwkv_prep(z, shift0.reshape(1, n, W_SHIFT), p, per_row=True)
    out_a, wkv_new = _wkv_step(state_wkv, layer, *[a.reshape(n, D_A) for a in prep], p['lnx_w'], p['lnx_b'])
    qn, kn = _qk_norm_rows(q.reshape(n, D_B), k.reshape(n, D_B), p['q_norm'], p['k_norm'])
    v = v.reshape(n, D_B)
    out_b = _moba_paged(qn, kn, v, cache_kt, cache_vt, page_table, layer, bias_pages)
    out_c, vg = _gmlp_first_pos(uv.reshape(n, 2 * D_C), p)
    x = _mix_ffn(x, out_a.reshape(1, n, D_A), out_b.reshape(1, n, D_B), out_c.reshape(1, n, D_C), mod, p,
                 big['w_out'], big['w_up'], big['w_down'], layer, per_row=True, head_major=False)
    return x, kn, v, wkv_new, z.reshape(n, W_SHIFT), vg


def kernel(x_prompt, x_sample, c_prompt, c_sample, cache_k, cache_v, page_table, state_wkv, state_shift, norm_mix, w_ada, b_ada, w_in, mu_shift, w0, w2, a0, a2, g2, k_k, k_a, r_k, lnx_w, lnx_b, q_norm, k_norm, rel_bias, out_norm_b, v_norm, w_s, b_s, out_norm_c, w_out, norm_ffn, w_up, w_down):
    layer_w = {
        'norm_mix': norm_mix, 'mu_shift': mu_shift, 'w0': w0, 'w2': w2, 'a0': a0, 'a2': a2, 'g2': g2,
        'k_k': k_k, 'k_a': k_a, 'r_k': r_k, 'lnx_w': lnx_w, 'lnx_b': lnx_b, 'q_norm': q_norm,
        'k_norm': k_norm, 'out_norm_b': out_norm_b, 'v_norm': v_norm, 'w_s': w_s, 'b_s': b_s,
        'out_norm_c': out_norm_c, 'norm_ffn': norm_ffn,
    }
    depth = w_in.shape[0]
    n_prompt, seq, d = x_prompt.shape
    n_dec = x_sample.shape[0]
    past_len = page_table.shape[1] * cache_k.shape[2]
    big = {'w_in': w_in.astype(BF16), 'w_out': w_out.astype(BF16),
           'w_up': w_up.astype(BF16), 'w_down': w_down.astype(BF16)}
    mod = _ada(jnp.concatenate([c_sample, c_prompt], axis=0), w_ada, b_ada)
    bias_tiles = _bias_tiles(rel_bias)
    bias_pages = _bias_pages(rel_bias, past_len, cache_k.shape[2])
    cache_kt = cache_k.transpose(0, 1, 3, 4, 2)
    cache_vt = cache_v.transpose(0, 1, 3, 4, 2)

    xp = x_prompt
    xs = x_sample.reshape(1, n_dec, d)
    outs = [[] for _ in range(9)]
    for l in range(depth):
        p = {name: arr[l] for name, arr in layer_w.items()}
        mod_s = mod[l, :n_dec].reshape(1, n_dec, 6 * d)
        mod_p = mod[l, n_dec:].reshape(n_prompt, 1, 6 * d)
        xp, k_p, v_p, wkv_p, sh_p = _prompt_layer(xp, mod_p, p, big, l, bias_tiles)
        xs, k_s, v_s, wkv_s, sh_s, vg_s = _sample_layer(
            xs, mod_s, p, big, l, bias_pages, cache_kt, cache_vt, page_table, state_wkv, state_shift[l])
        for lst, val in zip(outs, (k_p, v_p, k_s, v_s, wkv_p, wkv_s, sh_p, sh_s, vg_s)):
            lst.append(val)
    kp, vp, ks, vs, wp, ws, sp, ss, gs = (jnp.stack(o) for o in outs)
    heads = lambda a, rows: a.reshape(depth, rows, -1, N_HEADS_B, HEAD_DIM)
    return (xp, xs.reshape(n_dec, 1, d),
            heads(kp, n_prompt), heads(vp, n_prompt), heads(ks, n_dec), heads(vs, n_dec),
            wp, ws, sp, ss, gs.reshape(depth, n_dec, 1, D_C))
```
